```python
import jax
import jax.numpy as jnp
from jax import lax
import numpy as np

D_MODEL = 1024
BATCH = 8
SEQ = 8192
DEPTH = 2

CHUNK = 64
Q_BLOCK = 128
EPS = 1e-6

SSD_HEAD_DIM = 64
SSD_WIDTH = D_MODEL // 2
SSD_HEADS = SSD_WIDTH // SSD_HEAD_DIM
SSD_GROUPS = 2
SSD_STATE = 64
CONV_WIDTH = 4
XBC_WIDTH = SSD_WIDTH + 2 * SSD_GROUPS * SSD_STATE

POOL_WINDOWS = (2, 4, 8, 16)
POOL_GROUPS = len(POOL_WINDOWS)
POOL_WIDTH = D_MODEL // 4
POOL_GROUP_DIM = POOL_WIDTH // POOL_GROUPS

SB_HEAD_DIM = 64
SB_WIDTH = D_MODEL // 4
SB_HEADS = SB_WIDTH // SB_HEAD_DIM

MIX_WIDTH = SSD_WIDTH + POOL_WIDTH + SB_WIDTH
IN_WIDTH = SSD_WIDTH + XBC_WIDTH + SSD_HEADS + POOL_WIDTH + 3 * SB_WIDTH

PEER_HEADS = 8
PEER_TOPK = 16
N_KEYS = 128
N_EXPERTS = N_KEYS * N_KEYS
PEER_QDIM = 256
PEER_HALF = PEER_QDIM // 2
PEER_BLOCK = 128

kernel_name = "hybrid_ssd_pool_stickbreak_peer"


def rms_norm(x, g):
    xf = x.astype(jnp.float32)
    y = xf * lax.rsqrt(jnp.mean(xf * xf, axis=-1, keepdims=True) + EPS)
    return (y * g.astype(jnp.float32)).astype(x.dtype)


def modulate(h, shift, scale):
    return h * (1.0 + scale[:, None, :]) + shift[:, None, :]


def causal_depthwise_conv(x, w, b):
    out = lax.conv_general_dilated(
        x, w[:, None, :], window_strides=(1,), padding=[(CONV_WIDTH - 1, 0)],
        dimension_numbers=('NWC', 'WIO', 'NWC'), feature_group_count=x.shape[-1])
    return out + b


def ssd_chunked(x, dt, a_neg, bm, cm):
    bsz, s, h, p = x.shape
    g, n = bm.shape[-2:]
    k = h // g
    nc = s // CHUNK
    xdt = (x * dt[..., None]).reshape(bsz, nc, CHUNK, g, k, p)
    bm = bm.reshape(bsz, nc, CHUNK, g, n)
    cm = cm.reshape(bsz, nc, CHUNK, g, n)
    a_cs = jnp.cumsum((dt * a_neg).reshape(bsz, nc, CHUNK, g, k), axis=2)
    seg = a_cs[:, :, :, None] - a_cs[:, :, None]
    pos = jnp.arange(CHUNK)
    tril = (pos[:, None] >= pos[None, :])[:, :, None, None]
    decay_ls = jnp.exp(jnp.where(tril, seg, -jnp.inf))
    cb = jnp.einsum('bclgn,bcsgn->bclsg', cm, bm)
    y_diag = jnp.einsum('bclsg,bclsgk,bcsgkp->bclgkp', cb, decay_ls, xdt)
    decay_states = jnp.exp(a_cs[:, :, -1:] - a_cs)
    states = jnp.einsum('bclgn,bclgk,bclgkp->bcgkpn', bm, decay_states, xdt)
    chunk_decay = jnp.exp(a_cs[:, :, -1])

    def step(carry, inp):
        s_c, d_c = inp
        return d_c[..., None, None] * carry + s_c, carry

    _, prev = lax.scan(step, jnp.zeros_like(states[:, 0]),
                       (jnp.moveaxis(states, 1, 0), jnp.moveaxis(chunk_decay, 1, 0)))
    prev = jnp.moveaxis(prev, 0, 1)
    y_off = jnp.einsum('bclgn,bcgkpn,bclgk->bclgkp', cm, prev, jnp.exp(a_cs))
    return (y_diag + y_off).reshape(bsz, s, h, p)


def multiscale_pool(xp, w_pool, scale):
    bsz, s, _ = xp.shape
    xf = xp.astype(jnp.float32).reshape(bsz, s, POOL_GROUPS, POOL_GROUP_DIM)
    cs = jnp.concatenate([jnp.zeros_like(xf[:, :1]), jnp.cumsum(xf, axis=1)], axis=1)
    t = jnp.arange(s)
    means = []
    for gi, w in enumerate(POOL_WINDOWS):
        cs_g = cs[:, :, gi]
        lo = jnp.maximum(t + 1 - w, 0)
        cnt = jnp.minimum(t + 1, w).astype(jnp.float32)
        means.append((cs_g[:, 1:] - cs_g[:, lo]) / cnt[None, :, None])
    pooled = (jnp.stack(means, axis=2) - xf).astype(xp.dtype)
    mixed = jnp.einsum('bsgc,gcd->bsgd', pooled, w_pool)
    return mixed.reshape(bsz, s, POOL_WIDTH) * scale


def stick_breaking_attention(q, k, v):
    bsz, s, h, d = q.shape
    nb = s // Q_BLOCK
    kf = k.transpose(0, 2, 1, 3)
    vf = v.transpose(0, 2, 1, 3)
    qb = q.reshape(bsz, nb, Q_BLOCK, h, d).transpose(1, 0, 3, 2, 4)
    key_pos = jnp.arange(s)
    inv_sqrt_d = d ** -0.5

    def block(args):
        q_blk, i = args
        qpos = i * Q_BLOCK + jnp.arange(Q_BLOCK)
        z = jnp.einsum('bhqd,bhsd->bhqs', q_blk, kf).astype(jnp.float32) * inv_sqrt_d
        mask = key_pos[None, :] < qpos[:, None]
        log_beta = jax.nn.log_sigmoid(z)
        log_keep = jnp.where(mask, log_beta - z, 0.0)
        later = lax.cumsum(log_keep, axis=3, reverse=True) - log_keep
        w = jnp.where(mask, jnp.exp(log_beta + later), 0.0)
        return jnp.einsum('bhqs,bhsd->bhqd', w.astype(vf.dtype), vf)

    out = lax.map(block, (qb, jnp.arange(nb)))
    return out.transpose(1, 0, 3, 2, 4).reshape(bsz, s, h * d)


def token_mixer(h, w_in, conv_w, conv_b, dt_bias, a_log, d_skip, ssd_norm_g,
                pool_w, pool_scale, w_out):
    bsz, s, _ = h.shape
    proj = h @ w_in
    cuts = [SSD_WIDTH, SSD_WIDTH + XBC_WIDTH, SSD_WIDTH + XBC_WIDTH + SSD_HEADS,
            SSD_WIDTH + XBC_WIDTH + SSD_HEADS + POOL_WIDTH]
    z, xbc, dt_raw, xp, qkv = jnp.split(proj, cuts, axis=-1)

    xbc = jax.nn.silu(causal_depthwise_conv(xbc, conv_w, conv_b))
    xs, bm, cm = jnp.split(xbc, [SSD_WIDTH, SSD_WIDTH + SSD_GROUPS * SSD_STATE], axis=-1)
    dt = jax.nn.softplus(dt_raw.astype(jnp.float32) + dt_bias.astype(jnp.float32))
    a_neg = -jnp.exp(a_log.astype(jnp.float32))
    xs4 = xs.reshape(bsz, s, SSD_HEADS, SSD_HEAD_DIM).astype(jnp.float32)
    y = ssd_chunked(xs4, dt, a_neg,
                    bm.reshape(bsz, s, SSD_GROUPS, SSD_STATE).astype(jnp.float32),
                    cm.reshape(bsz, s, SSD_GROUPS, SSD_STATE).astype(jnp.float32))
    y = y + d_skip.astype(jnp.float32)[:, None] * xs4
    y = y.reshape(bsz, s, SSD_WIDTH) * jax.nn.silu(z.astype(jnp.float32))
    yg = y.reshape(bsz, s, SSD_GROUPS, SSD_WIDTH // SSD_GROUPS)
    yg = yg * lax.rsqrt(jnp.mean(yg * yg, axis=-1, keepdims=True) + EPS)
    y_ssd = (yg.reshape(bsz, s, SSD_WIDTH) * ssd_norm_g.astype(jnp.float32)).astype(h.dtype)

    y_pool = multiscale_pool(xp, pool_w, pool_scale)

    q, k, v = jnp.split(qkv, 3, axis=-1)
    shp = (bsz, s, SB_HEADS, SB_HEAD_DIM)
    y_sb = stick_breaking_attention(q.reshape(shp), k.reshape(shp), v.reshape(shp))

    mixed = jnp.concatenate([y_ssd, y_pool.astype(h.dtype), y_sb.astype(h.dtype)], axis=-1)
    return mixed @ w_out


def peer_ffn(h, w_q, keys, u_tab, v_tab):
    bsz, s, d = h.shape
    t = bsz * s
    ht = h.reshape(t, d)
    q = (ht @ w_q).reshape(t, PEER_HEADS, 2, PEER_HALF)
    sc = jnp.einsum('thid,hikd->thik', q, keys).astype(jnp.float32)
    top_s, top_i = lax.top_k(sc, PEER_TOPK)
    cand = top_s[:, :, 0, :, None] + top_s[:, :, 1, None, :]
    best_s, best_pos = lax.top_k(cand.reshape(t, PEER_HEADS, PEER_TOPK * PEER_TOPK), PEER_TOPK)
    i1 = jnp.take_along_axis(top_i[:, :, 0], best_pos // PEER_TOPK, axis=-1)
    i2 = jnp.take_along_axis(top_i[:, :, 1], best_pos % PEER_TOPK, axis=-1)
    experts = i1 * N_KEYS + i2
    gates = jax.nn.softmax(best_s, axis=-1).astype(h.dtype)
    nb = t // PEER_BLOCK

    def block(args):
        hb, eb, gb = args
        act = jnp.einsum('thkd,td->thk', u_tab[eb], hb)
        coef = gb * jax.nn.gelu(act, approximate=False)
        return jnp.einsum('thk,thkd->td', coef, v_tab[eb])

    out = lax.map(block, (ht.reshape(nb, PEER_BLOCK, d),
                          experts.reshape(nb, PEER_BLOCK, PEER_HEADS, PEER_TOPK),
                          gates.reshape(nb, PEER_BLOCK, PEER_HEADS, PEER_TOPK)))
    return out.reshape(bsz, s, d)


def setup_inputs(seed: int = 0):
    key = jax.random.key(seed)
    ks = jax.random.split(key, 22)
    L, D = DEPTH, D_MODEL
    f32 = jnp.float32
    nrm = jax.random.normal
    dt0 = jnp.exp(jax.random.uniform(ks[9], (L, SSD_HEADS), f32,
                                     float(np.log(1e-3)), float(np.log(1e-1))))
    return {
        'x': nrm(ks[0], (BATCH, SEQ, D), f32),
        'c': nrm(ks[1], (BATCH, D), f32),
        'ada_w': nrm(ks[2], (L, D, 6 * D), f32) * (0.5 * D ** -0.5),
        'ada_b': nrm(ks[3], (L, 6 * D), f32) * 0.01,
        'mix_norm_g': 1.0 + 0.05 * nrm(ks[4], (L, D), f32),
        'ffn_norm_g': 1.0 + 0.05 * nrm(ks[5], (L, D), f32),
        'w_in': nrm(ks[6], (L, D, IN_WIDTH), f32) * D ** -0.5,
        'conv_w': nrm(ks[7], (L, CONV_WIDTH, XBC_WIDTH), f32) * CONV_WIDTH ** -0.5,
        'conv_b': nrm(ks[8], (L, XBC_WIDTH), f32) * 0.01,
        'dt_bias': dt0 + jnp.log(-jnp.expm1(-dt0)),
        'a_log': jnp.log(jax.random.uniform(ks[10], (L, SSD_HEADS), f32, 1.0, 16.0)),
        'd_skip': 1.0 + 0.05 * nrm(ks[11], (L, SSD_HEADS), f32),
        'ssd_norm_g': 1.0 + 0.05 * nrm(ks[12], (L, SSD_WIDTH), f32),
        'pool_w': nrm(ks[13], (L, POOL_GROUPS, POOL_GROUP_DIM, POOL_GROUP_DIM), f32) * POOL_GROUP_DIM ** -0.5,
        'pool_scale': 1.0 + 0.1 * nrm(ks[14], (L, POOL_WIDTH), f32),
        'w_out': nrm(ks[15], (L, MIX_WIDTH, D), f32) * MIX_WIDTH ** -0.5,
        'peer_wq': nrm(ks[16], (L, D, PEER_HEADS * PEER_QDIM), f32) * D ** -0.5,
        'peer_keys': nrm(ks[17], (L, PEER_HEADS, 2, N_KEYS, PEER_HALF), f32) * PEER_HALF ** -0.5,
        'peer_u': nrm(ks[18], (L, N_EXPERTS, D), f32) * D ** -0.5,
        'peer_v': nrm(ks[19], (L, N_EXPERTS, D), f32),
        'final_norm_g': 1.0 + 0.05 * nrm(ks[20], (D,), f32),
    }


def reference(x, c, ada_w, ada_b, mix_norm_g, ffn_norm_g, w_in, conv_w, conv_b, dt_bias,
              a_log, d_skip, ssd_norm_g, pool_w, pool_scale, w_out, peer_wq, peer_keys,
              peer_u, peer_v, final_norm_g):
    cond = jax.nn.silu(c)
    for l in range(DEPTH):
        mod = cond @ ada_w[l] + ada_b[l]
        sh1, sc1, g1, sh2, sc2, g2 = jnp.split(mod, 6, axis=-1)
        h = modulate(rms_norm(x, mix_norm_g[l]), sh1, sc1)
        mix = token_mixer(h, w_in[l], conv_w[l], conv_b[l], dt_bias[l], a_log[l], d_skip[l],
                          ssd_norm_g[l], pool_w[l], pool_scale[l], w_out[l])
        x = x + g1[:, None, :] * mix
        h = modulate(rms_norm(x, ffn_norm_g[l]), sh2, sc2)
        x = x + g2[:, None, :] * peer_ffn(h, peer_wq[l], peer_keys[l], peer_u[l], peer_v[l])
    return rms_norm(x, final_norm_g)
```

```python
import functools

import jax
import jax.numpy as jnp
from jax import lax
from jax.experimental import pallas as pl
from jax.experimental.pallas import tpu as pltpu

F32 = jnp.float32
BF16 = jnp.bfloat16
EPS = 1e-6
HIGHEST = lax.Precision.HIGHEST

SSD_HEAD_DIM = 64
SSD_GROUPS = 2
SSD_STATE = 64
CONV_WIDTH = 4
POOL_WINDOWS = (2, 4, 8, 16)
SB_HEAD_DIM = 64
PEER_HEADS = 8
PEER_TOPK = 16
N_KEYS = 128
PEER_HALF = 128

LANES = 128
VMEM_LIMIT = 56 * 1024 * 1024

TM_IN = 512
SSD_CHUNK = 128
HALO = 16
TM_POOL = 512
SB_BLOCK = 128
SB_SKIP = 120.0
TT_TOPK = 128
TG_GATE = 256
TB_PEER = 512
EB_PEER = 2048


def _cparams(sem):
    return pltpu.CompilerParams(dimension_semantics=sem, vmem_limit_bytes=VMEM_LIMIT)


def _nt_dot(a, b):
    return lax.dot_general(a, b, (((1,), (1,)), ((), ())), preferred_element_type=F32)


def _softplus(x):
    return jnp.maximum(x, 0.0) + jnp.log1p(jnp.exp(-jnp.abs(x)))


def _silu(x):
    return x * jax.nn.sigmoid(x)


def _mod_kernel(c_ref, w_ref, b_ref, o_ref):
    cond = _silu(c_ref[...])
    o_ref[0] = jnp.dot(cond, w_ref[0], preferred_element_type=F32, precision=HIGHEST) + b_ref[0]


def _mod_call(c, ada_w, ada_b):
    nl, d, n6 = ada_w.shape
    bsz = c.shape[0]
    tn = 1536
    return pl.pallas_call(
        _mod_kernel,
        grid=(nl, n6 // tn),
        in_specs=[pl.BlockSpec((bsz, d), lambda l, j: (0, 0)),
                  pl.BlockSpec((1, d, tn), lambda l, j: (l, 0, j)),
                  pl.BlockSpec((1, 1, tn), lambda l, j: (l, 0, j))],
        out_specs=pl.BlockSpec((1, bsz, tn), lambda l, j: (l, 0, j)),
        out_shape=jax.ShapeDtypeStruct((nl, bsz, n6), F32),
        compiler_params=_cparams(("arbitrary", "arbitrary")),
        name="adaln_mod",
    )(c, ada_w, ada_b.reshape(nl, 1, n6))


def _norm_mod(x, g, sh, sc):
    ms = jnp.mean(x * x, axis=-1, keepdims=True)
    y = x * lax.rsqrt(ms + EPS) * g
    return y * (1.0 + sc) + sh


def _in_kernel(x_ref, sh_ref, sc_ref, g_ref, w_ref, z_ref, xbc_ref, xp_ref, q_ref, k_ref, v_ref,
               dt_ref, *, cuts):
    h = _norm_mod(x_ref[0], g_ref[...], sh_ref[0], sc_ref[0]).astype(BF16)
    outs = (z_ref, xbc_ref, xp_ref, q_ref, k_ref, v_ref, dt_ref)
    for o_ref, (a, b) in zip(outs, cuts):
        o_ref[0] = jnp.dot(h, w_ref[:, a:b], preferred_element_type=F32).astype(o_ref.dtype)


def _in_call(x, sh, sc, g, w_cat, widths):
    bsz, s, d = x.shape
    tm = min(TM_IN, s)
    cuts, a = [], 0
    for w in widths:
        cuts.append((a, a + w))
        a += w
    dtypes = (BF16, BF16, BF16, BF16, BF16, BF16, F32)
    tok = lambda w: pl.BlockSpec((1, tm, w), lambda b, i: (b, i, 0))
    vec = pl.BlockSpec((1, 1, d), lambda b, i: (b, 0, 0))
    return pl.pallas_call(
        functools.partial(_in_kernel, cuts=tuple(cuts)),
        grid=(bsz, s // tm),
        in_specs=[tok(d), vec, vec,
                  pl.BlockSpec((1, d), lambda b, i: (0, 0)),
                  pl.BlockSpec(w_cat.shape, lambda b, i: (0, 0))],
        out_specs=[tok(w) for w in widths],
        out_shape=[jax.ShapeDtypeStruct((bsz, s, w), dt) for w, dt in zip(widths, dtypes)],
        compiler_params=_cparams(("arbitrary", "arbitrary")),
        name="in_proj",
    )(x, sh, sc, g, w_cat)


def _ssd_kernel(xbc_ref, halo_ref, dt_ref, z_ref, cw_ref, cb_ref, dtb_ref, alog_ref, expand_ref,
                dsk_ref, ng_ref, y_ref, state_ref, *, chunk, width, gn):
    i = pl.program_id(1)
    L = chunk
    heads = width // SSD_HEAD_DIM
    hpg = heads // SSD_GROUPS
    half = width // SSD_GROUPS

    @pl.when(i == 0)
    def _():
        state_ref[...] = jnp.zeros_like(state_ref)

    hal = jnp.where(i > 0, halo_ref[0].astype(F32), 0.0)
    full = jnp.concatenate([hal, xbc_ref[0].astype(F32)], axis=0)
    cw = cw_ref[...]
    acc = full * cw[CONV_WIDTH - 1:CONV_WIDTH]
    for j in range(1, CONV_WIDTH):
        acc = acc + pltpu.roll(full, j, 0) * cw[CONV_WIDTH - 1 - j:CONV_WIDTH - j]
    xbc = _silu(acc[HALO:] + cb_ref[...])
    xs = xbc[:, :width]
    bm = xbc[:, width:width + gn]
    cm = xbc[:, width + gn:width + 2 * gn]

    dt = _softplus(dt_ref[0] + dtb_ref[...])
    a = dt * (-jnp.exp(alog_ref[...]))
    row_i = lax.broadcasted_iota(jnp.int32, (L, L), 0)
    col_i = lax.broadcasted_iota(jnp.int32, (L, L), 1)
    tril = row_i >= col_i
    a_cs = jnp.dot(tril.astype(F32), a, preferred_element_type=F32, precision=HIGHEST)
    a_cs_t = a_cs.T
    expand = expand_ref[...]
    dt_x = jnp.dot(dt, expand, preferred_element_type=F32, precision=HIGHEST)
    acs_x = jnp.dot(a_cs, expand, preferred_element_type=F32, precision=HIGHEST)
    alast_x = acs_x[L - 1:L, :]
    xdt = xs * dt_x

    bm_b = bm.astype(BF16)
    cm_b = cm.astype(BF16)
    lane_g = lax.broadcasted_iota(jnp.int32, (L, gn), 1) // SSD_STATE
    lane_h = lax.broadcasted_iota(jnp.int32, (L, half), 1) // SSD_HEAD_DIM

    state = state_ref[...]
    y_off = jnp.dot(cm_b, state.astype(BF16), preferred_element_type=F32) * jnp.exp(acs_x)

    y_halves = []
    for g in range(SSD_GROUPS):
        cb = _nt_dot(jnp.where(lane_g == g, cm_b, jnp.zeros_like(cm_b)), bm_b)
        xdt_g = xdt[:, g * half:(g + 1) * half]
        yh = jnp.zeros((L, half), F32)
        for hh in range(hpg):
            h = g * hpg + hh
            seg = a_cs[:, h:h + 1] - a_cs_t[h:h + 1, :]
            dec = jnp.exp(jnp.where(tril, seg, -1e30))
            m = (cb * dec).astype(BF16)
            rhs = jnp.where(lane_h == hh, xdt_g, 0.0).astype(BF16)
            yh = yh + jnp.dot(m, rhs, preferred_element_type=F32)
        y_halves.append(yh)

    ds_x = jnp.exp(alast_x - acs_x)
    upd = jnp.dot(bm.T.astype(BF16), (xdt * ds_x).astype(BF16), preferred_element_type=F32)
    srow = lax.broadcasted_iota(jnp.int32, (gn, width), 0) // SSD_STATE
    scol = lax.broadcasted_iota(jnp.int32, (gn, width), 1) // half
    state_ref[...] = jnp.exp(alast_x) * state + jnp.where(srow == scol, upd, 0.0)

    zg = _silu(z_ref[0].astype(F32))
    for g in range(SSD_GROUPS):
        sl = slice(g * half, (g + 1) * half)
        yg = (y_halves[g] + y_off[:, sl] + dsk_ref[:, sl] * xs[:, sl]) * zg[:, sl]
        ms = jnp.mean(yg * yg, axis=-1, keepdims=True)
        y_ref[0, :, sl] = (yg * lax.rsqrt(ms + EPS) * ng_ref[:, sl]).astype(y_ref.dtype)


def _ssd_call(xbc, dt, z, conv_w, conv_b, dt_bias, a_log, d_skip, ssd_norm_g):
    bsz, s, c = xbc.shape
    width = z.shape[-1]
    gn = SSD_GROUPS * SSD_STATE
    heads = width // SSD_HEAD_DIM
    L = min(SSD_CHUNK, s)
    pad = lambda v: jnp.zeros((1, LANES), F32).at[0, :heads].set(v)
    expand = (jnp.arange(LANES)[:, None] == (jnp.arange(width)[None, :] // SSD_HEAD_DIM)).astype(F32)
    dsk = jnp.repeat(d_skip, SSD_HEAD_DIM).reshape(1, width)
    hb = L // HALO
    const = lambda shp: pl.BlockSpec(shp, lambda b, i: (0, 0))
    return pl.pallas_call(
        functools.partial(_ssd_kernel, chunk=L, width=width, gn=gn),
        grid=(bsz, s // L),
        in_specs=[pl.BlockSpec((1, L, c), lambda b, i: (b, i, 0)),
                  pl.BlockSpec((1, HALO, c), lambda b, i: (b, jnp.maximum(i * hb - 1, 0), 0)),
                  pl.BlockSpec((1, L, LANES), lambda b, i: (b, i, 0)),
                  pl.BlockSpec((1, L, width), lambda b, i: (b, i, 0)),
                  const((CONV_WIDTH, c)), const((1, c)), const((1, LANES)), const((1, LANES)),
                  const((LANES, width)), const((1, width)), const((1, width))],
        out_specs=pl.BlockSpec((1, L, width), lambda b, i: (b, i, 0)),
        out_shape=jax.ShapeDtypeStruct((bsz, s, width), BF16),
        scratch_shapes=[pltpu.VMEM((gn, width), F32)],
        compiler_params=_cparams(("arbitrary", "arbitrary")),
        name="ssd_mixer",
    )(xbc, xbc, dt, z, conv_w, conv_b.reshape(1, c), pad(dt_bias), pad(a_log), expand, dsk,
      ssd_norm_g.reshape(1, width))


def _pool_kernel(xp_ref, halo_ref, w_ref, sc_ref, y_ref, *, tm, gdim):
    i = pl.program_id(1)
    hal = jnp.where(i > 0, halo_ref[0].astype(F32), 0.0)
    x = xp_ref[0].astype(F32)
    full = jnp.concatenate([hal, x], axis=0)
    lane_g = lax.broadcasted_iota(jnp.int32, x.shape, 1) // gdim
    tpos = i * tm + lax.broadcasted_iota(jnp.int32, x.shape, 0)
    win_sum = jnp.zeros_like(x)
    cnt = jnp.ones_like(x)
    s = full
    span = 1
    for gi, w in enumerate(POOL_WINDOWS):
        while span < w:
            s = s + pltpu.roll(s, span, 0)
            span *= 2
        win_sum = jnp.where(lane_g == gi, s[HALO:], win_sum)
        cnt = jnp.where(lane_g == gi, jnp.minimum(tpos + 1, w).astype(F32), cnt)
    pooled = (win_sum / cnt - x).astype(BF16)
    y = jnp.dot(pooled, w_ref[...], preferred_element_type=F32) * sc_ref[...]
    y_ref[0] = y.astype(y_ref.dtype)


def _pool_call(xp, pool_w, pool_scale):
    bsz, s, w = xp.shape
    ng, gdim, _ = pool_w.shape
    assert POOL_WINDOWS[-1] <= HALO and all(b == 2 * a for a, b in zip(POOL_WINDOWS, POOL_WINDOWS[1:]))
    tm = min(TM_POOL, s)
    wbd = jnp.zeros((w, w), F32)
    for g in range(ng):
        wbd = wbd.at[g * gdim:(g + 1) * gdim, g * gdim:(g + 1) * gdim].set(pool_w[g])
    hb = tm // HALO
    return pl.pallas_call(
        functools.partial(_pool_kernel, tm=tm, gdim=gdim),
        grid=(bsz, s // tm),
        in_specs=[pl.BlockSpec((1, tm, w), lambda b, i: (b, i, 0)),
                  pl.BlockSpec((1, HALO, w), lambda b, i: (b, jnp.maximum(i * hb - 1, 0), 0)),
                  pl.BlockSpec((w, w), lambda b, i: (0, 0)),
                  pl.BlockSpec((1, w), lambda b, i: (0, 0))],
        out_specs=pl.BlockSpec((1, tm, w), lambda b, i: (b, i, 0)),
        out_shape=jax.ShapeDtypeStruct((bsz, s, w), BF16),
        compiler_params=_cparams(("arbitrary", "arbitrary")),
        name="pool_mixer",
    )(xp, xp, wbd.astype(BF16), pool_scale.reshape(1, w))


def _sb_kernel(q_ref, k_ref, v_ref, y_ref, qm_ref, acc_ref, carry_ref, *, blk, heads):
    i = pl.program_id(1)
    T = blk
    w = q_ref.shape[-1]
    lane_h = lax.broadcasted_iota(jnp.int32, (T, w), 1) // SB_HEAD_DIM
    q = q_ref[0] * (SB_HEAD_DIM ** -0.5)
    for h in range(heads):
        qm_ref[h] = jnp.where(lane_h == h, q, jnp.zeros_like(q))
    acc_ref[...] = jnp.zeros_like(acc_ref)
    carry_ref[...] = jnp.zeros_like(carry_ref)
    row_i = lax.broadcasted_iota(jnp.int32, (T, T), 0)
    col_i = lax.broadcasted_iota(jnp.int32, (T, T), 1)
    causal = col_i < row_i
    suffix = (row_i > col_i).astype(BF16)

    def process(j, diag):
        start = pl.multiple_of(j * T, T)
        ks = k_ref[0, pl.ds(start, T), :]
        vs = v_ref[0, pl.ds(start, T), :]
        for h in range(heads):
            z = _nt_dot(qm_ref[h], ks)
            lk = -_softplus(z)
            if diag:
                lk = jnp.where(causal, lk, 0.0)
            hi = lk.astype(BF16)
            lo = (lk - hi.astype(F32)).astype(BF16)
            later = (jnp.dot(hi, suffix, preferred_element_type=F32)
                     + jnp.dot(lo, suffix, preferred_element_type=F32))
            c = carry_ref[h]
            wgt = jnp.exp(z + lk + later + c)
            if diag:
                wgt = jnp.where(causal, wgt, 0.0)
            pv = jnp.dot(wgt.astype(BF16), vs, preferred_element_type=F32)
            acc_ref[...] += jnp.where(lane_h == h, pv, 0.0)
            carry_ref[h] = c + jnp.sum(lk, axis=1, keepdims=True)

    def alive():
        return (jnp.max(carry_ref[...]) > -SB_SKIP).astype(jnp.int32)

    process(i, True)

    def cond(st):
        return jnp.logical_and(st[0] >= 0, st[1] > 0)

    def body(st):
        process(st[0], False)
        return st[0] - 1, alive()

    lax.while_loop(cond, body, (i - 1, alive()))
    y_ref[0] = acc_ref[...].astype(y_ref.dtype)


def _sb_call(q, k, v):
    bsz, s, w = q.shape
    heads = w // SB_HEAD_DIM
    T = min(SB_BLOCK, s)
    return pl.pallas_call(
        functools.partial(_sb_kernel, blk=T, heads=heads),
        grid=(bsz, s // T),
        in_specs=[pl.BlockSpec((1, T, w), lambda b, i: (b, i, 0)),
                  pl.BlockSpec((1, s, w), lambda b, i: (b, 0, 0)),
                  pl.BlockSpec((1, s, w), lambda b, i: (b, 0, 0))],
        out_specs=pl.BlockSpec((1, T, w), lambda b, i: (b, i, 0)),
        out_shape=jax.ShapeDtypeStruct((bsz, s, w), BF16),
        scratch_shapes=[pltpu.VMEM((heads, T, w), BF16),
                        pltpu.VMEM((T, w), F32),
                        pltpu.VMEM((heads, T, 1), F32)],
        compiler_params=_cparams(("arbitrary", "arbitrary")),
        name="stick_breaking",
    )(q, k, v)


def _out_kernel(ys_ref, yp_ref, yb_ref, x_ref, g1_ref, sh_ref, sc_ref, ng_ref, wo_ref, wq_ref,
                keys_ref, x1_ref, h2_ref, sct_ref, *, cuts):
    mix = jnp.zeros(x_ref.shape[1:], F32)
    for y_ref, (a, b) in zip((ys_ref, yp_ref, yb_ref), cuts):
        mix = mix + jnp.dot(y_ref[0], wo_ref[a:b, :], preferred_element_type=F32)
    x1 = x_ref[0] + g1_ref[0] * mix
    x1_ref[0] = x1
    h2 = _norm_mod(x1, ng_ref[...], sh_ref[0], sc_ref[0]).astype(BF16)
    h2_ref[0] = h2
    qb = jnp.dot(h2, wq_ref[...], preferred_element_type=F32).astype(BF16)
    for lst in range(keys_ref.shape[0]):
        sct_ref[lst] = _nt_dot(keys_ref[lst], qb[:, lst * PEER_HALF:(lst + 1) * PEER_HALF])


def _out_call(ys, yp, yb, x, g1, sh, sc, ng, w_out, wq, keys):
    bsz, s, d = x.shape
    tm = min(TM_IN, s)
    nlist = keys.shape[0]
    cuts, a = [], 0
    for y in (ys, yp, yb):
        cuts.append((a, a + y.shape[-1]))
        a += y.shape[-1]
    tok = lambda w: pl.BlockSpec((1, tm, w), lambda b, i: (b, i, 0))
    vec = pl.BlockSpec((1, 1, d), lambda b, i: (b, 0, 0))
    full2 = lambda arr: pl.BlockSpec(arr.shape, lambda b, i: (0,) * arr.ndim)
    nblk = s // tm
    return pl.pallas_call(
        functools.partial(_out_kernel, cuts=tuple(cuts)),
        grid=(bsz, nblk),
        in_specs=[tok(ys.shape[-1]), tok(yp.shape[-1]), tok(yb.shape[-1]), tok(d), vec, vec, vec,
                  pl.BlockSpec((1, d), lambda b, i: (0, 0)), full2(w_out), full2(wq), full2(keys)],
        out_specs=[tok(d), tok(d),
                   pl.BlockSpec((nlist, N_KEYS, tm), lambda b, i: (0, 0, b * nblk + i))],
        out_shape=[jax.ShapeDtypeStruct((bsz, s, d), F32),
                   jax.ShapeDtypeStruct((bsz, s, d), BF16),
                   jax.ShapeDtypeStruct((nlist, N_KEYS, bsz * s), F32)],
        compiler_params=_cparams(("arbitrary", "arbitrary")),
        name="out_proj_peer_query",
    )(ys, yp, yb, x, g1, sh, sc, ng, w_out, wq, keys)


def _top_rows(s, k):
    nrows = s.shape[0]
    rows = lax.broadcasted_iota(jnp.int32, s.shape, 0).astype(F32)
    vals, idxs = [], []
    for _ in range(k):
        m = jnp.max(s, axis=0, keepdims=True)
        idx = jnp.min(jnp.where(s == m, rows, float(nrows)), axis=0, keepdims=True)
        vals.append(m)
        idxs.append(idx)
        s = jnp.where(rows == idx, -jnp.inf, s)
    return jnp.concatenate(vals, axis=0), jnp.concatenate(idxs, axis=0)


def _pick_rows(tab, sel):
    k = tab.shape[0]
    out = jnp.zeros_like(sel)
    for j in range(k):
        out = out + jnp.where(sel == float(j), tab[j:j + 1, :], 0.0)
    return out


def _topk_kernel(sct_ref, i1_ref, i2_ref, gt_ref, p1_ref, p2_ref, pg_ref):
    K = PEER_TOPK

    def head(h, _):
        v1, x1 = _top_rows(sct_ref[2 * h], K)
        v2, x2 = _top_rows(sct_ref[2 * h + 1], K)
        cand = jnp.concatenate([v1[j:j + 1, :] + v2 for j in range(K)], axis=0)
        best, pos = _top_rows(cand, K)
        j1 = jnp.floor(pos * (1.0 / K))
        j2 = pos - j1 * K
        e = jnp.exp(best - best[0:1, :])
        gate = e / jnp.sum(e, axis=0, keepdims=True)
        r0 = pl.multiple_of(h * K, K)
        p1_ref[pl.ds(r0, K), :] = _pick_rows(x1, j1)
        p2_ref[pl.ds(r0, K), :] = _pick_rows(x2, j2)
        pg_ref[pl.ds(r0, K), :] = gate
        return 0

    lax.fori_loop(0, PEER_HEADS, head, 0)
    i1_ref[...] = p1_ref[...].T
    i2_ref[...] = p2_ref[...].T
    gt_ref[...] = pg_ref[...].T


def _topk_call(sct):
    nlist, nk, t = sct.shape
    tt = min(TT_TOPK, t)
    npair = PEER_HEADS * PEER_TOPK
    out = pl.BlockSpec((tt, npair), lambda i: (i, 0))
    return pl.pallas_call(
        _topk_kernel,
        grid=(t // tt,),
        in_specs=[pl.BlockSpec((nlist, nk, tt), lambda i: (0, 0, i))],
        out_specs=[out, out, out],
        out_shape=[jax.ShapeDtypeStruct((t, npair), F32)] * 3,
        scratch_shapes=[pltpu.VMEM((npair, tt), F32)] * 3,
        compiler_params=_cparams(("arbitrary",)),
        name="peer_topk",
    )(sct)


def _gate_kernel(i1_ref, i2_ref, gt_ref, g_ref, *, tg):
    npair = i1_ref.shape[-1]
    sub = lax.broadcasted_iota(jnp.int32, (N_KEYS, npair), 0).astype(F32)

    def tok(t, _):
        r = pl.ds(t, 1)
        lhs = jnp.where(sub == i1_ref[r, :], gt_ref[r, :], 0.0).astype(BF16)
        rhs = jnp.where(sub == i2_ref[r, :], 1.0, 0.0).astype(BF16)
        g_ref[t] = _nt_dot(lhs, rhs).astype(g_ref.dtype)
        return 0

    lax.fori_loop(0, tg, tok, 0)


def _gate_call(i1, i2, gt):
    t, npair = i1.shape
    tg = min(TG_GATE, t)
    spec = pl.BlockSpec((tg, npair), lambda i: (i, 0))
    return pl.pallas_call(
        functools.partial(_gate_kernel, tg=tg),
        grid=(t // tg,),
        in_specs=[spec, spec, spec],
        out_specs=pl.BlockSpec((tg, N_KEYS, N_KEYS), lambda i: (i, 0, 0)),
        out_shape=jax.ShapeDtypeStruct((t, N_KEYS, N_KEYS), BF16),
        compiler_params=_cparams(("arbitrary",)),
        name="peer_gates",
    )(i1, i2, gt)


def _peer_kernel(h_ref, g_ref, u_ref, v_ref, x_ref, g2_ref, fg_ref, o_ref, acc_ref, *, final):
    j = pl.program_id(1)

    @pl.when(j == 0)
    def _():
        acc_ref[...] = jnp.zeros_like(acc_ref)

    act = _nt_dot(h_ref[...], u_ref[...])
    gelu = 0.5 * act * (1.0 + lax.erf(act * (2.0 ** -0.5)))
    coef = (g_ref[...].astype(F32) * gelu).astype(BF16)
    acc_ref[...] += jnp.dot(coef, v_ref[...], preferred_element_type=F32)

    @pl.when(j == pl.num_programs(1) - 1)
    def _():
        x2 = x_ref[...] + g2_ref[0] * acc_ref[...]
        if final:
            ms = jnp.mean(x2 * x2, axis=-1, keepdims=True)
            x2 = x2 * lax.rsqrt(ms + EPS) * fg_ref[...]
        o_ref[...] = x2


def _peer_call(h2, gmat, u, v, x1, g2, fg, final):
    bsz, s, d = x1.shape
    t = bsz * s
    ne = u.shape[0]
    tb = min(TB_PEER, s)
    eb = min(EB_PEER, ne)
    per_b = s // tb
    out = pl.pallas_call(
        functools.partial(_peer_kernel, final=final),
        grid=(t // tb, ne // eb),
        in_specs=[pl.BlockSpec((tb, d), lambda i, j: (i, 0)),
                  pl.BlockSpec((tb, eb), lambda i, j: (i, j)),
                  pl.BlockSpec((eb, d), lambda i, j: (j, 0)),
                  pl.BlockSpec((eb, d), lambda i, j: (j, 0)),
                  pl.BlockSpec((tb, d), lambda i, j: (i, 0)),
                  pl.BlockSpec((1, 1, d), lambda i, j: (i // per_b, 0, 0)),
                  pl.BlockSpec((1, d), lambda i, j: (0, 0))],
        out_specs=pl.BlockSpec((tb, d), lambda i, j: (i, 0)),
        out_shape=jax.ShapeDtypeStruct((t, d), F32),
        scratch_shapes=[pltpu.VMEM((tb, d), F32)],
        compiler_params=_cparams(("arbitrary", "arbitrary")),
        name="peer_dense",
    )(h2.reshape(t, d), gmat.reshape(t, ne), u, v, x1.reshape(t, d), g2, fg)
    return out.reshape(bsz, s, d)


def kernel(x, c, ada_w, ada_b, mix_norm_g, ffn_norm_g, w_in, conv_w, conv_b, dt_bias, a_log, d_skip,
           ssd_norm_g, pool_w, pool_scale, w_out, peer_wq, peer_keys, peer_u, peer_v, final_norm_g):
    bsz, s, d = x.shape
    depth = ada_w.shape[0]
    ssd_w = ssd_norm_g.shape[-1]
    heads = dt_bias.shape[-1]
    xbc_w = conv_w.shape[-1]
    pool_wd = pool_scale.shape[-1]
    sb_w = (w_in.shape[-1] - ssd_w - xbc_w - heads - pool_wd) // 3
    o_z, o_xbc, o_dt, o_xp = 0, ssd_w, ssd_w + xbc_w, ssd_w + xbc_w + heads
    o_q = o_xp + pool_wd
    widths = (ssd_w, xbc_w, pool_wd, sb_w, sb_w, sb_w, LANES)

    mods = _mod_call(c, ada_w, ada_b)
    for l in range(depth):
        mod = [mods[l, :, k * d:(k + 1) * d].reshape(bsz, 1, d) for k in range(6)]
        sh1, sc1, g1, sh2, sc2, g2 = mod
        wl = w_in[l]
        w_cat = jnp.concatenate(
            [wl[:, o_z:o_xbc], wl[:, o_xbc:o_dt], wl[:, o_xp:o_q], wl[:, o_q:],
             jnp.pad(wl[:, o_dt:o_xp], ((0, 0), (0, LANES - heads)))], axis=1).astype(BF16)
        z, xbc, xp, q, k, v, dt = _in_call(x, sh1, sc1, mix_norm_g[l].reshape(1, d), w_cat, widths)
        y_ssd = _ssd_call(xbc, dt, z, conv_w[l], conv_b[l], dt_bias[l], a_log[l], d_skip[l],
                          ssd_norm_g[l])
        y_pool = _pool_call(xp, pool_w[l], pool_scale[l])
        y_sb = _sb_call(q, k, v)
        keys = peer_keys[l].reshape(-1, N_KEYS, PEER_HALF).astype(BF16)
        x1, h2, sct = _out_call(y_ssd, y_pool, y_sb, x, g1, sh2, sc2, ffn_norm_g[l].reshape(1, d),
                                w_out[l].astype(BF16), peer_wq[l].astype(BF16), keys)
        i1, i2, gt = _topk_call(sct)
        gmat = _gate_call(i1, i2, gt)
        x = _peer_call(h2, gmat, peer_u[l].astype(BF16), peer_v[l].astype(BF16), x1, g2,
                       final_norm_g.reshape(1, d), final=(l == depth - 1))
    return x
```

```python
import functools

import jax
import jax.numpy as jnp
from jax import lax
from jax.experimental import pallas as pl
from jax.experimental.pallas import tpu as pltpu

F32 = jnp.float32
BF16 = jnp.bfloat16
EPS = 1e-6
HIGHEST = lax.Precision.HIGHEST

SSD_HEAD_DIM = 64
SSD_GROUPS = 2
SSD_STATE = 64
CONV_WIDTH = 4
POOL_WINDOWS = (2, 4, 8, 16)
SB_HEAD_DIM = 64
PEER_HEADS = 8
PEER_TOPK = 16
N_KEYS = 128
PEER_HALF = 128

LANES = 128
SUBLANES = 8
VMEM_LIMIT = 56 * 1024 * 1024

TM_IN = 512
SSD_CHUNK = 128
HALO = 16
TM_POOL = 512
SB_BLOCK = 128
SB_SKIP = 120.0
TT_TOPK = 256
TG_GATE = 128
GATE_GROUP = 32
GATE_PITCH = 40
TB_PEER = 512
EB_PEER = 2048


def _cparams(sem):
    return pltpu.CompilerParams(dimension_semantics=sem, vmem_limit_bytes=VMEM_LIMIT)


def _nt_dot(a, b):
    return lax.dot_general(a, b, (((1,), (1,)), ((), ())), preferred_element_type=F32)


def _softplus(x):
    return jnp.maximum(x, 0.0) + jnp.log1p(jnp.exp(-jnp.abs(x)))


def _silu(x):
    return x * jax.nn.sigmoid(x)


def _mod_kernel(c_ref, w_ref, b_ref, o_ref):
    cond = _silu(c_ref[...])
    o_ref[0] = jnp.dot(cond, w_ref[0], preferred_element_type=F32, precision=HIGHEST) + b_ref[0]


def _mod_call(c, ada_w, ada_b):
    nl, d, n6 = ada_w.shape
    bsz = c.shape[0]
    tn = 1536
    return pl.pallas_call(
        _mod_kernel,
        grid=(nl, n6 // tn),
        in_specs=[pl.BlockSpec((bsz, d), lambda l, j: (0, 0)),
                  pl.BlockSpec((1, d, tn), lambda l, j: (l, 0, j)),
                  pl.BlockSpec((1, 1, tn), lambda l, j: (l, 0, j))],
        out_specs=pl.BlockSpec((1, bsz, tn), lambda l, j: (l, 0, j)),
        out_shape=jax.ShapeDtypeStruct((nl, bsz, n6), F32),
        compiler_params=_cparams(("arbitrary", "arbitrary")),
        name="adaln_mod",
    )(c, ada_w, ada_b.reshape(nl, 1, n6))


def _norm_mod(x, g, sh, sc):
    ms = jnp.mean(x * x, axis=-1, keepdims=True)
    y = x * lax.rsqrt(ms + EPS) * g
    return y * (1.0 + sc) + sh


def _in_kernel(x_ref, sh_ref, sc_ref, g_ref, w_ref, z_ref, xbc_ref, xp_ref, q_ref, k_ref, v_ref,
               dt_ref, *, cuts):
    h = _norm_mod(x_ref[0], g_ref[...], sh_ref[0], sc_ref[0]).astype(BF16)
    outs = (z_ref, xbc_ref, xp_ref, q_ref, k_ref, v_ref, dt_ref)
    for o_ref, (a, b) in zip(outs, cuts):
        o_ref[0] = jnp.dot(h, w_ref[:, a:b], preferred_element_type=F32).astype(o_ref.dtype)


def _in_call(x, sh, sc, g, w_cat, widths):
    bsz, s, d = x.shape
    tm = min(TM_IN, s)
    cuts, a = [], 0
    for w in widths:
        cuts.append((a, a + w))
        a += w
    dtypes = (BF16, BF16, BF16, BF16, BF16, BF16, F32)
    tok = lambda w: pl.BlockSpec((1, tm, w), lambda b, i: (b, i, 0))
    vec = pl.BlockSpec((1, 1, d), lambda b, i: (b, 0, 0))
    return pl.pallas_call(
        functools.partial(_in_kernel, cuts=tuple(cuts)),
        grid=(bsz, s // tm),
        in_specs=[tok(d), vec, vec,
                  pl.BlockSpec((1, d), lambda b, i: (0, 0)),
                  pl.BlockSpec(w_cat.shape, lambda b, i: (0, 0))],
        out_specs=[tok(w) for w in widths],
        out_shape=[jax.ShapeDtypeStruct((bsz, s, w), dt) for w, dt in zip(widths, dtypes)],
        compiler_params=_cparams(("arbitrary", "arbitrary")),
        name="in_proj",
    )(x, sh, sc, g, w_cat)


def _ssd_kernel(xbc_ref, halo_ref, dt_ref, z_ref, cw_ref, cb_ref, dtb_ref, alog_ref, expand_ref,
                dsk_ref, ng_ref, y_ref, state_ref, *, chunk, width, gn):
    i = pl.program_id(1)
    L = chunk
    heads = width // SSD_HEAD_DIM
    hpg = heads // SSD_GROUPS
    half = width // SSD_GROUPS

    @pl.when(i == 0)
    def _():
        state_ref[...] = jnp.zeros_like(state_ref)

    hal = jnp.where(i > 0, halo_ref[0].astype(F32), 0.0)
    full = jnp.concatenate([hal, xbc_ref[0].astype(F32)], axis=0)
    cw = cw_ref[...]
    acc = full * cw[CONV_WIDTH - 1:CONV_WIDTH]
    for j in range(1, CONV_WIDTH):
        acc = acc + pltpu.roll(full, j, 0) * cw[CONV_WIDTH - 1 - j:CONV_WIDTH - j]
    xbc = _silu(acc[HALO:] + cb_ref[...])
    xs = xbc[:, :width]
    bm = xbc[:, width:width + gn]
    cm = xbc[:, width + gn:width + 2 * gn]

    dt = _softplus(dt_ref[0] + dtb_ref[...])
    a = dt * (-jnp.exp(alog_ref[...]))
    row_i = lax.broadcasted_iota(jnp.int32, (L, L), 0)
    col_i = lax.broadcasted_iota(jnp.int32, (L, L), 1)
    tril = row_i >= col_i
    a_cs = jnp.dot(tril.astype(F32), a, preferred_element_type=F32, precision=HIGHEST)
    a_cs_t = a_cs.T
    expand = expand_ref[...]
    dt_x = jnp.dot(dt, expand, preferred_element_type=F32, precision=HIGHEST)
    acs_x = jnp.dot(a_cs, expand, preferred_element_type=F32, precision=HIGHEST)
    alast_x = acs_x[L - 1:L, :]
    xdt = xs * dt_x

    bm_b = bm.astype(BF16)
    cm_b = cm.astype(BF16)
    lane_g = lax.broadcasted_iota(jnp.int32, (L, gn), 1) // SSD_STATE
    lane_h = lax.broadcasted_iota(jnp.int32, (L, half), 1) // SSD_HEAD_DIM

    state = state_ref[...]
    y_off = jnp.dot(cm_b, state.astype(BF16), preferred_element_type=F32) * jnp.exp(acs_x)

    y_halves = []
    for g in range(SSD_GROUPS):
        cb = _nt_dot(jnp.where(lane_g == g, cm_b, jnp.zeros_like(cm_b)), bm_b)
        xdt_g = xdt[:, g * half:(g + 1) * half]
        yh = jnp.zeros((L, half), F32)
        for hh in range(hpg):
            h = g * hpg + hh
            seg = a_cs[:, h:h + 1] - a_cs_t[h:h + 1, :]
            dec = jnp.exp(jnp.where(tril, seg, -1e30))
            m = (cb * dec).astype(BF16)
            rhs = jnp.where(lane_h == hh, xdt_g, 0.0).astype(BF16)
            yh = yh + jnp.dot(m, rhs, preferred_element_type=F32)
        y_halves.append(yh)

    ds_x = jnp.exp(alast_x - acs_x)
    upd = jnp.dot(bm.T.astype(BF16), (xdt * ds_x).astype(BF16), preferred_element_type=F32)
    srow = lax.broadcasted_iota(jnp.int32, (gn, width), 0) // SSD_STATE
    scol = lax.broadcasted_iota(jnp.int32, (gn, width), 1) // half
    state_ref[...] = jnp.exp(alast_x) * state + jnp.where(srow == scol, upd, 0.0)

    zg = _silu(z_ref[0].astype(F32))
    for g in range(SSD_GROUPS):
        sl = slice(g * half, (g + 1) * half)
        yg = (y_halves[g] + y_off[:, sl] + dsk_ref[:, sl] * xs[:, sl]) * zg[:, sl]
        ms = jnp.mean(yg * yg, axis=-1, keepdims=True)
        y_ref[0, :, sl] = (yg * lax.rsqrt(ms + EPS) * ng_ref[:, sl]).astype(y_ref.dtype)


def _ssd_call(xbc, dt, z, conv_w, conv_b, dt_bias, a_log, d_skip, ssd_norm_g):
    bsz, s, c = xbc.shape
    width = z.shape[-1]
    gn = SSD_GROUPS * SSD_STATE
    heads = width // SSD_HEAD_DIM
    L = min(SSD_CHUNK, s)
    pad = lambda v: jnp.zeros((1, LANES), F32).at[0, :heads].set(v)
    expand = (jnp.arange(LANES)[:, None] == (jnp.arange(width)[None, :] // SSD_HEAD_DIM)).astype(F32)
    dsk = jnp.repeat(d_skip, SSD_HEAD_DIM).reshape(1, width)
    hb = L // HALO
    const = lambda shp: pl.BlockSpec(shp, lambda b, i: (0, 0))
    return pl.pallas_call(
        functools.partial(_ssd_kernel, chunk=L, width=width, gn=gn),
        grid=(bsz, s // L),
        in_specs=[pl.BlockSpec((1, L, c), lambda b, i: (b, i, 0)),
                  pl.BlockSpec((1, HALO, c), lambda b, i: (b, jnp.maximum(i * hb - 1, 0), 0)),
                  pl.BlockSpec((1, L, LANES), lambda b, i: (b, i, 0)),
                  pl.BlockSpec((1, L, width), lambda b, i: (b, i, 0)),
                  const((CONV_WIDTH, c)), const((1, c)), const((1, LANES)), const((1, LANES)),
                  const((LANES, width)), const((1, width)), const((1, width))],
        out_specs=pl.BlockSpec((1, L, width), lambda b, i: (b, i, 0)),
        out_shape=jax.ShapeDtypeStruct((bsz, s, width), BF16),
        scratch_shapes=[pltpu.VMEM((gn, width), F32)],
        compiler_params=_cparams(("arbitrary", "arbitrary")),
        name="ssd_mixer",
    )(xbc, xbc, dt, z, conv_w, conv_b.reshape(1, c), pad(dt_bias), pad(a_log), expand, dsk,
      ssd_norm_g.reshape(1, width))


def _pool_kernel(xp_ref, halo_ref, w_ref, sc_ref, y_ref, *, tm, gdim):
    i = pl.program_id(1)
    hal = jnp.where(i > 0, halo_ref[0].astype(F32), 0.0)
    x = xp_ref[0].astype(F32)
    full = jnp.concatenate([hal, x], axis=0)
    lane_g = lax.broadcasted_iota(jnp.int32, x.shape, 1) // gdim
    tpos = i * tm + lax.broadcasted_iota(jnp.int32, x.shape, 0)
    win_sum = jnp.zeros_like(x)
    cnt = jnp.ones_like(x)
    s = full
    span = 1
    for gi, w in enumerate(POOL_WINDOWS):
        while span < w:
            s = s + pltpu.roll(s, span, 0)
            span *= 2
        win_sum = jnp.where(lane_g == gi, s[HALO:], win_sum)
        cnt = jnp.where(lane_g == gi, jnp.minimum(tpos + 1, w).astype(F32), cnt)
    pooled = (win_sum / cnt - x).astype(BF16)
    y = jnp.dot(pooled, w_ref[...], preferred_element_type=F32) * sc_ref[...]
    y_ref[0] = y.astype(y_ref.dtype)


def _pool_call(xp, pool_w, pool_scale):
    bsz, s, w = xp.shape
    ng, gdim, _ = pool_w.shape
    assert POOL_WINDOWS[-1] <= HALO and all(b == 2 * a for a, b in zip(POOL_WINDOWS, POOL_WINDOWS[1:]))
    tm = min(TM_POOL, s)
    wbd = jnp.zeros((w, w), F32)
    for g in range(ng):
        wbd = wbd.at[g * gdim:(g + 1) * gdim, g * gdim:(g + 1) * gdim].set(pool_w[g])
    hb = tm // HALO
    return pl.pallas_call(
        functools.partial(_pool_kernel, tm=tm, gdim=gdim),
        grid=(bsz, s // tm),
        in_specs=[pl.BlockSpec((1, tm, w), lambda b, i: (b, i, 0)),
                  pl.BlockSpec((1, HALO, w), lambda b, i: (b, jnp.maximum(i * hb - 1, 0), 0)),
                  pl.BlockSpec((w, w), lambda b, i: (0, 0)),
                  pl.BlockSpec((1, w), lambda b, i: (0, 0))],
        out_specs=pl.BlockSpec((1, tm, w), lambda b, i: (b, i, 0)),
        out_shape=jax.ShapeDtypeStruct((bsz, s, w), BF16),
        compiler_params=_cparams(("arbitrary", "arbitrary")),
        name="pool_mixer",
    )(xp, xp, wbd.astype(BF16), pool_scale.reshape(1, w))


def _sb_kernel(q_ref, k_ref, v_ref, y_ref, qm_ref, acc_ref, carry_ref, *, blk, heads):
    i = pl.program_id(1)
    T = blk
    w = q_ref.shape[-1]
    lane_h = lax.broadcasted_iota(jnp.int32, (T, w), 1) // SB_HEAD_DIM
    q = q_ref[0] * (SB_HEAD_DIM ** -0.5)
    for h in range(heads):
        qm_ref[h] = jnp.where(lane_h == h, q, jnp.zeros_like(q))
    acc_ref[...] = jnp.zeros_like(acc_ref)
    carry_ref[...] = jnp.zeros_like(carry_ref)
    row_i = lax.broadcasted_iota(jnp.int32, (T, T), 0)
    col_i = lax.broadcasted_iota(jnp.int32, (T, T), 1)
    causal = col_i < row_i
    suffix = (row_i > col_i).astype(BF16)

    def process(j, diag):
        start = pl.multiple_of(j * T, T)
        ks = k_ref[0, pl.ds(start, T), :]
        vs = v_ref[0, pl.ds(start, T), :]
        for h in range(heads):
            z = _nt_dot(qm_ref[h], ks)
            lk = -_softplus(z)
            if diag:
                lk = jnp.where(causal, lk, 0.0)
            hi = lk.astype(BF16)
            lo = (lk - hi.astype(F32)).astype(BF16)
            later = (jnp.dot(hi, suffix, preferred_element_type=F32)
                     + jnp.dot(lo, suffix, preferred_element_type=F32))
            c = carry_ref[h]
            wgt = jnp.exp(z + lk + later + c)
            if diag:
                wgt = jnp.where(causal, wgt, 0.0)
            pv = jnp.dot(wgt.astype(BF16), vs, preferred_element_type=F32)
            acc_ref[...] += jnp.where(lane_h == h, pv, 0.0)
            carry_ref[h] = c + jnp.sum(lk, axis=1, keepdims=True)

    def alive():
        return (jnp.max(carry_ref[...]) > -SB_SKIP).astype(jnp.int32)

    process(i, True)

    def cond(st):
        return jnp.logical_and(st[0] >= 0, st[1] > 0)

    def body(st):
        process(st[0], False)
        return st[0] - 1, alive()

    lax.while_loop(cond, body, (i - 1, alive()))
    y_ref[0] = acc_ref[...].astype(y_ref.dtype)


def _sb_call(q, k, v):
    bsz, s, w = q.shape
    heads = w // SB_HEAD_DIM
    T = min(SB_BLOCK, s)
    return pl.pallas_call(
        functools.partial(_sb_kernel, blk=T, heads=heads),
        grid=(bsz, s // T),
        in_specs=[pl.BlockSpec((1, T, w), lambda b, i: (b, i, 0)),
                  pl.BlockSpec((1, s, w), lambda b, i: (b, 0, 0)),
                  pl.BlockSpec((1, s, w), lambda b, i: (b, 0, 0))],
        out_specs=pl.BlockSpec((1, T, w), lambda b, i: (b, i, 0)),
        out_shape=jax.ShapeDtypeStruct((bsz, s, w), BF16),
        scratch_shapes=[pltpu.VMEM((heads, T, w), BF16),
                        pltpu.VMEM((T, w), F32),
                        pltpu.VMEM((heads, T, 1), F32)],
        compiler_params=_cparams(("arbitrary", "arbitrary")),
        name="stick_breaking",
    )(q, k, v)


def _out_kernel(ys_ref, yp_ref, yb_ref, x_ref, g1_ref, sh_ref, sc_ref, ng_ref, wo_ref, wq_ref,
                keys_ref, x1_ref, h2_ref, sct_ref, *, cuts):
    mix = jnp.zeros(x_ref.shape[1:], F32)
    for y_ref, (a, b) in zip((ys_ref, yp_ref, yb_ref), cuts):
        mix = mix + jnp.dot(y_ref[0], wo_ref[a:b, :], preferred_element_type=F32)
    x1 = x_ref[0] + g1_ref[0] * mix
    x1_ref[0] = x1
    h2 = _norm_mod(x1, ng_ref[...], sh_ref[0], sc_ref[0]).astype(BF16)
    h2_ref[0] = h2
    qb = jnp.dot(h2, wq_ref[...], preferred_element_type=F32).astype(BF16)
    for lst in range(keys_ref.shape[0]):
        sct_ref[lst] = _nt_dot(keys_ref[lst], qb[:, lst * PEER_HALF:(lst + 1) * PEER_HALF])


def _out_call(ys, yp, yb, x, g1, sh, sc, ng, w_out, wq, keys):
    bsz, s, d = x.shape
    tm = min(TM_IN, s)
    nlist = keys.shape[0]
    cuts, a = [], 0
    for y in (ys, yp, yb):
        cuts.append((a, a + y.shape[-1]))
        a += y.shape[-1]
    tok = lambda w: pl.BlockSpec((1, tm, w), lambda b, i: (b, i, 0))
    vec = pl.BlockSpec((1, 1, d), lambda b, i: (b, 0, 0))
    full2 = lambda arr: pl.BlockSpec(arr.shape, lambda b, i: (0,) * arr.ndim)
    nblk = s // tm
    return pl.pallas_call(
        functools.partial(_out_kernel, cuts=tuple(cuts)),
        grid=(bsz, nblk),
        in_specs=[tok(ys.shape[-1]), tok(yp.shape[-1]), tok(yb.shape[-1]), tok(d), vec, vec, vec,
                  pl.BlockSpec((1, d), lambda b, i: (0, 0)), full2(w_out), full2(wq), full2(keys)],
        out_specs=[tok(d), tok(d),
                   pl.BlockSpec((nlist, N_KEYS, tm), lambda b, i: (0, 0, b * nblk + i))],
        out_shape=[jax.ShapeDtypeStruct((bsz, s, d), F32),
                   jax.ShapeDtypeStruct((bsz, s, d), BF16),
                   jax.ShapeDtypeStruct((nlist, N_KEYS, bsz * s), F32)],
        compiler_params=_cparams(("arbitrary", "arbitrary")),
        name="out_proj_peer_query",
    )(ys, yp, yb, x, g1, sh, sc, ng, w_out, wq, keys)


def _top_rows(s, k, payload=None):
    nrows = s.shape[0]
    rows = lax.broadcasted_iota(jnp.int32, s.shape, 0).astype(F32)
    vals, picks = [], []
    for _ in range(k):
        m = jnp.max(s, axis=0, keepdims=True)
        idx = jnp.min(jnp.where(s == m, rows, float(nrows)), axis=0, keepdims=True)
        hit = rows == idx
        vals.append(m)
        if payload is None:
            picks.append(idx)
        else:
            picks.append(jnp.sum(jnp.where(hit, payload, 0.0), axis=0, keepdims=True))
        s = jnp.where(hit, -jnp.inf, s)
    return jnp.concatenate(vals, axis=0), jnp.concatenate(picks, axis=0)


def _pair_candidates(v1, x1, v2, x2):
    K = PEER_TOPK
    cands, ids = [], []
    a = 0
    while K // (a + 1) > 1:
        n = K // (a + 1)
        npad = -(-n // SUBLANES) * SUBLANES
        c = v1[a:a + 1, :] + v2[0:npad, :]
        if n < npad:
            c = jnp.where(lax.broadcasted_iota(jnp.int32, c.shape, 0) < n, c, -jnp.inf)
        cands.append(c)
        ids.append(x1[a:a + 1, :] * float(N_KEYS) + x2[0:npad, :])
        a += 1
    cands.append(v1[a:K, :] + v2[0:1, :])
    ids.append(x1[a:K, :] * float(N_KEYS) + x2[0:1, :])
    return jnp.concatenate(cands, axis=0), jnp.concatenate(ids, axis=0)


def _topk_kernel(sct_ref, i1_ref, i2_ref, gt_ref, p1_ref, p2_ref, pg_ref):
    K = PEER_TOPK

    def head(h, _):
        v1, x1 = _top_rows(sct_ref[2 * h], K)
        v2, x2 = _top_rows(sct_ref[2 * h + 1], K)
        cand, ids = _pair_candidates(v1, x1, v2, x2)
        best, expert = _top_rows(cand, K, payload=ids)
        e = jnp.exp(best - best[0:1, :])
        gate = e / jnp.sum(e, axis=0, keepdims=True)
        i1 = jnp.floor(expert * (1.0 / N_KEYS))
        r0 = pl.multiple_of(h * K, K)
        p1_ref[pl.ds(r0, K), :] = i1
        p2_ref[pl.ds(r0, K), :] = expert - i1 * float(N_KEYS)
        pg_ref[pl.ds(r0, K), :] = gate
        return 0

    lax.fori_loop(0, PEER_HEADS, head, 0)
    i1_ref[...] = p1_ref[...].T
    i2_ref[...] = p2_ref[...].T
    gt_ref[...] = pg_ref[...].T


def _topk_call(sct):
    nlist, nk, t = sct.shape
    tt = min(TT_TOPK, t)
    npair = PEER_HEADS * PEER_TOPK
    out = pl.BlockSpec((tt, npair), lambda i: (i, 0))
    return pl.pallas_call(
        _topk_kernel,
        grid=(t // tt,),
        in_specs=[pl.BlockSpec((nlist, nk, tt), lambda i: (0, 0, i))],
        out_specs=[out, out, out],
        out_shape=[jax.ShapeDtypeStruct((t, npair), F32)] * 3,
        scratch_shapes=[pltpu.VMEM((npair, tt), F32)] * 3,
        compiler_params=_cparams(("arbitrary",)),
        name="peer_topk",
    )(sct)


def _gate_kernel(i1_ref, i2_ref, gt_ref, g_ref, scr_ref, *, tgs):
    npair = i1_ref.shape[-1]
    sub = lax.broadcasted_iota(jnp.int32, (N_KEYS, npair), 0).astype(F32)

    def group(gi, _):
        g0 = pl.multiple_of(gi * GATE_GROUP, GATE_GROUP)

        for tl in range(GATE_GROUP):
            r = pl.ds(g0 + tl, 1)
            lhs = jnp.where(sub == i1_ref[r, :], gt_ref[r, :], 0.0).astype(BF16)
            rhs = jnp.where(sub == i2_ref[r, :], 1.0, 0.0).astype(BF16)
            scr_ref[pl.ds(tl, N_KEYS, stride=GATE_PITCH), :] = _nt_dot(lhs, rhs)
        for a in range(N_KEYS):
            g_ref[a, pl.ds(g0, GATE_GROUP), :] = (
                scr_ref[a * GATE_PITCH:a * GATE_PITCH + GATE_GROUP, :].astype(g_ref.dtype))
        return 0

    lax.fori_loop(0, tgs // GATE_GROUP, group, 0)


def _gate_call(i1, i2, gt):
    t, npair = i1.shape
    tgs = min(TG_GATE, t)
    spec = pl.BlockSpec((tgs, npair), lambda i: (i, 0))
    return pl.pallas_call(
        functools.partial(_gate_kernel, tgs=tgs),
        grid=(t // tgs,),
        in_specs=[spec, spec, spec],
        out_specs=pl.BlockSpec((N_KEYS, tgs, N_KEYS), lambda i: (0, i, 0)),
        out_shape=jax.ShapeDtypeStruct((N_KEYS, t, N_KEYS), BF16),
        scratch_shapes=[pltpu.VMEM((N_KEYS * GATE_PITCH, N_KEYS), F32)],
        compiler_params=_cparams(("arbitrary",)),
        name="peer_gates",
    )(i1, i2, gt)


def _peer_kernel(h_ref, g_ref, u_ref, v_ref, x_ref, g2_ref, fg_ref, o_ref, acc_ref, *, final):
    j = pl.program_id(1)

    @pl.when(j == 0)
    def _():
        acc_ref[...] = jnp.zeros_like(acc_ref)

    act = _nt_dot(h_ref[...], u_ref[...])
    gelu = 0.5 * act * (1.0 + lax.erf(act * (2.0 ** -0.5)))
    coef = jnp.concatenate(
        [(g_ref[a].astype(F32) * gelu[:, a * N_KEYS:(a + 1) * N_KEYS]).astype(BF16)
         for a in range(g_ref.shape[0])], axis=1)
    acc_ref[...] += jnp.dot(coef, v_ref[...], preferred_element_type=F32)

    @pl.when(j == pl.num_programs(1) - 1)
    def _():
        x2 = x_ref[...] + g2_ref[0] * acc_ref[...]
        if final:
            ms = jnp.mean(x2 * x2, axis=-1, keepdims=True)
            x2 = x2 * lax.rsqrt(ms + EPS) * fg_ref[...]
        o_ref[...] = x2


def _peer_call(h2, gmat, u, v, x1, g2, fg, final):
    bsz, s, d = x1.shape
    t = bsz * s
    ne = u.shape[0]
    tb = min(TB_PEER, s)
    eb = min(EB_PEER, ne)
    per_b = s // tb
    out = pl.pallas_call(
        functools.partial(_peer_kernel, final=final),
        grid=(t // tb, ne // eb),
        in_specs=[pl.BlockSpec((tb, d), lambda i, j: (i, 0)),
                  pl.BlockSpec((eb // N_KEYS, tb, N_KEYS), lambda i, j: (j, i, 0)),
                  pl.BlockSpec((eb, d), lambda i, j: (j, 0)),
                  pl.BlockSpec((eb, d), lambda i, j: (j, 0)),
                  pl.BlockSpec((tb, d), lambda i, j: (i, 0)),
                  pl.BlockSpec((1, 1, d), lambda i, j: (i // per_b, 0, 0)),
                  pl.BlockSpec((1, d), lambda i, j: (0, 0))],
        out_specs=pl.BlockSpec((tb, d), lambda i, j: (i, 0)),
        out_shape=jax.ShapeDtypeStruct((t, d), F32),
        scratch_shapes=[pltpu.VMEM((tb, d), F32)],
        compiler_params=_cparams(("arbitrary", "arbitrary")),
        name="peer_dense",
    )(h2.reshape(t, d), gmat, u, v, x1.reshape(t, d), g2, fg)
    return out.reshape(bsz, s, d)


def kernel(x, c, ada_w, ada_b, mix_norm_g, ffn_norm_g, w_in, conv_w, conv_b, dt_bias, a_log, d_skip,
           ssd_norm_g, pool_w, pool_scale, w_out, peer_wq, peer_keys, peer_u, peer_v, final_norm_g):
    bsz, s, d = x.shape
    depth = ada_w.shape[0]
    ssd_w = ssd_norm_g.shape[-1]
    heads = dt_bias.shape[-1]
    xbc_w = conv_w.shape[-1]
    pool_wd = pool_scale.shape[-1]
    sb_w = (w_in.shape[-1] - ssd_w - xbc_w - heads - pool_wd) // 3
    o_z, o_xbc, o_dt, o_xp = 0, ssd_w, ssd_w + xbc_w, ssd_w + xbc_w + heads
    o_q = o_xp + pool_wd
    widths = (ssd_w, xbc_w, pool_wd, sb_w, sb_w, sb_w, LANES)

    mods = _mod_call(c, ada_w, ada_b)
    for l in range(depth):
        mod = [mods[l, :, k * d:(k + 1) * d].reshape(bsz, 1, d) for k in range(6)]
        sh1, sc1, g1, sh2, sc2, g2 = mod
        wl = w_in[l]
        w_cat = jnp.concatenate(
            [wl[:, o_z:o_xbc], wl[:, o_xbc:o_dt], wl[:, o_xp:o_q], wl[:, o_q:],
             jnp.pad(wl[:, o_dt:o_xp], ((0, 0), (0, LANES - heads)))], axis=1).astype(BF16)
        z, xbc, xp, q, k, v, dt = _in_call(x, sh1, sc1, mix_norm_g[l].reshape(1, d), w_cat, widths)
        y_ssd = _ssd_call(xbc, dt, z, conv_w[l], conv_b[l], dt_bias[l], a_log[l], d_skip[l],
                          ssd_norm_g[l])
        y_pool = _pool_call(xp, pool_w[l], pool_scale[l])
        y_sb = _sb_call(q, k, v)
        keys = peer_keys[l].reshape(-1, N_KEYS, PEER_HALF).astype(BF16)
        x1, h2, sct = _out_call(y_ssd, y_pool, y_sb, x, g1, sh2, sc2, ffn_norm_g[l].reshape(1, d),
                                w_out[l].astype(BF16), peer_wq[l].astype(BF16), keys)
        i1, i2, gt = _topk_call(sct)
        gmat = _gate_call(i1, i2, gt)
        x = _peer_call(h2, gmat, peer_u[l].astype(BF16), peer_v[l].astype(BF16), x1, g2,
                       final_norm_g.reshape(1, d), final=(l == depth - 1))
    return x
```

```python
import functools

import jax
import jax.numpy as jnp
from jax import lax
from jax.experimental import pallas as pl
from jax.experimental.pallas import tpu as pltpu

F32 = jnp.float32
BF16 = jnp.bfloat16
EPS = 1e-6
HIGHEST = lax.Precision.HIGHEST

SSD_HEAD_DIM = 64
SSD_GROUPS = 2
SSD_STATE = 64
CONV_WIDTH = 4
POOL_WINDOWS = (2, 4, 8, 16)
SB_HEAD_DIM = 64
PEER_HEADS = 8
PEER_TOPK = 16
N_KEYS = 128
PEER_HALF = 128

LANES = 128
SUBLANES = 8
MXU_WIDTH = 256
VMEM_LIMIT = 56 * 1024 * 1024

TM_IN = 512
SSD_CHUNK = 128
HALO = 16
TM_POOL = 512
SB_BLOCK = 128
SB_SKIP = 120.0
GATE_GROUP = 32
GATE_PITCH = 40
TB_PEER = 256


def _cparams(sem):
    return pltpu.CompilerParams(dimension_semantics=sem, vmem_limit_bytes=VMEM_LIMIT)


def _nt_dot(a, b):
    return lax.dot_general(a, b, (((1,), (1,)), ((), ())), preferred_element_type=F32)


def _softplus(x):
    return jnp.maximum(x, 0.0) + jnp.log1p(jnp.exp(-jnp.abs(x)))


def _silu(x):
    return x * jax.nn.sigmoid(x)


def _mod_kernel(c_ref, w_ref, b_ref, o_ref):
    cond = _silu(c_ref[...])
    o_ref[0] = jnp.dot(cond, w_ref[0], preferred_element_type=F32, precision=HIGHEST) + b_ref[0]


def _mod_call(c, ada_w, ada_b):
    nl, d, n6 = ada_w.shape
    bsz = c.shape[0]
    tn = 1536
    return pl.pallas_call(
        _mod_kernel,
        grid=(nl, n6 // tn),
        in_specs=[pl.BlockSpec((bsz, d), lambda l, j: (0, 0)),
                  pl.BlockSpec((1, d, tn), lambda l, j: (l, 0, j)),
                  pl.BlockSpec((1, 1, tn), lambda l, j: (l, 0, j))],
        out_specs=pl.BlockSpec((1, bsz, tn), lambda l, j: (l, 0, j)),
        out_shape=jax.ShapeDtypeStruct((nl, bsz, n6), F32),
        compiler_params=_cparams(("arbitrary", "arbitrary")),
        name="adaln_mod",
    )(c, ada_w, ada_b.reshape(nl, 1, n6))


def _norm_mod(x, g, sh, sc):
    ms = jnp.mean(x * x, axis=-1, keepdims=True)
    y = x * lax.rsqrt(ms + EPS) * g
    return y * (1.0 + sc) + sh


def _in_kernel(x_ref, sh_ref, sc_ref, g_ref, w_ref, z_ref, xbc_ref, xp_ref, q_ref, k_ref, v_ref,
               dt_ref, *, cuts):
    h = _norm_mod(x_ref[0], g_ref[...], sh_ref[0], sc_ref[0]).astype(BF16)
    outs = (z_ref, xbc_ref, xp_ref, q_ref, k_ref, v_ref, dt_ref)
    for o_ref, (a, b) in zip(outs, cuts):
        o_ref[0] = jnp.dot(h, w_ref[:, a:b], preferred_element_type=F32).astype(o_ref.dtype)


def _in_call(x, sh, sc, g, w_cat, widths):
    bsz, s, d = x.shape
    tm = min(TM_IN, s)
    cuts, a = [], 0
    for w in widths:
        cuts.append((a, a + w))
        a += w
    dtypes = (BF16, BF16, BF16, BF16, BF16, BF16, F32)
    tok = lambda w: pl.BlockSpec((1, tm, w), lambda b, i: (b, i, 0))
    vec = pl.BlockSpec((1, 1, d), lambda b, i: (b, 0, 0))
    return pl.pallas_call(
        functools.partial(_in_kernel, cuts=tuple(cuts)),
        grid=(bsz, s // tm),
        in_specs=[tok(d), vec, vec,
                  pl.BlockSpec((1, d), lambda b, i: (0, 0)),
                  pl.BlockSpec(w_cat.shape, lambda b, i: (0, 0))],
        out_specs=[tok(w) for w in widths],
        out_shape=[jax.ShapeDtypeStruct((bsz, s, w), dt) for w, dt in zip(widths, dtypes)],
        compiler_params=_cparams(("arbitrary", "arbitrary")),
        name="in_proj",
    )(x, sh, sc, g, w_cat)


def _ssd_kernel(xbc_ref, halo_ref, dt_ref, z_ref, cw_ref, cb_ref, dtb_ref, alog_ref, expand_ref,
                dsk_ref, ng_ref, y_ref, state_ref, *, chunk, width, gn):
    i = pl.program_id(1)
    L = chunk
    heads = width // SSD_HEAD_DIM
    hpg = heads // SSD_GROUPS
    half = width // SSD_GROUPS

    @pl.when(i == 0)
    def _():
        state_ref[...] = jnp.zeros_like(state_ref)

    hal = jnp.where(i > 0, halo_ref[0].astype(F32), 0.0)
    full = jnp.concatenate([hal, xbc_ref[0].astype(F32)], axis=0)
    cw = cw_ref[...]
    acc = full * cw[CONV_WIDTH - 1:CONV_WIDTH]
    for j in range(1, CONV_WIDTH):
        acc = acc + pltpu.roll(full, j, 0) * cw[CONV_WIDTH - 1 - j:CONV_WIDTH - j]
    xbc = _silu(acc[HALO:] + cb_ref[...])
    xs = xbc[:, :width]
    bm = xbc[:, width:width + gn]
    cm = xbc[:, width + gn:width + 2 * gn]

    dt = _softplus(dt_ref[0] + dtb_ref[...])
    a = dt * (-jnp.exp(alog_ref[...]))
    row_i = lax.broadcasted_iota(jnp.int32, (L, L), 0)
    col_i = lax.broadcasted_iota(jnp.int32, (L, L), 1)
    tril = row_i >= col_i
    a_cs = jnp.dot(tril.astype(F32), a, preferred_element_type=F32, precision=HIGHEST)
    a_cs_t = a_cs.T
    expand = expand_ref[...]
    dt_x = jnp.dot(dt, expand, preferred_element_type=F32, precision=HIGHEST)
    acs_x = jnp.dot(a_cs, expand, preferred_element_type=F32, precision=HIGHEST)
    alast_x = acs_x[L - 1:L, :]
    xdt = xs * dt_x

    bm_b = bm.astype(BF16)
    cm_b = cm.astype(BF16)
    lane_g = lax.broadcasted_iota(jnp.int32, (L, gn), 1) // SSD_STATE
    lane_h = lax.broadcasted_iota(jnp.int32, (L, half), 1) // SSD_HEAD_DIM

    state = state_ref[...]
    y_off = jnp.dot(cm_b, state.astype(BF16), preferred_element_type=F32) * jnp.exp(acs_x)

    y_halves = []
    for g in range(SSD_GROUPS):
        cb = _nt_dot(jnp.where(lane_g == g, cm_b, jnp.zeros_like(cm_b)), bm_b)
        xdt_g = xdt[:, g * half:(g + 1) * half]
        yh = jnp.zeros((L, half), F32)
        for hh in range(hpg):
            h = g * hpg + hh
            seg = a_cs[:, h:h + 1] - a_cs_t[h:h + 1, :]
            dec = jnp.exp(jnp.where(tril, seg, -1e30))
            m = (cb * dec).astype(BF16)
            rhs = jnp.where(lane_h == hh, xdt_g, 0.0).astype(BF16)
            yh = yh + jnp.dot(m, rhs, preferred_element_type=F32)
        y_halves.append(yh)

    ds_x = jnp.exp(alast_x - acs_x)
    upd = jnp.dot(bm.T.astype(BF16), (xdt * ds_x).astype(BF16), preferred_element_type=F32)
    srow = lax.broadcasted_iota(jnp.int32, (gn, width), 0) // SSD_STATE
    scol = lax.broadcasted_iota(jnp.int32, (gn, width), 1) // half
    state_ref[...] = jnp.exp(alast_x) * state + jnp.where(srow == scol, upd, 0.0)

    zg = _silu(z_ref[0].astype(F32))
    for g in range(SSD_GROUPS):
        sl = slice(g * half, (g + 1) * half)
        yg = (y_halves[g] + y_off[:, sl] + dsk_ref[:, sl] * xs[:, sl]) * zg[:, sl]
        ms = jnp.mean(yg * yg, axis=-1, keepdims=True)
        y_ref[0, :, sl] = (yg * lax.rsqrt(ms + EPS) * ng_ref[:, sl]).astype(y_ref.dtype)


def _ssd_call(xbc, dt, z, conv_w, conv_b, dt_bias, a_log, d_skip, ssd_norm_g):
    bsz, s, c = xbc.shape
    width = z.shape[-1]
    gn = SSD_GROUPS * SSD_STATE
    heads = width // SSD_HEAD_DIM
    L = min(SSD_CHUNK, s)
    pad = lambda v: jnp.zeros((1, LANES), F32).at[0, :heads].set(v)
    expand = (jnp.arange(LANES)[:, None] == (jnp.arange(width)[None, :] // SSD_HEAD_DIM)).astype(F32)
    dsk = jnp.repeat(d_skip, SSD_HEAD_DIM).reshape(1, width)
    hb = L // HALO
    const = lambda shp: pl.BlockSpec(shp, lambda b, i: (0, 0))
    return pl.pallas_call(
        functools.partial(_ssd_kernel, chunk=L, width=width, gn=gn),
        grid=(bsz, s // L),
        in_specs=[pl.BlockSpec((1, L, c), lambda b, i: (b, i, 0)),
                  pl.BlockSpec((1, HALO, c), lambda b, i: (b, jnp.maximum(i * hb - 1, 0), 0)),
                  pl.BlockSpec((1, L, LANES), lambda b, i: (b, i, 0)),
                  pl.BlockSpec((1, L, width), lambda b, i: (b, i, 0)),
                  const((CONV_WIDTH, c)), const((1, c)), const((1, LANES)), const((1, LANES)),
                  const((LANES, width)), const((1, width)), const((1, width))],
        out_specs=pl.BlockSpec((1, L, width), lambda b, i: (b, i, 0)),
        out_shape=jax.ShapeDtypeStruct((bsz, s, width), BF16),
        scratch_shapes=[pltpu.VMEM((gn, width), F32)],
        compiler_params=_cparams(("arbitrary", "arbitrary")),
        name="ssd_mixer",
    )(xbc, xbc, dt, z, conv_w, conv_b.reshape(1, c), pad(dt_bias), pad(a_log), expand, dsk,
      ssd_norm_g.reshape(1, width))


def _pool_kernel(xp_ref, halo_ref, w_ref, sc_ref, y_ref, *, tm, gdim):
    i = pl.program_id(1)
    hal = jnp.where(i > 0, halo_ref[0].astype(F32), 0.0)
    x = xp_ref[0].astype(F32)
    full = jnp.concatenate([hal, x], axis=0)
    lane_g = lax.broadcasted_iota(jnp.int32, x.shape, 1) // gdim
    tpos = i * tm + lax.broadcasted_iota(jnp.int32, x.shape, 0)
    win_sum = jnp.zeros_like(x)
    cnt = jnp.ones_like(x)
    s = full
    span = 1
    for gi, w in enumerate(POOL_WINDOWS):
        while span < w:
            s = s + pltpu.roll(s, span, 0)
            span *= 2
        win_sum = jnp.where(lane_g == gi, s[HALO:], win_sum)
        cnt = jnp.where(lane_g == gi, jnp.minimum(tpos + 1, w).astype(F32), cnt)
    pooled = (win_sum / cnt - x).astype(BF16)
    y = jnp.dot(pooled, w_ref[...], preferred_element_type=F32) * sc_ref[...]
    y_ref[0] = y.astype(y_ref.dtype)


def _pool_call(xp, pool_w, pool_scale):
    bsz, s, w = xp.shape
    ng, gdim, _ = pool_w.shape
    assert POOL_WINDOWS[-1] <= HALO and all(b == 2 * a for a, b in zip(POOL_WINDOWS, POOL_WINDOWS[1:]))
    tm = min(TM_POOL, s)
    wbd = jnp.zeros((w, w), F32)
    for g in range(ng):
        wbd = wbd.at[g * gdim:(g + 1) * gdim, g * gdim:(g + 1) * gdim].set(pool_w[g])
    hb = tm // HALO
    return pl.pallas_call(
        functools.partial(_pool_kernel, tm=tm, gdim=gdim),
        grid=(bsz, s // tm),
        in_specs=[pl.BlockSpec((1, tm, w), lambda b, i: (b, i, 0)),
                  pl.BlockSpec((1, HALO, w), lambda b, i: (b, jnp.maximum(i * hb - 1, 0), 0)),
                  pl.BlockSpec((w, w), lambda b, i: (0, 0)),
                  pl.BlockSpec((1, w), lambda b, i: (0, 0))],
        out_specs=pl.BlockSpec((1, tm, w), lambda b, i: (b, i, 0)),
        out_shape=jax.ShapeDtypeStruct((bsz, s, w), BF16),
        compiler_params=_cparams(("arbitrary", "arbitrary")),
        name="pool_mixer",
    )(xp, xp, wbd.astype(BF16), pool_scale.reshape(1, w))


def _sb_kernel(q_ref, k_ref, v_ref, y_ref, qm_ref, acc_ref, carry_ref, *, blk, heads):
    i = pl.program_id(1)
    T = blk
    w = q_ref.shape[-1]
    lane_h = lax.broadcasted_iota(jnp.int32, (T, w), 1) // SB_HEAD_DIM
    q = q_ref[0] * (SB_HEAD_DIM ** -0.5)
    for h in range(heads):
        qm_ref[h * T:(h + 1) * T, :] = jnp.where(lane_h == h, q, jnp.zeros_like(q))
    acc_ref[...] = jnp.zeros_like(acc_ref)
    carry_ref[...] = jnp.zeros_like(carry_ref)
    row_i = lax.broadcasted_iota(jnp.int32, (T, T), 0)
    col_i = lax.broadcasted_iota(jnp.int32, (T, T), 1)
    suffix = (row_i > col_i).astype(BF16)
    qrow = lax.broadcasted_iota(jnp.int32, (heads * T, T), 0) % T
    causal = lax.broadcasted_iota(jnp.int32, (heads * T, T), 1) < qrow

    def process(j, diag):
        start = pl.multiple_of(j * T, T)
        ks = k_ref[0, pl.ds(start, T), :]
        vs = v_ref[0, pl.ds(start, T), :]
        z = _nt_dot(qm_ref[...], ks)
        lk = -_softplus(z)
        if diag:
            lk = jnp.where(causal, lk, 0.0)
        hi = lk.astype(BF16)
        lo = (lk - hi.astype(F32)).astype(BF16)
        later = (jnp.dot(hi, suffix, preferred_element_type=F32)
                 + jnp.dot(lo, suffix, preferred_element_type=F32))
        c = carry_ref[...]
        wgt = jnp.exp(z + lk + later + c)
        if diag:
            wgt = jnp.where(causal, wgt, 0.0)
        pv = jnp.dot(wgt.astype(BF16), vs, preferred_element_type=F32)
        out = pv[0:T]
        for h in range(1, heads):
            out = jnp.where(lane_h == h, pv[h * T:(h + 1) * T], out)
        acc_ref[...] += out
        carry_ref[...] = c + jnp.sum(lk, axis=1, keepdims=True)

    def alive():
        return (jnp.max(carry_ref[...]) > -SB_SKIP).astype(jnp.int32)

    process(i, True)

    def cond(st):
        return jnp.logical_and(st[0] >= 0, st[1] > 0)

    def body(st):
        process(st[0], False)
        return st[0] - 1, alive()

    lax.while_loop(cond, body, (i - 1, alive()))
    y_ref[0] = acc_ref[...].astype(y_ref.dtype)


def _sb_call(q, k, v):
    bsz, s, w = q.shape
    heads = w // SB_HEAD_DIM
    T = min(SB_BLOCK, s)
    return pl.pallas_call(
        functools.partial(_sb_kernel, blk=T, heads=heads),
        grid=(bsz, s // T),
        in_specs=[pl.BlockSpec((1, T, w), lambda b, i: (b, i, 0)),
                  pl.BlockSpec((1, s, w), lambda b, i: (b, 0, 0)),
                  pl.BlockSpec((1, s, w), lambda b, i: (b, 0, 0))],
        out_specs=pl.BlockSpec((1, T, w), lambda b, i: (b, i, 0)),
        out_shape=jax.ShapeDtypeStruct((bsz, s, w), BF16),
        scratch_shapes=[pltpu.VMEM((heads * T, w), BF16),
                        pltpu.VMEM((T, w), F32),
                        pltpu.VMEM((heads * T, 1), F32)],
        compiler_params=_cparams(("arbitrary", "arbitrary")),
        name="stick_breaking",
    )(q, k, v)


def _out_kernel(ys_ref, yp_ref, yb_ref, x_ref, g1_ref, sh_ref, sc_ref, ng_ref, wo_ref, wq_ref,
                keys_ref, x1_ref, h2_ref, sct_ref, *, cuts):
    mix = jnp.zeros(x_ref.shape[1:], F32)
    for y_ref, (a, b) in zip((ys_ref, yp_ref, yb_ref), cuts):
        mix = mix + jnp.dot(y_ref[0], wo_ref[a:b, :], preferred_element_type=F32)
    x1 = x_ref[0] + g1_ref[0] * mix
    x1_ref[0] = x1
    h2 = _norm_mod(x1, ng_ref[...], sh_ref[0], sc_ref[0]).astype(BF16)
    h2_ref[0] = h2
    qb = jnp.dot(h2, wq_ref[...], preferred_element_type=F32).astype(BF16)
    for lst in range(keys_ref.shape[0]):
        sct_ref[lst] = _nt_dot(keys_ref[lst], qb[:, lst * PEER_HALF:(lst + 1) * PEER_HALF])


def _out_call(ys, yp, yb, x, g1, sh, sc, ng, w_out, wq, keys):
    bsz, s, d = x.shape
    tm = min(TM_IN, s)
    nlist = keys.shape[0]
    cuts, a = [], 0
    for y in (ys, yp, yb):
        cuts.append((a, a + y.shape[-1]))
        a += y.shape[-1]
    tok = lambda w: pl.BlockSpec((1, tm, w), lambda b, i: (b, i, 0))
    vec = pl.BlockSpec((1, 1, d), lambda b, i: (b, 0, 0))
    full2 = lambda arr: pl.BlockSpec(arr.shape, lambda b, i: (0,) * arr.ndim)
    nblk = s // tm
    return pl.pallas_call(
        functools.partial(_out_kernel, cuts=tuple(cuts)),
        grid=(bsz, nblk),
        in_specs=[tok(ys.shape[-1]), tok(yp.shape[-1]), tok(yb.shape[-1]), tok(d), vec, vec, vec,
                  pl.BlockSpec((1, d), lambda b, i: (0, 0)), full2(w_out), full2(wq), full2(keys)],
        out_specs=[tok(d), tok(d),
                   pl.BlockSpec((nlist, N_KEYS, tm), lambda b, i: (0, 0, b * nblk + i))],
        out_shape=[jax.ShapeDtypeStruct((bsz, s, d), F32),
                   jax.ShapeDtypeStruct((bsz, s, d), BF16),
                   jax.ShapeDtypeStruct((nlist, N_KEYS, bsz * s), F32)],
        compiler_params=_cparams(("arbitrary", "arbitrary")),
        name="out_proj_peer_query",
    )(ys, yp, yb, x, g1, sh, sc, ng, w_out, wq, keys)


def _top_rows(s, k, payload=None):
    nrows = s.shape[0]
    rows = lax.broadcasted_iota(jnp.int32, s.shape, 0).astype(F32)
    vals, picks = [], []
    for _ in range(k):
        m = jnp.max(s, axis=0, keepdims=True)
        idx = jnp.min(jnp.where(s == m, rows, float(nrows)), axis=0, keepdims=True)
        hit = rows == idx
        vals.append(m)
        if payload is None:
            picks.append(idx)
        else:
            picks.append(jnp.sum(jnp.where(hit, payload, 0.0), axis=0, keepdims=True))
        s = jnp.where(hit, -jnp.inf, s)
    return jnp.concatenate(vals, axis=0), jnp.concatenate(picks, axis=0)


def _pair_candidates(v1, x1, v2, x2):
    K = PEER_TOPK
    cands, ids = [], []
    a = 0
    while K // (a + 1) > 1:
        n = K // (a + 1)
        npad = -(-n // SUBLANES) * SUBLANES
        c = v1[a:a + 1, :] + v2[0:npad, :]
        if n < npad:
            c = jnp.where(lax.broadcasted_iota(jnp.int32, c.shape, 0) < n, c, -jnp.inf)
        cands.append(c)
        ids.append(x1[a:a + 1, :] * float(N_KEYS) + x2[0:npad, :])
        a += 1
    cands.append(v1[a:K, :] + v2[0:1, :])
    ids.append(x1[a:K, :] * float(N_KEYS) + x2[0:1, :])
    return jnp.concatenate(cands, axis=0), jnp.concatenate(ids, axis=0)


def _peer_kernel(sct_ref, h_ref, u_ref, v_ref, x_ref, g2_ref, fg_ref, o_ref,
                 pk_ref, pt_ref, scr_ref, g_ref, acc_ref, *, final):
    s = pl.program_id(0)
    j = pl.program_id(1)
    K = PEER_TOPK
    slot = s % 2
    eb = u_ref.shape[0]

    @pl.when(jnp.logical_and(s == 0, j == 0))
    def _():
        pk_ref[...] = jnp.zeros_like(pk_ref)
        scr_ref[...] = jnp.zeros_like(scr_ref)
        g_ref[...] = jnp.zeros_like(g_ref)

    @pl.when(j == 0)
    def _():
        acc_ref[...] = jnp.zeros_like(acc_ref)
        for k in range(pk_ref.shape[0]):
            pt_ref[k] = pk_ref[k].T

    v1, x1 = _top_rows(sct_ref[0], K)
    v2, x2 = _top_rows(sct_ref[1], K)
    cand, ids = _pair_candidates(v1, x1, v2, x2)
    best, expert = _top_rows(cand, K, payload=ids)
    e = jnp.exp(best - best[0:1, :])
    gate = e / jnp.sum(e, axis=0, keepdims=True)
    i1 = jnp.floor(expert * (1.0 / N_KEYS))
    r0 = pl.multiple_of(j * K, K)
    pk_ref[0, pl.ds(r0, K), :] = i1
    pk_ref[1, pl.ds(r0, K), :] = expert - i1 * float(N_KEYS)
    pk_ref[2, pl.ds(r0, K), :] = 0.5 * gate

    npair = pt_ref.shape[-1]
    sub = lax.broadcasted_iota(jnp.int32, (N_KEYS, npair), 0).astype(F32)
    t0 = pl.multiple_of(j * GATE_GROUP, GATE_GROUP)
    for tl in range(GATE_GROUP):
        r = pl.ds(t0 + tl, 1)
        lhs = jnp.where(sub == pt_ref[0, r, :], pt_ref[2, r, :], 0.0).astype(BF16)
        rhs = jnp.where(sub == pt_ref[1, r, :], 1.0, 0.0).astype(BF16)
        scr_ref[pl.ds(tl, N_KEYS, stride=GATE_PITCH), :] = _nt_dot(lhs, rhs)
    for a in range(N_KEYS):
        g_ref[1 - slot, a, pl.ds(t0, GATE_GROUP), :] = (
            scr_ref[a * GATE_PITCH:a * GATE_PITCH + GATE_GROUP, :].astype(g_ref.dtype))

    h = h_ref[...]
    a0 = j * (eb // N_KEYS)
    pieces = []
    for q in range(eb // MXU_WIDTH):
        act = _nt_dot(h, u_ref[q * MXU_WIDTH:(q + 1) * MXU_WIDTH, :])
        w = (act * (1.0 + lax.erf(act * (2.0 ** -0.5)))).astype(BF16)
        for a in range(MXU_WIDTH // N_KEYS):
            ga = g_ref[slot, a0 + q * (MXU_WIDTH // N_KEYS) + a]
            pieces.append(w[:, a * N_KEYS:(a + 1) * N_KEYS] * ga)
    coef = jnp.concatenate(pieces, axis=1)
    acc_ref[...] += jnp.dot(coef, v_ref[...], preferred_element_type=F32)

    @pl.when(j == pl.num_programs(1) - 1)
    def _():
        x2 = x_ref[...] + g2_ref[0] * acc_ref[...]
        if final:
            ms = jnp.mean(x2 * x2, axis=-1, keepdims=True)
            x2 = x2 * lax.rsqrt(ms + EPS) * fg_ref[...]
        o_ref[...] = x2


def _peer_call(sct, h2, u, v, x1, g2, fg, final):
    bsz, s, d = x1.shape
    t = bsz * s
    ne = u.shape[0]
    nlist, nk, _ = sct.shape
    tb = min(TB_PEER, s)
    eb = ne // PEER_HEADS
    npair = PEER_HEADS * PEER_TOPK
    assert tb == PEER_HEADS * GATE_GROUP and nlist == 2 * PEER_HEADS and eb % MXU_WIDTH == 0
    nt = t // tb
    per_b = s // tb
    dense_tile = lambda i: jnp.clip(i - 2, 0, nt - 1)
    row = lambda i, j: (dense_tile(i), 0)
    out = pl.pallas_call(
        functools.partial(_peer_kernel, final=final),
        grid=(nt + 2, PEER_HEADS),
        in_specs=[pl.BlockSpec((2, nk, tb), lambda i, j: (j, 0, jnp.minimum(i, nt - 1))),
                  pl.BlockSpec((tb, d), row),
                  pl.BlockSpec((eb, d), lambda i, j: (j, 0)),
                  pl.BlockSpec((eb, d), lambda i, j: (j, 0)),
                  pl.BlockSpec((tb, d), row),
                  pl.BlockSpec((1, 1, d), lambda i, j: (dense_tile(i) // per_b, 0, 0)),
                  pl.BlockSpec((1, d), lambda i, j: (0, 0))],
        out_specs=pl.BlockSpec((tb, d), row),
        out_shape=jax.ShapeDtypeStruct((t, d), F32),
        scratch_shapes=[pltpu.VMEM((3, npair, tb), F32),
                        pltpu.VMEM((3, tb, npair), F32),
                        pltpu.VMEM((N_KEYS * GATE_PITCH, N_KEYS), F32),
                        pltpu.VMEM((2, N_KEYS, tb, N_KEYS), BF16),
                        pltpu.VMEM((tb, d), F32)],
        compiler_params=_cparams(("arbitrary", "arbitrary")),
        name="peer_fused",
    )(sct, h2.reshape(t, d), u, v, x1.reshape(t, d), g2, fg)
    return out.reshape(bsz, s, d)


def kernel(x, c, ada_w, ada_b, mix_norm_g, ffn_norm_g, w_in, conv_w, conv_b, dt_bias, a_log, d_skip,
           ssd_norm_g, pool_w, pool_scale, w_out, peer_wq, peer_keys, peer_u, peer_v, final_norm_g):
    bsz, s, d = x.shape
    depth = ada_w.shape[0]
    ssd_w = ssd_norm_g.shape[-1]
    heads = dt_bias.shape[-1]
    xbc_w = conv_w.shape[-1]
    pool_wd = pool_scale.shape[-1]
    sb_w = (w_in.shape[-1] - ssd_w - xbc_w - heads - pool_wd) // 3
    o_z, o_xbc, o_dt, o_xp = 0, ssd_w, ssd_w + xbc_w, ssd_w + xbc_w + heads
    o_q = o_xp + pool_wd
    widths = (ssd_w, xbc_w, pool_wd, sb_w, sb_w, sb_w, LANES)

    mods = _mod_call(c, ada_w, ada_b)
    for l in range(depth):
        mod = [mods[l, :, k * d:(k + 1) * d].reshape(bsz, 1, d) for k in range(6)]
        sh1, sc1, g1, sh2, sc2, g2 = mod
        wl = w_in[l]
        w_cat = jnp.concatenate(
            [wl[:, o_z:o_xbc], wl[:, o_xbc:o_dt], wl[:, o_xp:o_q], wl[:, o_q:],
             jnp.pad(wl[:, o_dt:o_xp], ((0, 0), (0, LANES - heads)))], axis=1).astype(BF16)
        z, xbc, xp, q, k, v, dt = _in_call(x, sh1, sc1, mix_norm_g[l].reshape(1, d), w_cat, widths)
        y_ssd = _ssd_call(xbc, dt, z, conv_w[l], conv_b[l], dt_bias[l], a_log[l], d_skip[l],
                          ssd_norm_g[l])
        y_pool = _pool_call(xp, pool_w[l], pool_scale[l])
        y_sb = _sb_call(q, k, v)
        keys = peer_keys[l].reshape(-1, N_KEYS, PEER_HALF).astype(BF16)
        x1, h2, sct = _out_call(y_ssd, y_pool, y_sb, x, g1, sh2, sc2, ffn_norm_g[l].reshape(1, d),
                                w_out[l].astype(BF16), peer_wq[l].astype(BF16), keys)
        x = _peer_call(sct, h2, peer_u[l].astype(BF16), peer_v[l].astype(BF16), x1, g2,
                       final_norm_g.reshape(1, d), final=(l == depth - 1))
    return x
```

```python
import functools

import jax
import jax.numpy as jnp
from jax import lax
from jax.experimental import pallas as pl
from jax.experimental.pallas import tpu as pltpu

F32 = jnp.float32
BF16 = jnp.bfloat16
EPS = 1e-6
HIGHEST = lax.Precision.HIGHEST

SSD_HEAD_DIM = 64
SSD_GROUPS = 2
SSD_STATE = 64
CONV_WIDTH = 4
POOL_WINDOWS = (2, 4, 8, 16)
SB_HEAD_DIM = 64
PEER_HEADS = 8
PEER_TOPK = 16
N_KEYS = 128
PEER_HALF = 128

LANES = 128
SUBLANES = 8
MXU_WIDTH = 256
VMEM_LIMIT = 56 * 1024 * 1024

TM_IN = 512
SSD_CHUNK = 128
HALO = 16
TM_POOL = 512
SB_BLOCK = 128
SB_SKIP = 120.0
GATE_GROUP = 32
GATE_PITCH = 40
TB_PEER = 256
DENSE_SUB = 256


def _cparams(sem):
    return pltpu.CompilerParams(dimension_semantics=sem, vmem_limit_bytes=VMEM_LIMIT)


def _nt_dot(a, b):
    return lax.dot_general(a, b, (((1,), (1,)), ((), ())), preferred_element_type=F32)


def _softplus(x):
    return jnp.maximum(x, 0.0) + jnp.log1p(jnp.exp(-jnp.abs(x)))


def _silu(x):
    return x * jax.nn.sigmoid(x)


def _mod_kernel(c_ref, w_ref, b_ref, o_ref):
    cond = _silu(c_ref[...])
    o_ref[0] = jnp.dot(cond, w_ref[0], preferred_element_type=F32, precision=HIGHEST) + b_ref[0]


def _mod_call(c, ada_w, ada_b):
    nl, d, n6 = ada_w.shape
    bsz = c.shape[0]
    tn = 1536
    return pl.pallas_call(
        _mod_kernel,
        grid=(nl, n6 // tn),
        in_specs=[pl.BlockSpec((bsz, d), lambda l, j: (0, 0)),
                  pl.BlockSpec((1, d, tn), lambda l, j: (l, 0, j)),
                  pl.BlockSpec((1, 1, tn), lambda l, j: (l, 0, j))],
        out_specs=pl.BlockSpec((1, bsz, tn), lambda l, j: (l, 0, j)),
        out_shape=jax.ShapeDtypeStruct((nl, bsz, n6), F32),
        compiler_params=_cparams(("arbitrary", "arbitrary")),
        name="adaln_mod",
    )(c, ada_w, ada_b.reshape(nl, 1, n6))


def _norm_mod(x, g, sh, sc):
    ms = jnp.mean(x * x, axis=-1, keepdims=True)
    y = x * lax.rsqrt(ms + EPS) * g
    return y * (1.0 + sc) + sh


def _in_kernel(x_ref, sh_ref, sc_ref, g_ref, w_ref, z_ref, xbc_ref, xp_ref, q_ref, k_ref, v_ref,
               dt_ref, *, cuts):
    h = _norm_mod(x_ref[0], g_ref[...], sh_ref[0], sc_ref[0]).astype(BF16)
    outs = (z_ref, xbc_ref, xp_ref, q_ref, k_ref, v_ref, dt_ref)
    for o_ref, (a, b) in zip(outs, cuts):
        o_ref[0] = jnp.dot(h, w_ref[:, a:b], preferred_element_type=F32).astype(o_ref.dtype)


def _in_call(x, sh, sc, g, w_cat, widths):
    bsz, s, d = x.shape
    tm = min(TM_IN, s)
    cuts, a = [], 0
    for w in widths:
        cuts.append((a, a + w))
        a += w
    dtypes = (BF16, BF16, BF16, BF16, BF16, BF16, F32)
    tok = lambda w: pl.BlockSpec((1, tm, w), lambda b, i: (b, i, 0))
    vec = pl.BlockSpec((1, 1, d), lambda b, i: (b, 0, 0))
    return pl.pallas_call(
        functools.partial(_in_kernel, cuts=tuple(cuts)),
        grid=(bsz, s // tm),
        in_specs=[tok(d), vec, vec,
                  pl.BlockSpec((1, d), lambda b, i: (0, 0)),
                  pl.BlockSpec(w_cat.shape, lambda b, i: (0, 0))],
        out_specs=[tok(w) for w in widths],
        out_shape=[jax.ShapeDtypeStruct((bsz, s, w), dt) for w, dt in zip(widths, dtypes)],
        compiler_params=_cparams(("arbitrary", "arbitrary")),
        name="in_proj",
    )(x, sh, sc, g, w_cat)


def _ssd_kernel(xbc_ref, halo_ref, dt_ref, z_ref, cw_ref, cb_ref, dtb_ref, alog_ref, expand_ref,
                dsk_ref, ng_ref, y_ref, state_ref, *, chunk, width, gn):
    i = pl.program_id(1)
    L = chunk
    heads = width // SSD_HEAD_DIM
    hpg = heads // SSD_GROUPS
    half = width // SSD_GROUPS

    @pl.when(i == 0)
    def _():
        state_ref[...] = jnp.zeros_like(state_ref)

    hal = jnp.where(i > 0, halo_ref[0].astype(F32), 0.0)
    full = jnp.concatenate([hal, xbc_ref[0].astype(F32)], axis=0)
    cw = cw_ref[...]
    acc = full * cw[CONV_WIDTH - 1:CONV_WIDTH]
    for j in range(1, CONV_WIDTH):
        acc = acc + pltpu.roll(full, j, 0) * cw[CONV_WIDTH - 1 - j:CONV_WIDTH - j]
    xbc = _silu(acc[HALO:] + cb_ref[...])
    xs = xbc[:, :width]
    bm = xbc[:, width:width + gn]
    cm = xbc[:, width + gn:width + 2 * gn]

    dt = _softplus(dt_ref[0] + dtb_ref[...])
    a = dt * (-jnp.exp(alog_ref[...]))
    row_i = lax.broadcasted_iota(jnp.int32, (L, L), 0)
    col_i = lax.broadcasted_iota(jnp.int32, (L, L), 1)
    tril = row_i >= col_i
    a_cs = jnp.dot(tril.astype(F32), a, preferred_element_type=F32, precision=HIGHEST)
    a_cs_t = a_cs.T
    expand = expand_ref[...]
    dt_x = jnp.dot(dt, expand, preferred_element_type=F32, precision=HIGHEST)
    acs_x = jnp.dot(a_cs, expand, preferred_element_type=F32, precision=HIGHEST)
    alast_x = acs_x[L - 1:L, :]
    xdt = xs * dt_x

    bm_b = bm.astype(BF16)
    cm_b = cm.astype(BF16)
    lane_g = lax.broadcasted_iota(jnp.int32, (L, gn), 1) // SSD_STATE
    lane_h = lax.broadcasted_iota(jnp.int32, (L, half), 1) // SSD_HEAD_DIM

    state = state_ref[...]
    y_off = jnp.dot(cm_b, state.astype(BF16), preferred_element_type=F32) * jnp.exp(acs_x)

    y_halves = []
    for g in range(SSD_GROUPS):
        cb = _nt_dot(jnp.where(lane_g == g, cm_b, jnp.zeros_like(cm_b)), bm_b)
        xdt_g = xdt[:, g * half:(g + 1) * half]
        yh = jnp.zeros((L, half), F32)
        for hh in range(hpg):
            h = g * hpg + hh
            seg = a_cs[:, h:h + 1] - a_cs_t[h:h + 1, :]
            dec = jnp.exp(jnp.where(tril, seg, -1e30))
            m = (cb * dec).astype(BF16)
            rhs = jnp.where(lane_h == hh, xdt_g, 0.0).astype(BF16)
            yh = yh + jnp.dot(m, rhs, preferred_element_type=F32)
        y_halves.append(yh)

    ds_x = jnp.exp(alast_x - acs_x)
    upd = jnp.dot(bm.T.astype(BF16), (xdt * ds_x).astype(BF16), preferred_element_type=F32)
    srow = lax.broadcasted_iota(jnp.int32, (gn, width), 0) // SSD_STATE
    scol = lax.broadcasted_iota(jnp.int32, (gn, width), 1) // half
    state_ref[...] = jnp.exp(alast_x) * state + jnp.where(srow == scol, upd, 0.0)

    zg = _silu(z_ref[0].astype(F32))
    for g in range(SSD_GROUPS):
        sl = slice(g * half, (g + 1) * half)
        yg = (y_halves[g] + y_off[:, sl] + dsk_ref[:, sl] * xs[:, sl]) * zg[:, sl]
        ms = jnp.mean(yg * yg, axis=-1, keepdims=True)
        y_ref[0, :, sl] = (yg * lax.rsqrt(ms + EPS) * ng_ref[:, sl]).astype(y_ref.dtype)


def _ssd_call(xbc, dt, z, conv_w, conv_b, dt_bias, a_log, d_skip, ssd_norm_g):
    bsz, s, c = xbc.shape
    width = z.shape[-1]
    gn = SSD_GROUPS * SSD_STATE
    heads = width // SSD_HEAD_DIM
    L = min(SSD_CHUNK, s)
    pad = lambda v: jnp.zeros((1, LANES), F32).at[0, :heads].set(v)
    expand = (jnp.arange(LANES)[:, None] == (jnp.arange(width)[None, :] // SSD_HEAD_DIM)).astype(F32)
    dsk = jnp.repeat(d_skip, SSD_HEAD_DIM).reshape(1, width)
    hb = L // HALO
    const = lambda shp: pl.BlockSpec(shp, lambda b, i: (0, 0))
    return pl.pallas_call(
        functools.partial(_ssd_kernel, chunk=L, width=width, gn=gn),
        grid=(bsz, s // L),
        in_specs=[pl.BlockSpec((1, L, c), lambda b, i: (b, i, 0)),
                  pl.BlockSpec((1, HALO, c), lambda b, i: (b, jnp.maximum(i * hb - 1, 0), 0)),
                  pl.BlockSpec((1, L, LANES), lambda b, i: (b, i, 0)),
                  pl.BlockSpec((1, L, width), lambda b, i: (b, i, 0)),
                  const((CONV_WIDTH, c)), const((1, c)), const((1, LANES)), const((1, LANES)),
                  const((LANES, width)), const((1, width)), const((1, width))],
        out_specs=pl.BlockSpec((1, L, width), lambda b, i: (b, i, 0)),
        out_shape=jax.ShapeDtypeStruct((bsz, s, width), BF16),
        scratch_shapes=[pltpu.VMEM((gn, width), F32)],
        compiler_params=_cparams(("arbitrary", "arbitrary")),
        name="ssd_mixer",
    )(xbc, xbc, dt, z, conv_w, conv_b.reshape(1, c), pad(dt_bias), pad(a_log), expand, dsk,
      ssd_norm_g.reshape(1, width))


def _pool_kernel(xp_ref, halo_ref, w_ref, sc_ref, y_ref, *, tm, gdim):
    i = pl.program_id(1)
    hal = jnp.where(i > 0, halo_ref[0].astype(F32), 0.0)
    x = xp_ref[0].astype(F32)
    full = jnp.concatenate([hal, x], axis=0)
    lane_g = lax.broadcasted_iota(jnp.int32, x.shape, 1) // gdim
    tpos = i * tm + lax.broadcasted_iota(jnp.int32, x.shape, 0)
    win_sum = jnp.zeros_like(x)
    cnt = jnp.ones_like(x)
    s = full
    span = 1
    for gi, w in enumerate(POOL_WINDOWS):
        while span < w:
            s = s + pltpu.roll(s, span, 0)
            span *= 2
        win_sum = jnp.where(lane_g == gi, s[HALO:], win_sum)
        cnt = jnp.where(lane_g == gi, jnp.minimum(tpos + 1, w).astype(F32), cnt)
    pooled = (win_sum / cnt - x).astype(BF16)
    y = jnp.dot(pooled, w_ref[...], preferred_element_type=F32) * sc_ref[...]
    y_ref[0] = y.astype(y_ref.dtype)


def _pool_call(xp, pool_w, pool_scale):
    bsz, s, w = xp.shape
    ng, gdim, _ = pool_w.shape
    assert POOL_WINDOWS[-1] <= HALO and all(b == 2 * a for a, b in zip(POOL_WINDOWS, POOL_WINDOWS[1:]))
    tm = min(TM_POOL, s)
    wbd = jnp.zeros((w, w), F32)
    for g in range(ng):
        wbd = wbd.at[g * gdim:(g + 1) * gdim, g * gdim:(g + 1) * gdim].set(pool_w[g])
    hb = tm // HALO
    return pl.pallas_call(
        functools.partial(_pool_kernel, tm=tm, gdim=gdim),
        grid=(bsz, s // tm),
        in_specs=[pl.BlockSpec((1, tm, w), lambda b, i: (b, i, 0)),
                  pl.BlockSpec((1, HALO, w), lambda b, i: (b, jnp.maximum(i * hb - 1, 0), 0)),
                  pl.BlockSpec((w, w), lambda b, i: (0, 0)),
                  pl.BlockSpec((1, w), lambda b, i: (0, 0))],
        out_specs=pl.BlockSpec((1, tm, w), lambda b, i: (b, i, 0)),
        out_shape=jax.ShapeDtypeStruct((bsz, s, w), BF16),
        compiler_params=_cparams(("arbitrary", "arbitrary")),
        name="pool_mixer",
    )(xp, xp, wbd.astype(BF16), pool_scale.reshape(1, w))


def _sb_kernel(q_ref, k_ref, v_ref, y_ref, qm_ref, acc_ref, carry_ref, *, blk, heads):
    i = pl.program_id(1)
    T = blk
    w = q_ref.shape[-1]
    lane_h = lax.broadcasted_iota(jnp.int32, (T, w), 1) // SB_HEAD_DIM
    q = q_ref[0] * (SB_HEAD_DIM ** -0.5)
    for h in range(heads):
        qm_ref[h * T:(h + 1) * T, :] = jnp.where(lane_h == h, q, jnp.zeros_like(q))
    row_i = lax.broadcasted_iota(jnp.int32, (T, T), 0)
    col_i = lax.broadcasted_iota(jnp.int32, (T, T), 1)
    suffix = (row_i > col_i).astype(BF16)
    qrow = lax.broadcasted_iota(jnp.int32, (heads * T, T), 0) % T
    causal = lax.broadcasted_iota(jnp.int32, (heads * T, T), 1) < qrow

    def scores(j, mask):
        ks = k_ref[0, pl.ds(pl.multiple_of(j * T, T), T), :]
        z = _nt_dot(qm_ref[...], ks)
        lk = -_softplus(z)
        if mask is not None:
            lk = jnp.where(mask, lk, 0.0)
        hi = lk.astype(BF16)
        lo = (lk - hi.astype(F32)).astype(BF16)
        later = (jnp.dot(hi, suffix, preferred_element_type=F32)
                 + jnp.dot(lo, suffix, preferred_element_type=F32))
        return z, lk, later

    def weighted(j, z, lk, later, c, mask):
        vs = v_ref[0, pl.ds(pl.multiple_of(j * T, T), T), :]
        wgt = jnp.exp(z + lk + later + c)
        if mask is not None:
            wgt = jnp.where(mask, wgt, 0.0)
        pv = jnp.dot(wgt.astype(BF16), vs, preferred_element_type=F32)
        out = pv[0:T]
        for h in range(1, heads):
            out = jnp.where(lane_h == h, pv[h * T:(h + 1) * T], out)
        return out, c + jnp.sum(lk, axis=1, keepdims=True)

    def alive():
        return (jnp.max(carry_ref[...]) > -SB_SKIP).astype(jnp.int32)

    prev = jnp.maximum(i - 1, 0)
    has_prev = jnp.broadcast_to(i > 0, causal.shape)
    zd, lkd, ltd = scores(i, causal)
    zp, lkp, ltp = scores(prev, has_prev)
    out_d, c = weighted(i, zd, lkd, ltd, jnp.zeros((heads * T, 1), F32), causal)
    out_p, c = weighted(prev, zp, lkp, ltp, c, has_prev)
    acc_ref[...] = out_d + out_p
    carry_ref[...] = c

    def cond(st):
        return jnp.logical_and(st[0] >= 0, st[1] > 0)

    def body(st):
        j = st[0]
        z, lk, later = scores(j, None)
        out, c = weighted(j, z, lk, later, carry_ref[...], None)
        acc_ref[...] += out
        carry_ref[...] = c
        return j - 1, alive()

    lax.while_loop(cond, body, (i - 2, alive()))
    y_ref[0] = acc_ref[...].astype(y_ref.dtype)


def _sb_call(q, k, v):
    bsz, s, w = q.shape
    heads = w // SB_HEAD_DIM
    T = min(SB_BLOCK, s)
    return pl.pallas_call(
        functools.partial(_sb_kernel, blk=T, heads=heads),
        grid=(bsz, s // T),
        in_specs=[pl.BlockSpec((1, T, w), lambda b, i: (b, i, 0)),
                  pl.BlockSpec((1, s, w), lambda b, i: (b, 0, 0)),
                  pl.BlockSpec((1, s, w), lambda b, i: (b, 0, 0))],
        out_specs=pl.BlockSpec((1, T, w), lambda b, i: (b, i, 0)),
        out_shape=jax.ShapeDtypeStruct((bsz, s, w), BF16),
        scratch_shapes=[pltpu.VMEM((heads * T, w), BF16),
                        pltpu.VMEM((T, w), F32),
                        pltpu.VMEM((heads * T, 1), F32)],
        compiler_params=_cparams(("arbitrary", "arbitrary")),
        name="stick_breaking",
    )(q, k, v)


def _out_kernel(ys_ref, yp_ref, yb_ref, x_ref, g1_ref, sh_ref, sc_ref, ng_ref, wo_ref, wq_ref,
                keys_ref, x1_ref, h2_ref, sct_ref, *, cuts):
    mix = jnp.zeros(x_ref.shape[1:], F32)
    for y_ref, (a, b) in zip((ys_ref, yp_ref, yb_ref), cuts):
        mix = mix + jnp.dot(y_ref[0], wo_ref[a:b, :], preferred_element_type=F32)
    x1 = x_ref[0] + g1_ref[0] * mix
    x1_ref[0] = x1
    h2 = _norm_mod(x1, ng_ref[...], sh_ref[0], sc_ref[0]).astype(BF16)
    h2_ref[0] = h2
    qb = jnp.dot(h2, wq_ref[...], preferred_element_type=F32).astype(BF16)
    for lst in range(keys_ref.shape[0]):
        sct_ref[lst] = _nt_dot(keys_ref[lst], qb[:, lst * PEER_HALF:(lst + 1) * PEER_HALF])


def _out_call(ys, yp, yb, x, g1, sh, sc, ng, w_out, wq, keys):
    bsz, s, d = x.shape
    tm = min(TM_IN, s)
    nlist = keys.shape[0]
    cuts, a = [], 0
    for y in (ys, yp, yb):
        cuts.append((a, a + y.shape[-1]))
        a += y.shape[-1]
    tok = lambda w: pl.BlockSpec((1, tm, w), lambda b, i: (b, i, 0))
    vec = pl.BlockSpec((1, 1, d), lambda b, i: (b, 0, 0))
    full2 = lambda arr: pl.BlockSpec(arr.shape, lambda b, i: (0,) * arr.ndim)
    nblk = s // tm
    return pl.pallas_call(
        functools.partial(_out_kernel, cuts=tuple(cuts)),
        grid=(bsz, nblk),
        in_specs=[tok(ys.shape[-1]), tok(yp.shape[-1]), tok(yb.shape[-1]), tok(d), vec, vec, vec,
                  pl.BlockSpec((1, d), lambda b, i: (0, 0)), full2(w_out), full2(wq), full2(keys)],
        out_specs=[tok(d), tok(d),
                   pl.BlockSpec((nlist, N_KEYS, tm), lambda b, i: (0, 0, b * nblk + i))],
        out_shape=[jax.ShapeDtypeStruct((bsz, s, d), F32),
                   jax.ShapeDtypeStruct((bsz, s, d), BF16),
                   jax.ShapeDtypeStruct((nlist, N_KEYS, bsz * s), F32)],
        compiler_params=_cparams(("arbitrary", "arbitrary")),
        name="out_proj_peer_query",
    )(ys, yp, yb, x, g1, sh, sc, ng, w_out, wq, keys)


def _top_rows(s, k, payload=None):
    nrows = s.shape[0]
    rows = lax.broadcasted_iota(jnp.int32, s.shape, 0).astype(F32)
    vals, picks = [], []
    for _ in range(k):
        m = jnp.max(s, axis=0, keepdims=True)
        idx = jnp.min(jnp.where(s == m, rows, float(nrows)), axis=0, keepdims=True)
        hit = rows == idx
        vals.append(m)
        if payload is None:
            picks.append(idx)
        else:
            picks.append(jnp.sum(jnp.where(hit, payload, 0.0), axis=0, keepdims=True))
        s = jnp.where(hit, -jnp.inf, s)
    return jnp.concatenate(vals, axis=0), jnp.concatenate(picks, axis=0)


def _pair_candidates(v1, x1, v2, x2):
    K = PEER_TOPK
    cands, ids = [], []
    a = 0
    while K // (a + 1) > 1:
        n = K // (a + 1)
        npad = -(-n // SUBLANES) * SUBLANES
        c = v1[a:a + 1, :] + v2[0:npad, :]
        if n < npad:
            c = jnp.where(lax.broadcasted_iota(jnp.int32, c.shape, 0) < n, c, -jnp.inf)
        cands.append(c)
        ids.append(x1[a:a + 1, :] * float(N_KEYS) + x2[0:npad, :])
        a += 1
    cands.append(v1[a:K, :] + v2[0:1, :])
    ids.append(x1[a:K, :] * float(N_KEYS) + x2[0:1, :])
    return jnp.concatenate(cands, axis=0), jnp.concatenate(ids, axis=0)


def _peer_kernel(sct_ref, h_ref, u_ref, v_ref, x_ref, g2_ref, fg_ref, o_ref,
                 pk_ref, pt_ref, scr_ref, g_ref, acc_ref, *, final):
    s = pl.program_id(0)
    j = pl.program_id(1)
    K = PEER_TOPK
    slot = s % 2
    eb = u_ref.shape[0]

    @pl.when(jnp.logical_and(s == 0, j == 0))
    def _():
        pk_ref[...] = jnp.zeros_like(pk_ref)
        scr_ref[...] = jnp.zeros_like(scr_ref)
        g_ref[...] = jnp.zeros_like(g_ref)

    @pl.when(j == 0)
    def _():
        acc_ref[...] = jnp.zeros_like(acc_ref)
        for k in range(pk_ref.shape[0]):
            pt_ref[k] = pk_ref[k].T

    v1, x1 = _top_rows(sct_ref[0], K)
    v2, x2 = _top_rows(sct_ref[1], K)
    cand, ids = _pair_candidates(v1, x1, v2, x2)
    best, expert = _top_rows(cand, K, payload=ids)
    e = jnp.exp(best - best[0:1, :])
    gate = e / jnp.sum(e, axis=0, keepdims=True)
    i1 = jnp.floor(expert * (1.0 / N_KEYS))
    r0 = pl.multiple_of(j * K, K)
    pk_ref[0, pl.ds(r0, K), :] = i1
    pk_ref[1, pl.ds(r0, K), :] = expert - i1 * float(N_KEYS)
    pk_ref[2, pl.ds(r0, K), :] = 0.5 * gate

    npair = pt_ref.shape[-1]
    sub = lax.broadcasted_iota(jnp.int32, (N_KEYS, npair), 0).astype(F32)
    t0 = pl.multiple_of(j * GATE_GROUP, GATE_GROUP)
    for tl in range(GATE_GROUP):
        r = pl.ds(t0 + tl, 1)
        lhs = jnp.where(sub == pt_ref[0, r, :], pt_ref[2, r, :], 0.0).astype(BF16)
        rhs = jnp.where(sub == pt_ref[1, r, :], 1.0, 0.0).astype(BF16)
        scr_ref[pl.ds(tl, N_KEYS, stride=GATE_PITCH), :] = _nt_dot(lhs, rhs)
    for a in range(N_KEYS):
        g_ref[1 - slot, a, pl.ds(t0, GATE_GROUP), :] = (
            scr_ref[a * GATE_PITCH:a * GATE_PITCH + GATE_GROUP, :].astype(g_ref.dtype))

    h = h_ref[...]
    d = h.shape[-1]
    a0 = j * (eb // N_KEYS)
    pieces = []
    for q in range(eb // DENSE_SUB):
        act = _nt_dot(h, u_ref[q * DENSE_SUB:(q + 1) * DENSE_SUB, :d])
        w = (act * (1.0 + lax.erf(act * (2.0 ** -0.5)))).astype(BF16)
        for a in range(DENSE_SUB // N_KEYS):
            ga = g_ref[slot, a0 + q * (DENSE_SUB // N_KEYS) + a]
            pieces.append(w[:, a * N_KEYS:(a + 1) * N_KEYS] * ga)
    coef = jnp.concatenate(pieces, axis=1)
    acc_ref[...] += jnp.dot(coef, v_ref[:, :d], preferred_element_type=F32)

    @pl.when(j == pl.num_programs(1) - 1)
    def _():
        x2 = x_ref[...] + g2_ref[0] * acc_ref[...]
        if final:
            ms = jnp.mean(x2 * x2, axis=-1, keepdims=True)
            x2 = x2 * lax.rsqrt(ms + EPS) * fg_ref[...]
        o_ref[...] = x2


def _peer_call(sct, h2, u, v, x1, g2, fg, final):
    bsz, s, d = x1.shape
    t = bsz * s
    ne = u.shape[0]
    nlist, nk, _ = sct.shape
    tb = min(TB_PEER, s)
    eb = ne // PEER_HEADS
    npair = PEER_HEADS * PEER_TOPK
    assert tb == PEER_HEADS * GATE_GROUP and nlist == 2 * PEER_HEADS and eb % MXU_WIDTH == 0
    nt = t // tb
    per_b = s // tb
    dense_tile = lambda i: jnp.clip(i - 2, 0, nt - 1)
    row = lambda i, j: (dense_tile(i), 0)
    dp = d + LANES if (d // LANES) % 2 == 0 else d
    u = jnp.pad(u, ((0, 0), (0, dp - d)))
    v = jnp.pad(v, ((0, 0), (0, dp - d)))
    out = pl.pallas_call(
        functools.partial(_peer_kernel, final=final),
        grid=(nt + 2, PEER_HEADS),
        in_specs=[pl.BlockSpec((2, nk, tb), lambda i, j: (j, 0, jnp.minimum(i, nt - 1))),
                  pl.BlockSpec((tb, d), row),
                  pl.BlockSpec((eb, dp), lambda i, j: (j, 0)),
                  pl.BlockSpec((eb, dp), lambda i, j: (j, 0)),
                  pl.BlockSpec((tb, d), row),
                  pl.BlockSpec((1, 1, d), lambda i, j: (dense_tile(i) // per_b, 0, 0)),
                  pl.BlockSpec((1, d), lambda i, j: (0, 0))],
        out_specs=pl.BlockSpec((tb, d), row),
        out_shape=jax.ShapeDtypeStruct((t, d), F32),
        scratch_shapes=[pltpu.VMEM((3, npair, tb), F32),
                        pltpu.VMEM((3, tb, npair), F32),
                        pltpu.VMEM((N_KEYS * GATE_PITCH, N_KEYS), F32),
                        pltpu.VMEM((2, N_KEYS, tb, N_KEYS), BF16),
                        pltpu.VMEM((tb, d), F32)],
        compiler_params=_cparams(("arbitrary", "arbitrary")),
        name="peer_fused",
    )(sct, h2.reshape(t, d), u, v, x1.reshape(t, d), g2, fg)
    return out.reshape(bsz, s, d)


def kernel(x, c, ada_w, ada_b, mix_norm_g, ffn_norm_g, w_in, conv_w, conv_b, dt_bias, a_log, d_skip,
           ssd_norm_g, pool_w, pool_scale, w_out, peer_wq, peer_keys, peer_u, peer_v, final_norm_g):
    bsz, s, d = x.shape
    depth = ada_w.shape[0]
    ssd_w = ssd_norm_g.shape[-1]
    heads = dt_bias.shape[-1]
    xbc_w = conv_w.shape[-1]
    pool_wd = pool_scale.shape[-1]
    sb_w = (w_in.shape[-1] - ssd_w - xbc_w - heads - pool_wd) // 3
    o_z, o_xbc, o_dt, o_xp = 0, ssd_w, ssd_w + xbc_w, ssd_w + xbc_w + heads
    o_q = o_xp + pool_wd
    widths = (ssd_w, xbc_w, pool_wd, sb_w, sb_w, sb_w, LANES)

    mods = _mod_call(c, ada_w, ada_b)
    for l in range(depth):
        mod = [mods[l, :, k * d:(k + 1) * d].reshape(bsz, 1, d) for k in range(6)]
        sh1, sc1, g1, sh2, sc2, g2 = mod
        wl = w_in[l]
        w_cat = jnp.concatenate(
            [wl[:, o_z:o_xbc], wl[:, o_xbc:o_dt], wl[:, o_xp:o_q], wl[:, o_q:],
             jnp.pad(wl[:, o_dt:o_xp], ((0, 0), (0, LANES - heads)))], axis=1).astype(BF16)
        z, xbc, xp, q, k, v, dt = _in_call(x, sh1, sc1, mix_norm_g[l].reshape(1, d), w_cat, widths)
        y_ssd = _ssd_call(xbc, dt, z, conv_w[l], conv_b[l], dt_bias[l], a_log[l], d_skip[l],
                          ssd_norm_g[l])
        y_pool = _pool_call(xp, pool_w[l], pool_scale[l])
        y_sb = _sb_call(q, k, v)
        keys = peer_keys[l].reshape(-1, N_KEYS, PEER_HALF).astype(BF16)
        x1, h2, sct = _out_call(y_ssd, y_pool, y_sb, x, g1, sh2, sc2, ffn_norm_g[l].reshape(1, d),
                                w_out[l].astype(BF16), peer_wq[l].astype(BF16), keys)
        x = _peer_call(sct, h2, peer_u[l].astype(BF16), peer_v[l].astype(BF16), x1, g2,
                       final_norm_g.reshape(1, d), final=(l == depth - 1))
    return x
```

```python
import functools

import jax
import jax.numpy as jnp
from jax import lax
from jax.experimental import pallas as pl
from jax.experimental.pallas import tpu as pltpu

F32 = jnp.float32
BF16 = jnp.bfloat16
EPS = 1e-6
HIGHEST = lax.Precision.HIGHEST

SSD_HEAD_DIM = 64
SSD_GROUPS = 2
SSD_STATE = 64
CONV_WIDTH = 4
POOL_WINDOWS = (2, 4, 8, 16)
SB_HEAD_DIM = 64
PEER_HEADS = 8
PEER_TOPK = 16
N_KEYS = 128
PEER_HALF = 128

LANES = 128
SUBLANES = 8
MXU_WIDTH = 256
VMEM_LIMIT = 56 * 1024 * 1024

TM_IN = 512
SSD_CHUNK = 256
HALO = 16
TM_POOL = 512
SB_BLOCK = 128
SB_SKIP = 120.0
GATE_GROUP = 32
GATE_PITCH = 40
TB_PEER = 256
DENSE_SUB = 256


def _cparams(sem):
    return pltpu.CompilerParams(dimension_semantics=sem, vmem_limit_bytes=VMEM_LIMIT)


def _nt_dot(a, b):
    return lax.dot_general(a, b, (((1,), (1,)), ((), ())), preferred_element_type=F32)


def _softplus(x):
    return jnp.maximum(x, 0.0) + jnp.log(1.0 + jnp.exp(-jnp.abs(x)))


def _silu(x):
    return x * jax.nn.sigmoid(x)


def _mod_kernel(c_ref, w_ref, b_ref, o_ref):
    cond = _silu(c_ref[...])
    o_ref[0] = jnp.dot(cond, w_ref[0], preferred_element_type=F32, precision=HIGHEST) + b_ref[0]


def _mod_call(c, ada_w, ada_b):
    nl, d, n6 = ada_w.shape
    bsz = c.shape[0]
    tn = 1536
    return pl.pallas_call(
        _mod_kernel,
        grid=(nl, n6 // tn),
        in_specs=[pl.BlockSpec((bsz, d), lambda l, j: (0, 0)),
                  pl.BlockSpec((1, d, tn), lambda l, j: (l, 0, j)),
                  pl.BlockSpec((1, 1, tn), lambda l, j: (l, 0, j))],
        out_specs=pl.BlockSpec((1, bsz, tn), lambda l, j: (l, 0, j)),
        out_shape=jax.ShapeDtypeStruct((nl, bsz, n6), F32),
        compiler_params=_cparams(("arbitrary", "arbitrary")),
        name="adaln_mod",
    )(c, ada_w, ada_b.reshape(nl, 1, n6))


def _norm_mod(x, g, sh, sc):
    ms = jnp.mean(x * x, axis=-1, keepdims=True)
    y = x * lax.rsqrt(ms + EPS) * g
    return y * (1.0 + sc) + sh


def _in_kernel(x_ref, sh_ref, sc_ref, g_ref, w_ref, z_ref, xbc_ref, xp_ref, q_ref, k_ref, v_ref,
               dt_ref, *, cuts):
    h = _norm_mod(x_ref[0], g_ref[...], sh_ref[0], sc_ref[0]).astype(BF16)
    outs = (z_ref, xbc_ref, xp_ref, q_ref, k_ref, v_ref, dt_ref)
    for o_ref, (a, b) in zip(outs, cuts):
        o_ref[0] = jnp.dot(h, w_ref[:, a:b], preferred_element_type=F32).astype(o_ref.dtype)


def _in_call(x, sh, sc, g, w_cat, widths):
    bsz, s, d = x.shape
    tm = min(TM_IN, s)
    cuts, a = [], 0
    for w in widths:
        cuts.append((a, a + w))
        a += w
    dtypes = (BF16, BF16, BF16, BF16, BF16, BF16, F32)
    tok = lambda w: pl.BlockSpec((1, tm, w), lambda b, i: (b, i, 0))
    vec = pl.BlockSpec((1, 1, d), lambda b, i: (b, 0, 0))
    return pl.pallas_call(
        functools.partial(_in_kernel, cuts=tuple(cuts)),
        grid=(bsz, s // tm),
        in_specs=[tok(d), vec, vec,
                  pl.BlockSpec((1, d), lambda b, i: (0, 0)),
                  pl.BlockSpec(w_cat.shape, lambda b, i: (0, 0))],
        out_specs=[tok(w) for w in widths],
        out_shape=[jax.ShapeDtypeStruct((bsz, s, w), dt) for w, dt in zip(widths, dtypes)],
        compiler_params=_cparams(("arbitrary", "arbitrary")),
        name="in_proj",
    )(x, sh, sc, g, w_cat)


def _ssd_kernel(xbc_ref, halo_ref, dt_ref, z_ref, cw_ref, cb_ref, dtb_ref, alog_ref, expand_ref,
                dsk_ref, ng_ref, y_ref, state_ref, *, chunk, width, gn):
    i = pl.program_id(1)
    L = chunk
    heads = width // SSD_HEAD_DIM
    hpg = heads // SSD_GROUPS
    half = width // SSD_GROUPS

    @pl.when(i == 0)
    def _():
        state_ref[...] = jnp.zeros_like(state_ref)

    hal = jnp.where(i > 0, halo_ref[0].astype(F32), 0.0)
    full = jnp.concatenate([hal, xbc_ref[0].astype(F32)], axis=0)
    cw = cw_ref[...]
    acc = full * cw[CONV_WIDTH - 1:CONV_WIDTH]
    for j in range(1, CONV_WIDTH):
        acc = acc + pltpu.roll(full, j, 0) * cw[CONV_WIDTH - 1 - j:CONV_WIDTH - j]
    xbc = _silu(acc[HALO:] + cb_ref[...])
    xs = xbc[:, :width]
    bm = xbc[:, width:width + gn]
    cm = xbc[:, width + gn:width + 2 * gn]

    dt = _softplus(dt_ref[0] + dtb_ref[...])
    a = dt * (-jnp.exp(alog_ref[...]))
    row_i = lax.broadcasted_iota(jnp.int32, (L, L), 0)
    col_i = lax.broadcasted_iota(jnp.int32, (L, L), 1)
    tril = row_i >= col_i
    a_cs = jnp.dot(tril.astype(F32), a, preferred_element_type=F32, precision=HIGHEST)
    a_cs_t = a_cs.T
    expand = expand_ref[...]
    dt_x = jnp.dot(dt, expand, preferred_element_type=F32, precision=HIGHEST)
    acs_x = jnp.dot(a_cs, expand, preferred_element_type=F32, precision=HIGHEST)
    alast_x = acs_x[L - 1:L, :]
    xdt = xs * dt_x

    bm_b = bm.astype(BF16)
    cm_b = cm.astype(BF16)
    lane_g = lax.broadcasted_iota(jnp.int32, (L, gn), 1) // SSD_STATE
    lane_h = lax.broadcasted_iota(jnp.int32, (L, half), 1) // SSD_HEAD_DIM

    state = state_ref[...]
    y_off = jnp.dot(cm_b, state.astype(BF16), preferred_element_type=F32) * jnp.exp(acs_x)

    y_halves = []
    for g in range(SSD_GROUPS):
        cb = _nt_dot(jnp.where(lane_g == g, cm_b, jnp.zeros_like(cm_b)), bm_b)
        xdt_g = xdt[:, g * half:(g + 1) * half]
        yh = jnp.zeros((L, half), F32)
        for hh in range(hpg):
            h = g * hpg + hh
            seg = a_cs[:, h:h + 1] - a_cs_t[h:h + 1, :]
            dec = jnp.exp(jnp.where(tril, seg, -1e30))
            m = (cb * dec).astype(BF16)
            rhs = jnp.where(lane_h == hh, xdt_g, 0.0).astype(BF16)
            yh = yh + jnp.dot(m, rhs, preferred_element_type=F32)
        y_halves.append(yh)

    ds_x = jnp.exp(alast_x - acs_x)
    upd = jnp.dot(bm.T.astype(BF16), (xdt * ds_x).astype(BF16), preferred_element_type=F32)
    srow = lax.broadcasted_iota(jnp.int32, (gn, width), 0) // SSD_STATE
    scol = lax.broadcasted_iota(jnp.int32, (gn, width), 1) // half
    state_ref[...] = jnp.exp(alast_x) * state + jnp.where(srow == scol, upd, 0.0)

    zg = _silu(z_ref[0].astype(F32))
    for g in range(SSD_GROUPS):
        sl = slice(g * half, (g + 1) * half)
        yg = (y_halves[g] + y_off[:, sl] + dsk_ref[:, sl] * xs[:, sl]) * zg[:, sl]
        ms = jnp.mean(yg * yg, axis=-1, keepdims=True)
        y_ref[0, :, sl] = (yg * lax.rsqrt(ms + EPS) * ng_ref[:, sl]).astype(y_ref.dtype)


def _ssd_call(xbc, dt, z, conv_w, conv_b, dt_bias, a_log, d_skip, ssd_norm_g):
    bsz, s, c = xbc.shape
    width = z.shape[-1]
    gn = SSD_GROUPS * SSD_STATE
    heads = width // SSD_HEAD_DIM
    L = min(SSD_CHUNK, s)
    pad = lambda v: jnp.zeros((1, LANES), F32).at[0, :heads].set(v)
    expand = (jnp.arange(LANES)[:, None] == (jnp.arange(width)[None, :] // SSD_HEAD_DIM)).astype(F32)
    dsk = jnp.repeat(d_skip, SSD_HEAD_DIM).reshape(1, width)
    hb = L // HALO
    const = lambda shp: pl.BlockSpec(shp, lambda b, i: (0, 0))
    return pl.pallas_call(
        functools.partial(_ssd_kernel, chunk=L, width=width, gn=gn),
        grid=(bsz, s // L),
        in_specs=[pl.BlockSpec((1, L, c), lambda b, i: (b, i, 0)),
                  pl.BlockSpec((1, HALO, c), lambda b, i: (b, jnp.maximum(i * hb - 1, 0), 0)),
                  pl.BlockSpec((1, L, LANES), lambda b, i: (b, i, 0)),
                  pl.BlockSpec((1, L, width), lambda b, i: (b, i, 0)),
                  const((CONV_WIDTH, c)), const((1, c)), const((1, LANES)), const((1, LANES)),
                  const((LANES, width)), const((1, width)), const((1, width))],
        out_specs=pl.BlockSpec((1, L, width), lambda b, i: (b, i, 0)),
        out_shape=jax.ShapeDtypeStruct((bsz, s, width), BF16),
        scratch_shapes=[pltpu.VMEM((gn, width), F32)],
        compiler_params=_cparams(("arbitrary", "arbitrary")),
        name="ssd_mixer",
    )(xbc, xbc, dt, z, conv_w, conv_b.reshape(1, c), pad(dt_bias), pad(a_log), expand, dsk,
      ssd_norm_g.reshape(1, width))


def _pool_kernel(xp_ref, halo_ref, w_ref, sc_ref, y_ref, *, tm, gdim):
    i = pl.program_id(1)
    hal = jnp.where(i > 0, halo_ref[0].astype(F32), 0.0)
    x = xp_ref[0].astype(F32)
    full = jnp.concatenate([hal, x], axis=0)
    lane_g = lax.broadcasted_iota(jnp.int32, x.shape, 1) // gdim
    tpos = i * tm + lax.broadcasted_iota(jnp.int32, x.shape, 0)
    win_sum = jnp.zeros_like(x)
    cnt = jnp.ones_like(x)
    s = full
    span = 1
    for gi, w in enumerate(POOL_WINDOWS):
        while span < w:
            s = s + pltpu.roll(s, span, 0)
            span *= 2
        win_sum = jnp.where(lane_g == gi, s[HALO:], win_sum)
        cnt = jnp.where(lane_g == gi, jnp.minimum(tpos + 1, w).astype(F32), cnt)
    pooled = (win_sum / cnt - x).astype(BF16)
    y = jnp.dot(pooled, w_ref[...], preferred_element_type=F32) * sc_ref[...]
    y_ref[0] = y.astype(y_ref.dtype)


def _pool_call(xp, pool_w, pool_scale):
    bsz, s, w = xp.shape
    ng, gdim, _ = pool_w.shape
    assert POOL_WINDOWS[-1] <= HALO and all(b == 2 * a for a, b in zip(POOL_WINDOWS, POOL_WINDOWS[1:]))
    tm = min(TM_POOL, s)
    wbd = jnp.zeros((w, w), F32)
    for g in range(ng):
        wbd = wbd.at[g * gdim:(g + 1) * gdim, g * gdim:(g + 1) * gdim].set(pool_w[g])
    hb = tm // HALO
    return pl.pallas_call(
        functools.partial(_pool_kernel, tm=tm, gdim=gdim),
        grid=(bsz, s // tm),
        in_specs=[pl.BlockSpec((1, tm, w), lambda b, i: (b, i, 0)),
                  pl.BlockSpec((1, HALO, w), lambda b, i: (b, jnp.maximum(i * hb - 1, 0), 0)),
                  pl.BlockSpec((w, w), lambda b, i: (0, 0)),
                  pl.BlockSpec((1, w), lambda b, i: (0, 0))],
        out_specs=pl.BlockSpec((1, tm, w), lambda b, i: (b, i, 0)),
        out_shape=jax.ShapeDtypeStruct((bsz, s, w), BF16),
        compiler_params=_cparams(("arbitrary", "arbitrary")),
        name="pool_mixer",
    )(xp, xp, wbd.astype(BF16), pool_scale.reshape(1, w))


def _sb_kernel(q_ref, k_ref, v_ref, y_ref, qm_ref, acc_ref, carry_ref, *, blk, heads):
    i = pl.program_id(1)
    T = blk
    w = q_ref.shape[-1]
    lane_h = lax.broadcasted_iota(jnp.int32, (T, w), 1) // SB_HEAD_DIM
    q = q_ref[0] * (SB_HEAD_DIM ** -0.5)
    for h in range(heads):
        qm_ref[h * T:(h + 1) * T, :] = jnp.where(lane_h == h, q, jnp.zeros_like(q))
    row_i = lax.broadcasted_iota(jnp.int32, (T, T), 0)
    col_i = lax.broadcasted_iota(jnp.int32, (T, T), 1)
    suffix = (row_i > col_i).astype(BF16)
    qrow = lax.broadcasted_iota(jnp.int32, (heads * T, T), 0) % T
    causal = lax.broadcasted_iota(jnp.int32, (heads * T, T), 1) < qrow

    def scores(j, mask):
        ks = k_ref[0, pl.ds(pl.multiple_of(j * T, T), T), :]
        z = _nt_dot(qm_ref[...], ks)
        lk = -_softplus(z)
        if mask is not None:
            lk = jnp.where(mask, lk, 0.0)
        hi = lk.astype(BF16)
        lo = (lk - hi.astype(F32)).astype(BF16)
        later = (jnp.dot(hi, suffix, preferred_element_type=F32)
                 + jnp.dot(lo, suffix, preferred_element_type=F32))
        return z, lk, later

    def weighted(j, z, lk, later, c, mask):
        vs = v_ref[0, pl.ds(pl.multiple_of(j * T, T), T), :]
        wgt = jnp.exp(z + lk + later + c)
        if mask is not None:
            wgt = jnp.where(mask, wgt, 0.0)
        pv = jnp.dot(wgt.astype(BF16), vs, preferred_element_type=F32)
        out = pv[0:T]
        for h in range(1, heads):
            out = jnp.where(lane_h == h, pv[h * T:(h + 1) * T], out)
        return out, c + jnp.sum(lk, axis=1, keepdims=True)

    def alive():
        return (jnp.max(carry_ref[...]) > -SB_SKIP).astype(jnp.int32)

    prev = jnp.maximum(i - 1, 0)
    has_prev = jnp.broadcast_to(i > 0, causal.shape)
    zd, lkd, ltd = scores(i, causal)
    zp, lkp, ltp = scores(prev, has_prev)
    out_d, c = weighted(i, zd, lkd, ltd, jnp.zeros((heads * T, 1), F32), causal)
    out_p, c = weighted(prev, zp, lkp, ltp, c, has_prev)
    acc_ref[...] = out_d + out_p
    carry_ref[...] = c

    def cond(st):
        return jnp.logical_and(st[0] >= 0, st[1] > 0)

    def body(st):
        j = st[0]
        z, lk, later = scores(j, None)
        out, c = weighted(j, z, lk, later, carry_ref[...], None)
        acc_ref[...] += out
        carry_ref[...] = c
        return j - 1, alive()

    lax.while_loop(cond, body, (i - 2, alive()))
    y_ref[0] = acc_ref[...].astype(y_ref.dtype)


def _sb_call(q, k, v):
    bsz, s, w = q.shape
    heads = w // SB_HEAD_DIM
    T = min(SB_BLOCK, s)
    return pl.pallas_call(
        functools.partial(_sb_kernel, blk=T, heads=heads),
        grid=(bsz, s // T),
        in_specs=[pl.BlockSpec((1, T, w), lambda b, i: (b, i, 0)),
                  pl.BlockSpec((1, s, w), lambda b, i: (b, 0, 0)),
                  pl.BlockSpec((1, s, w), lambda b, i: (b, 0, 0))],
        out_specs=pl.BlockSpec((1, T, w), lambda b, i: (b, i, 0)),
        out_shape=jax.ShapeDtypeStruct((bsz, s, w), BF16),
        scratch_shapes=[pltpu.VMEM((heads * T, w), BF16),
                        pltpu.VMEM((T, w), F32),
                        pltpu.VMEM((heads * T, 1), F32)],
        compiler_params=_cparams(("arbitrary", "arbitrary")),
        name="stick_breaking",
    )(q, k, v)


def _out_kernel(ys_ref, yp_ref, yb_ref, x_ref, g1_ref, sh_ref, sc_ref, ng_ref, wo_ref, wq_ref,
                keys_ref, x1_ref, h2_ref, sct_ref, *, cuts):
    mix = jnp.zeros(x_ref.shape[1:], F32)
    for y_ref, (a, b) in zip((ys_ref, yp_ref, yb_ref), cuts):
        mix = mix + jnp.dot(y_ref[0], wo_ref[a:b, :], preferred_element_type=F32)
    x1 = x_ref[0] + g1_ref[0] * mix
    x1_ref[0] = x1
    h2 = _norm_mod(x1, ng_ref[...], sh_ref[0], sc_ref[0]).astype(BF16)
    h2_ref[0] = h2
    qb = jnp.dot(h2, wq_ref[...], preferred_element_type=F32).astype(BF16)
    for lst in range(keys_ref.shape[0]):
        sct_ref[lst] = _nt_dot(keys_ref[lst], qb[:, lst * PEER_HALF:(lst + 1) * PEER_HALF])


def _out_call(ys, yp, yb, x, g1, sh, sc, ng, w_out, wq, keys):
    bsz, s, d = x.shape
    tm = min(TM_IN, s)
    nlist = keys.shape[0]
    cuts, a = [], 0
    for y in (ys, yp, yb):
        cuts.append((a, a + y.shape[-1]))
        a += y.shape[-1]
    tok = lambda w: pl.BlockSpec((1, tm, w), lambda b, i: (b, i, 0))
    vec = pl.BlockSpec((1, 1, d), lambda b, i: (b, 0, 0))
    full2 = lambda arr: pl.BlockSpec(arr.shape, lambda b, i: (0,) * arr.ndim)
    nblk = s // tm
    return pl.pallas_call(
        functools.partial(_out_kernel, cuts=tuple(cuts)),
        grid=(bsz, nblk),
        in_specs=[tok(ys.shape[-1]), tok(yp.shape[-1]), tok(yb.shape[-1]), tok(d), vec, vec, vec,
                  pl.BlockSpec((1, d), lambda b, i: (0, 0)), full2(w_out), full2(wq), full2(keys)],
        out_specs=[tok(d), tok(d),
                   pl.BlockSpec((nlist, N_KEYS, tm), lambda b, i: (0, 0, b * nblk + i))],
        out_shape=[jax.ShapeDtypeStruct((bsz, s, d), F32),
                   jax.ShapeDtypeStruct((bsz, s, d), BF16),
                   jax.ShapeDtypeStruct((nlist, N_KEYS, bsz * s), F32)],
        compiler_params=_cparams(("arbitrary", "arbitrary")),
        name="out_proj_peer_query",
    )(ys, yp, yb, x, g1, sh, sc, ng, w_out, wq, keys)


def _top_rows(s, k, payload=None):
    nrows = s.shape[0]
    rows = lax.broadcasted_iota(jnp.int32, s.shape, 0).astype(F32)
    vals, picks = [], []
    for _ in range(k):
        m = jnp.max(s, axis=0, keepdims=True)
        idx = jnp.min(jnp.where(s == m, rows, float(nrows)), axis=0, keepdims=True)
        hit = rows == idx
        vals.append(m)
        if payload is None:
            picks.append(idx)
        else:
            picks.append(jnp.sum(jnp.where(hit, payload, 0.0), axis=0, keepdims=True))
        s = jnp.where(hit, -jnp.inf, s)
    return jnp.concatenate(vals, axis=0), jnp.concatenate(picks, axis=0)


def _pair_candidates(v1, x1, v2, x2):
    K = PEER_TOPK
    cands, ids = [], []
    a = 0
    while K // (a + 1) > 1:
        n = K // (a + 1)
        npad = -(-n // SUBLANES) * SUBLANES
        c = v1[a:a + 1, :] + v2[0:npad, :]
        if n < npad:
            c = jnp.where(lax.broadcasted_iota(jnp.int32, c.shape, 0) < n, c, -jnp.inf)
        cands.append(c)
        ids.append(x1[a:a + 1, :] * float(N_KEYS) + x2[0:npad, :])
        a += 1
    cands.append(v1[a:K, :] + v2[0:1, :])
    ids.append(x1[a:K, :] * float(N_KEYS) + x2[0:1, :])
    return jnp.concatenate(cands, axis=0), jnp.concatenate(ids, axis=0)


def _peer_kernel(sct_ref, h_ref, u_ref, v_ref, x_ref, g2_ref, fg_ref, o_ref,
                 pk_ref, pt_ref, scr_ref, g_ref, coef_ref, acc_ref, *, final):
    s = pl.program_id(0)
    j = pl.program_id(1)
    K = PEER_TOPK
    slot = s % 2
    eb = u_ref.shape[0]

    @pl.when(jnp.logical_and(s == 0, j == 0))
    def _():
        pk_ref[...] = jnp.zeros_like(pk_ref)
        scr_ref[...] = jnp.zeros_like(scr_ref)
        g_ref[...] = jnp.zeros_like(g_ref)
        coef_ref[...] = jnp.zeros_like(coef_ref)
        acc_ref[...] = jnp.zeros_like(acc_ref)

    @pl.when(j == 0)
    def _():
        for k in range(pk_ref.shape[0]):
            pt_ref[k] = pk_ref[k].T

    v1, x1 = _top_rows(sct_ref[0], K)
    v2, x2 = _top_rows(sct_ref[1], K)
    cand, ids = _pair_candidates(v1, x1, v2, x2)
    best, expert = _top_rows(cand, K, payload=ids)
    e = jnp.exp(best - best[0:1, :])
    gate = e / jnp.sum(e, axis=0, keepdims=True)
    i1 = jnp.floor(expert * (1.0 / N_KEYS))
    r0 = pl.multiple_of(j * K, K)
    pk_ref[0, pl.ds(r0, K), :] = i1
    pk_ref[1, pl.ds(r0, K), :] = expert - i1 * float(N_KEYS)
    pk_ref[2, pl.ds(r0, K), :] = 0.5 * gate

    npair = pt_ref.shape[-1]
    sub = lax.broadcasted_iota(jnp.int32, (N_KEYS, npair), 0).astype(F32)
    t0 = pl.multiple_of(j * GATE_GROUP, GATE_GROUP)
    for tl in range(GATE_GROUP):
        r = pl.ds(t0 + tl, 1)
        lhs = jnp.where(sub == pt_ref[0, r, :], pt_ref[2, r, :], 0.0).astype(BF16)
        rhs = jnp.where(sub == pt_ref[1, r, :], 1.0, 0.0).astype(BF16)
        scr_ref[pl.ds(tl, N_KEYS, stride=GATE_PITCH), :] = _nt_dot(lhs, rhs)
    for a in range(N_KEYS):
        g_ref[1 - slot, a, pl.ds(t0, GATE_GROUP), :] = (
            scr_ref[a * GATE_PITCH:a * GATE_PITCH + GATE_GROUP, :].astype(g_ref.dtype))

    h = h_ref[...]
    d = h.shape[-1]
    acc_ref[...] += jnp.dot(coef_ref[...], v_ref[:, :d], preferred_element_type=F32)

    a0 = j * (eb // N_KEYS)
    pieces = []
    for q in range(eb // DENSE_SUB):
        act = _nt_dot(h, u_ref[q * DENSE_SUB:(q + 1) * DENSE_SUB, :d])
        w = (act * (1.0 + lax.erf(act * (2.0 ** -0.5)))).astype(BF16)
        for a in range(DENSE_SUB // N_KEYS):
            ga = g_ref[slot, a0 + q * (DENSE_SUB // N_KEYS) + a]
            pieces.append(w[:, a * N_KEYS:(a + 1) * N_KEYS] * ga)
    coef_ref[...] = jnp.concatenate(pieces, axis=1)

    @pl.when(j == 0)
    def _():
        x2 = x_ref[...] + g2_ref[0] * acc_ref[...]
        if final:
            ms = jnp.mean(x2 * x2, axis=-1, keepdims=True)
            x2 = x2 * lax.rsqrt(ms + EPS) * fg_ref[...]
        o_ref[...] = x2
        acc_ref[...] = jnp.zeros_like(acc_ref)


def _peer_call(sct, h2, u, v, x1, g2, fg, final):
    bsz, s, d = x1.shape
    t = bsz * s
    ne = u.shape[0]
    nlist, nk, _ = sct.shape
    tb = min(TB_PEER, s)
    eb = ne // PEER_HEADS
    npair = PEER_HEADS * PEER_TOPK
    assert tb == PEER_HEADS * GATE_GROUP and nlist == 2 * PEER_HEADS and eb % MXU_WIDTH == 0
    nt = t // tb
    per_b = s // tb
    nchunk = PEER_HEADS
    dense_tile = lambda i: jnp.clip(i - 2, 0, nt - 1)
    row = lambda i, j: (dense_tile(i), 0)
    out_tile = lambda i, j: jnp.clip(i - 2 - (j == 0).astype(jnp.int32), 0, nt - 1)
    out_row = lambda i, j: (out_tile(i, j), 0)
    dp = d + LANES if (d // LANES) % 2 == 0 else d
    u = jnp.pad(u, ((0, 0), (0, dp - d)))
    v = jnp.pad(v, ((0, 0), (0, dp - d)))
    out = pl.pallas_call(
        functools.partial(_peer_kernel, final=final),
        grid=(nt + 3, nchunk),
        in_specs=[pl.BlockSpec((2, nk, tb), lambda i, j: (j, 0, jnp.minimum(i, nt - 1))),
                  pl.BlockSpec((tb, d), row),
                  pl.BlockSpec((eb, dp), lambda i, j: (j, 0)),
                  pl.BlockSpec((eb, dp), lambda i, j: ((j + nchunk - 1) % nchunk, 0)),
                  pl.BlockSpec((tb, d), out_row),
                  pl.BlockSpec((1, 1, d), lambda i, j: (out_tile(i, j) // per_b, 0, 0)),
                  pl.BlockSpec((1, d), lambda i, j: (0, 0))],
        out_specs=pl.BlockSpec((tb, d), out_row),
        out_shape=jax.ShapeDtypeStruct((t, d), F32),
        scratch_shapes=[pltpu.VMEM((3, npair, tb), F32),
                        pltpu.VMEM((3, tb, npair), F32),
                        pltpu.VMEM((N_KEYS * GATE_PITCH, N_KEYS), F32),
                        pltpu.VMEM((2, N_KEYS, tb, N_KEYS), BF16),
                        pltpu.VMEM((tb, eb), BF16),
                        pltpu.VMEM((tb, d), F32)],
        compiler_params=_cparams(("arbitrary", "arbitrary")),
        name="peer_fused",
    )(sct, h2.reshape(t, d), u, v, x1.reshape(t, d), g2, fg)
    return out.reshape(bsz, s, d)


def kernel(x, c, ada_w, ada_b, mix_norm_g, ffn_norm_g, w_in, conv_w, conv_b, dt_bias, a_log, d_skip,
           ssd_norm_g, pool_w, pool_scale, w_out, peer_wq, peer_keys, peer_u, peer_v, final_norm_g):
    bsz, s, d = x.shape
    depth = ada_w.shape[0]
    ssd_w = ssd_norm_g.shape[-1]
    heads = dt_bias.shape[-1]
    xbc_w = conv_w.shape[-1]
    pool_wd = pool_scale.shape[-1]
    sb_w = (w_in.shape[-1] - ssd_w - xbc_w - heads - pool_wd) // 3
    o_z, o_xbc, o_dt, o_xp = 0, ssd_w, ssd_w + xbc_w, ssd_w + xbc_w + heads
    o_q = o_xp + pool_wd
    widths = (ssd_w, xbc_w, pool_wd, sb_w, sb_w, sb_w, LANES)

    mods = _mod_call(c, ada_w, ada_b)
    for l in range(depth):
        mod = [mods[l, :, k * d:(k + 1) * d].reshape(bsz, 1, d) for k in range(6)]
        sh1, sc1, g1, sh2, sc2, g2 = mod
        wl = w_in[l]
        w_cat = jnp.concatenate(
            [wl[:, o_z:o_xbc], wl[:, o_xbc:o_dt], wl[:, o_xp:o_q], wl[:, o_q:],
             jnp.pad(wl[:, o_dt:o_xp], ((0, 0), (0, LANES - heads)))], axis=1).astype(BF16)
        z, xbc, xp, q, k, v, dt = _in_call(x, sh1, sc1, mix_norm_g[l].reshape(1, d), w_cat, widths)
        y_ssd = _ssd_call(xbc, dt, z, conv_w[l], conv_b[l], dt_bias[l], a_log[l], d_skip[l],
                          ssd_norm_g[l])
        y_pool = _pool_call(xp, pool_w[l], pool_scale[l])
        y_sb = _sb_call(q, k, v)
        keys = peer_keys[l].reshape(-1, N_KEYS, PEER_HALF).astype(BF16)
        x1, h2, sct = _out_call(y_ssd, y_pool, y_sb, x, g1, sh2, sc2, ffn_norm_g[l].reshape(1, d),
                                w_out[l].astype(BF16), peer_wq[l].astype(BF16), keys)
        x = _peer_call(sct, h2, peer_u[l].astype(BF16), peer_v[l].astype(BF16), x1, g2,
                       final_norm_g.reshape(1, d), final=(l == depth - 1))
    return x
```

```python
import functools

import jax
import jax.numpy as jnp
from jax import lax
from jax.experimental import pallas as pl
from jax.experimental.pallas import tpu as pltpu

F32 = jnp.float32
BF16 = jnp.bfloat16
EPS = 1e-6
HIGHEST = lax.Precision.HIGHEST

SSD_HEAD_DIM = 64
SSD_GROUPS = 2
SSD_STATE = 64
CONV_WIDTH = 4
POOL_WINDOWS = (2, 4, 8, 16)
SB_HEAD_DIM = 64
PEER_HEADS = 8
PEER_TOPK = 16
N_KEYS = 128
PEER_HALF = 128

LANES = 128
SUBLANES = 8
MXU_WIDTH = 256
VMEM_LIMIT = 56 * 1024 * 1024

TM_IN = 512
SSD_CHUNK = 256
HALO = 16
TM_POOL = 512
SB_BLOCK = 128
SB_SKIP = 40.0
GATE_GROUP = 32
GATE_PITCH = 40
TB_PEER = 256
DENSE_SUB = 256


def _cparams(sem):
    return pltpu.CompilerParams(dimension_semantics=sem, vmem_limit_bytes=VMEM_LIMIT)


def _nt_dot(a, b):
    return lax.dot_general(a, b, (((1,), (1,)), ((), ())), preferred_element_type=F32)


def _softplus(x):
    return jnp.maximum(x, 0.0) + jnp.log(1.0 + jnp.exp(-jnp.abs(x)))


def _silu(x):
    return x * jax.nn.sigmoid(x)


def _mod_kernel(c_ref, w_ref, b_ref, o_ref):
    cond = _silu(c_ref[...])
    o_ref[0] = jnp.dot(cond, w_ref[0], preferred_element_type=F32, precision=HIGHEST) + b_ref[0]


def _mod_call(c, ada_w, ada_b):
    nl, d, n6 = ada_w.shape
    bsz = c.shape[0]
    tn = 1536
    return pl.pallas_call(
        _mod_kernel,
        grid=(nl, n6 // tn),
        in_specs=[pl.BlockSpec((bsz, d), lambda l, j: (0, 0)),
                  pl.BlockSpec((1, d, tn), lambda l, j: (l, 0, j)),
                  pl.BlockSpec((1, 1, tn), lambda l, j: (l, 0, j))],
        out_specs=pl.BlockSpec((1, bsz, tn), lambda l, j: (l, 0, j)),
        out_shape=jax.ShapeDtypeStruct((nl, bsz, n6), F32),
        compiler_params=_cparams(("arbitrary", "arbitrary")),
        name="adaln_mod",
    )(c, ada_w, ada_b.reshape(nl, 1, n6))


def _norm_mod(x, g, sh, sc):
    ms = jnp.mean(x * x, axis=-1, keepdims=True)
    y = x * lax.rsqrt(ms + EPS) * g
    return y * (1.0 + sc) + sh


def _in_kernel(x_ref, sh_ref, sc_ref, g_ref, w_ref, z_ref, xbc_ref, xp_ref, q_ref, k_ref, v_ref,
               dt_ref, *, cuts):
    h = _norm_mod(x_ref[0], g_ref[...], sh_ref[0], sc_ref[0]).astype(BF16)
    outs = (z_ref, xbc_ref, xp_ref, q_ref, k_ref, v_ref, dt_ref)
    for o_ref, (a, b) in zip(outs, cuts):
        o_ref[0] = jnp.dot(h, w_ref[:, a:b], preferred_element_type=F32).astype(o_ref.dtype)


def _in_call(x, sh, sc, g, w_cat, widths):
    bsz, s, d = x.shape
    tm = min(TM_IN, s)
    cuts, a = [], 0
    for w in widths:
        cuts.append((a, a + w))
        a += w
    dtypes = (BF16, BF16, BF16, BF16, BF16, BF16, F32)
    tok = lambda w: pl.BlockSpec((1, tm, w), lambda b, i: (b, i, 0))
    vec = pl.BlockSpec((1, 1, d), lambda b, i: (b, 0, 0))
    return pl.pallas_call(
        functools.partial(_in_kernel, cuts=tuple(cuts)),
        grid=(bsz, s // tm),
        in_specs=[tok(d), vec, vec,
                  pl.BlockSpec((1, d), lambda b, i: (0, 0)),
                  pl.BlockSpec(w_cat.shape, lambda b, i: (0, 0))],
        out_specs=[tok(w) for w in widths],
        out_shape=[jax.ShapeDtypeStruct((bsz, s, w), dt) for w, dt in zip(widths, dtypes)],
        compiler_params=_cparams(("arbitrary", "arbitrary")),
        name="in_proj",
    )(x, sh, sc, g, w_cat)


def _ssd_kernel(xbc_ref, halo_ref, dt_ref, z_ref, cw_ref, cb_ref, dtb_ref, alog_ref, expand_ref,
                dsk_ref, ng_ref, y_ref, state_ref, *, chunk, width, gn):
    i = pl.program_id(1)
    L = chunk
    heads = width // SSD_HEAD_DIM
    hpg = heads // SSD_GROUPS
    half = width // SSD_GROUPS

    @pl.when(i == 0)
    def _():
        state_ref[...] = jnp.zeros_like(state_ref)

    hal = jnp.where(i > 0, halo_ref[0].astype(F32), 0.0)
    full = jnp.concatenate([hal, xbc_ref[0].astype(F32)], axis=0)
    cw = cw_ref[...]
    acc = full * cw[CONV_WIDTH - 1:CONV_WIDTH]
    for j in range(1, CONV_WIDTH):
        acc = acc + pltpu.roll(full, j, 0) * cw[CONV_WIDTH - 1 - j:CONV_WIDTH - j]
    xbc = _silu(acc[HALO:] + cb_ref[...])
    xs = xbc[:, :width]
    bm = xbc[:, width:width + gn]
    cm = xbc[:, width + gn:width + 2 * gn]

    dt = _softplus(dt_ref[0] + dtb_ref[...])
    a = dt * (-jnp.exp(alog_ref[...]))
    row_i = lax.broadcasted_iota(jnp.int32, (L, L), 0)
    col_i = lax.broadcasted_iota(jnp.int32, (L, L), 1)
    tril = row_i >= col_i
    a_cs = jnp.dot(tril.astype(F32), a, preferred_element_type=F32, precision=HIGHEST)
    a_cs_t = a_cs.T
    expand = expand_ref[...]
    dt_x = jnp.dot(dt, expand, preferred_element_type=F32, precision=HIGHEST)
    acs_x = jnp.dot(a_cs, expand, preferred_element_type=F32, precision=HIGHEST)
    alast_x = acs_x[L - 1:L, :]
    xdt = xs * dt_x

    bm_b = bm.astype(BF16)
    cm_b = cm.astype(BF16)
    lane_g = lax.broadcasted_iota(jnp.int32, (L, gn), 1) // SSD_STATE
    lane_h = lax.broadcasted_iota(jnp.int32, (L, half), 1) // SSD_HEAD_DIM

    state = state_ref[...]
    y_off = jnp.dot(cm_b, state.astype(BF16), preferred_element_type=F32) * jnp.exp(acs_x)

    y_halves = []
    for g in range(SSD_GROUPS):
        cb = _nt_dot(jnp.where(lane_g == g, cm_b, jnp.zeros_like(cm_b)), bm_b)
        xdt_g = xdt[:, g * half:(g + 1) * half]
        yh = jnp.zeros((L, half), F32)
        for hh in range(hpg):
            h = g * hpg + hh
            seg = a_cs[:, h:h + 1] - a_cs_t[h:h + 1, :]
            dec = jnp.exp(jnp.where(tril, seg, -1e30))
            m = (cb * dec).astype(BF16)
            rhs = jnp.where(lane_h == hh, xdt_g, 0.0).astype(BF16)
            yh = yh + jnp.dot(m, rhs, preferred_element_type=F32)
        y_halves.append(yh)

    ds_x = jnp.exp(alast_x - acs_x)
    upd = jnp.dot(bm.T.astype(BF16), (xdt * ds_x).astype(BF16), preferred_element_type=F32)
    srow = lax.broadcasted_iota(jnp.int32, (gn, width), 0) // SSD_STATE
    scol = lax.broadcasted_iota(jnp.int32, (gn, width), 1) // half
    state_ref[...] = jnp.exp(alast_x) * state + jnp.where(srow == scol, upd, 0.0)

    zg = _silu(z_ref[0].astype(F32))
    for g in range(SSD_GROUPS):
        sl = slice(g * half, (g + 1) * half)
        yg = (y_halves[g] + y_off[:, sl] + dsk_ref[:, sl] * xs[:, sl]) * zg[:, sl]
        ms = jnp.mean(yg * yg, axis=-1, keepdims=True)
        y_ref[0, :, sl] = (yg * lax.rsqrt(ms + EPS) * ng_ref[:, sl]).astype(y_ref.dtype)


def _ssd_call(xbc, dt, z, conv_w, conv_b, dt_bias, a_log, d_skip, ssd_norm_g):
    bsz, s, c = xbc.shape
    width = z.shape[-1]
    gn = SSD_GROUPS * SSD_STATE
    heads = width // SSD_HEAD_DIM
    L = min(SSD_CHUNK, s)
    pad = lambda v: jnp.zeros((1, LANES), F32).at[0, :heads].set(v)
    expand = (jnp.arange(LANES)[:, None] == (jnp.arange(width)[None, :] // SSD_HEAD_DIM)).astype(F32)
    dsk = jnp.repeat(d_skip, SSD_HEAD_DIM).reshape(1, width)
    hb = L // HALO
    const = lambda shp: pl.BlockSpec(shp, lambda b, i: (0, 0))
    return pl.pallas_call(
        functools.partial(_ssd_kernel, chunk=L, width=width, gn=gn),
        grid=(bsz, s // L),
        in_specs=[pl.BlockSpec((1, L, c), lambda b, i: (b, i, 0)),
                  pl.BlockSpec((1, HALO, c), lambda b, i: (b, jnp.maximum(i * hb - 1, 0), 0)),
                  pl.BlockSpec((1, L, LANES), lambda b, i: (b, i, 0)),
                  pl.BlockSpec((1, L, width), lambda b, i: (b, i, 0)),
                  const((CONV_WIDTH, c)), const((1, c)), const((1, LANES)), const((1, LANES)),
                  const((LANES, width)), const((1, width)), const((1, width))],
        out_specs=pl.BlockSpec((1, L, width), lambda b, i: (b, i, 0)),
        out_shape=jax.ShapeDtypeStruct((bsz, s, width), BF16),
        scratch_shapes=[pltpu.VMEM((gn, width), F32)],
        compiler_params=_cparams(("arbitrary", "arbitrary")),
        name="ssd_mixer",
    )(xbc, xbc, dt, z, conv_w, conv_b.reshape(1, c), pad(dt_bias), pad(a_log), expand, dsk,
      ssd_norm_g.reshape(1, width))


def _pool_kernel(xp_ref, halo_ref, w_ref, sc_ref, y_ref, *, tm, gdim):
    i = pl.program_id(1)
    hal = jnp.where(i > 0, halo_ref[0].astype(F32), 0.0)
    x = xp_ref[0].astype(F32)
    full = jnp.concatenate([hal, x], axis=0)
    lane_g = lax.broadcasted_iota(jnp.int32, x.shape, 1) // gdim
    tpos = i * tm + lax.broadcasted_iota(jnp.int32, x.shape, 0)
    win_sum = jnp.zeros_like(x)
    cnt = jnp.ones_like(x)
    s = full
    span = 1
    for gi, w in enumerate(POOL_WINDOWS):
        while span < w:
            s = s + pltpu.roll(s, span, 0)
            span *= 2
        win_sum = jnp.where(lane_g == gi, s[HALO:], win_sum)
        cnt = jnp.where(lane_g == gi, jnp.minimum(tpos + 1, w).astype(F32), cnt)
    pooled = (win_sum / cnt - x).astype(BF16)
    y = jnp.dot(pooled, w_ref[...], preferred_element_type=F32) * sc_ref[...]
    y_ref[0] = y.astype(y_ref.dtype)


def _pool_call(xp, pool_w, pool_scale):
    bsz, s, w = xp.shape
    ng, gdim, _ = pool_w.shape
    assert POOL_WINDOWS[-1] <= HALO and all(b == 2 * a for a, b in zip(POOL_WINDOWS, POOL_WINDOWS[1:]))
    tm = min(TM_POOL, s)
    wbd = jnp.zeros((w, w), F32)
    for g in range(ng):
        wbd = wbd.at[g * gdim:(g + 1) * gdim, g * gdim:(g + 1) * gdim].set(pool_w[g])
    hb = tm // HALO
    return pl.pallas_call(
        functools.partial(_pool_kernel, tm=tm, gdim=gdim),
        grid=(bsz, s // tm),
        in_specs=[pl.BlockSpec((1, tm, w), lambda b, i: (b, i, 0)),
                  pl.BlockSpec((1, HALO, w), lambda b, i: (b, jnp.maximum(i * hb - 1, 0), 0)),
                  pl.BlockSpec((w, w), lambda b, i: (0, 0)),
                  pl.BlockSpec((1, w), lambda b, i: (0, 0))],
        out_specs=pl.BlockSpec((1, tm, w), lambda b, i: (b, i, 0)),
        out_shape=jax.ShapeDtypeStruct((bsz, s, w), BF16),
        compiler_params=_cparams(("arbitrary", "arbitrary")),
        name="pool_mixer",
    )(xp, xp, wbd.astype(BF16), pool_scale.reshape(1, w))


def _sb_kernel(q_ref, k_ref, v_ref, y_ref, qm_ref, acc_ref, carry_ref, *, blk, heads):
    i = pl.program_id(1)
    T = blk
    w = q_ref.shape[-1]
    lane_h = lax.broadcasted_iota(jnp.int32, (T, w), 1) // SB_HEAD_DIM
    q = q_ref[0] * (SB_HEAD_DIM ** -0.5)
    for h in range(heads):
        qm_ref[h * T:(h + 1) * T, :] = jnp.where(lane_h == h, q, jnp.zeros_like(q))
    row_i = lax.broadcasted_iota(jnp.int32, (T, T), 0)
    col_i = lax.broadcasted_iota(jnp.int32, (T, T), 1)
    suffix = (row_i > col_i).astype(BF16)
    qrow = lax.broadcasted_iota(jnp.int32, (heads * T, T), 0) % T
    causal = lax.broadcasted_iota(jnp.int32, (heads * T, T), 1) < qrow

    def scores(j, mask):
        ks = k_ref[0, pl.ds(pl.multiple_of(j * T, T), T), :]
        z = _nt_dot(qm_ref[...], ks)
        lk = -_softplus(z)
        if mask is not None:
            lk = jnp.where(mask, lk, 0.0)
        hi = lk.astype(BF16)
        lo = (lk - hi.astype(F32)).astype(BF16)
        later = (jnp.dot(hi, suffix, preferred_element_type=F32)
                 + jnp.dot(lo, suffix, preferred_element_type=F32))
        return z, lk, later

    def weighted(j, z, lk, later, c, mask):
        vs = v_ref[0, pl.ds(pl.multiple_of(j * T, T), T), :]
        wgt = jnp.exp(z + lk + later + c)
        if mask is not None:
            wgt = jnp.where(mask, wgt, 0.0)
        pv = jnp.dot(wgt.astype(BF16), vs, preferred_element_type=F32)
        out = pv[0:T]
        for h in range(1, heads):
            out = jnp.where(lane_h == h, pv[h * T:(h + 1) * T], out)
        return out, c + jnp.sum(lk, axis=1, keepdims=True)

    def alive():
        return (jnp.max(carry_ref[...]) > -SB_SKIP).astype(jnp.int32)

    prev = jnp.maximum(i - 1, 0)
    has_prev = jnp.broadcast_to(i > 0, causal.shape)
    zd, lkd, ltd = scores(i, causal)
    zp, lkp, ltp = scores(prev, has_prev)
    out_d, c = weighted(i, zd, lkd, ltd, jnp.zeros((heads * T, 1), F32), causal)
    out_p, c = weighted(prev, zp, lkp, ltp, c, has_prev)
    acc_ref[...] = out_d + out_p
    carry_ref[...] = c

    def cond(st):
        return jnp.logical_and(st[0] >= 0, st[1] > 0)

    def body(st):
        j = st[0]
        z, lk, later = scores(j, None)
        out, c = weighted(j, z, lk, later, carry_ref[...], None)
        acc_ref[...] += out
        carry_ref[...] = c
        return j - 1, alive()

    lax.while_loop(cond, body, (i - 2, alive()))
    y_ref[0] = acc_ref[...].astype(y_ref.dtype)


def _sb_call(q, k, v):
    bsz, s, w = q.shape
    heads = w // SB_HEAD_DIM
    T = min(SB_BLOCK, s)
    return pl.pallas_call(
        functools.partial(_sb_kernel, blk=T, heads=heads),
        grid=(bsz, s // T),
        in_specs=[pl.BlockSpec((1, T, w), lambda b, i: (b, i, 0)),
                  pl.BlockSpec((1, s, w), lambda b, i: (b, 0, 0)),
                  pl.BlockSpec((1, s, w), lambda b, i: (b, 0, 0))],
        out_specs=pl.BlockSpec((1, T, w), lambda b, i: (b, i, 0)),
        out_shape=jax.ShapeDtypeStruct((bsz, s, w), BF16),
        scratch_shapes=[pltpu.VMEM((heads * T, w), BF16),
                        pltpu.VMEM((T, w), F32),
                        pltpu.VMEM((heads * T, 1), F32)],
        compiler_params=_cparams(("arbitrary", "arbitrary")),
        name="stick_breaking",
    )(q, k, v)


def _out_kernel(ys_ref, yp_ref, yb_ref, x_ref, g1_ref, sh_ref, sc_ref, ng_ref, wo_ref, wq_ref,
                keys_ref, x1_ref, h2_ref, sct_ref, *, cuts):
    mix = jnp.zeros(x_ref.shape[1:], F32)
    for y_ref, (a, b) in zip((ys_ref, yp_ref, yb_ref), cuts):
        mix = mix + jnp.dot(y_ref[0], wo_ref[a:b, :], preferred_element_type=F32)
    x1 = x_ref[0] + g1_ref[0] * mix
    x1_ref[0] = x1
    h2 = _norm_mod(x1, ng_ref[...], sh_ref[0], sc_ref[0]).astype(BF16)
    h2_ref[0] = h2
    qb = jnp.dot(h2, wq_ref[...], preferred_element_type=F32).astype(BF16)
    for lst in range(keys_ref.shape[0]):
        sct_ref[lst] = _nt_dot(keys_ref[lst], qb[:, lst * PEER_HALF:(lst + 1) * PEER_HALF])


def _out_call(ys, yp, yb, x, g1, sh, sc, ng, w_out, wq, keys):
    bsz, s, d = x.shape
    tm = min(TM_IN, s)
    nlist = keys.shape[0]
    cuts, a = [], 0
    for y in (ys, yp, yb):
        cuts.append((a, a + y.shape[-1]))
        a += y.shape[-1]
    tok = lambda w: pl.BlockSpec((1, tm, w), lambda b, i: (b, i, 0))
    vec = pl.BlockSpec((1, 1, d), lambda b, i: (b, 0, 0))
    full2 = lambda arr: pl.BlockSpec(arr.shape, lambda b, i: (0,) * arr.ndim)
    nblk = s // tm
    return pl.pallas_call(
        functools.partial(_out_kernel, cuts=tuple(cuts)),
        grid=(bsz, nblk),
        in_specs=[tok(ys.shape[-1]), tok(yp.shape[-1]), tok(yb.shape[-1]), tok(d), vec, vec, vec,
                  pl.BlockSpec((1, d), lambda b, i: (0, 0)), full2(w_out), full2(wq), full2(keys)],
        out_specs=[tok(d), tok(d),
                   pl.BlockSpec((nlist, N_KEYS, tm), lambda b, i: (0, 0, b * nblk + i))],
        out_shape=[jax.ShapeDtypeStruct((bsz, s, d), F32),
                   jax.ShapeDtypeStruct((bsz, s, d), BF16),
                   jax.ShapeDtypeStruct((nlist, N_KEYS, bsz * s), F32)],
        compiler_params=_cparams(("arbitrary", "arbitrary")),
        name="out_proj_peer_query",
    )(ys, yp, yb, x, g1, sh, sc, ng, w_out, wq, keys)


def _top_rows(s, k, payload=None):
    nrows = s.shape[0]
    rows = lax.broadcasted_iota(jnp.int32, s.shape, 0).astype(F32)
    vals, picks = [], []
    for _ in range(k):
        m = jnp.max(s, axis=0, keepdims=True)
        idx = jnp.min(jnp.where(s == m, rows, float(nrows)), axis=0, keepdims=True)
        hit = rows == idx
        vals.append(m)
        if payload is None:
            picks.append(idx)
        else:
            picks.append(jnp.sum(jnp.where(hit, payload, 0.0), axis=0, keepdims=True))
        s = jnp.where(hit, -jnp.inf, s)
    return jnp.concatenate(vals, axis=0), jnp.concatenate(picks, axis=0)


def _pair_candidates(v1, x1, v2, x2):
    K = PEER_TOPK
    cands, ids = [], []
    a = 0
    while K // (a + 1) > 1:
        n = K // (a + 1)
        npad = -(-n // SUBLANES) * SUBLANES
        c = v1[a:a + 1, :] + v2[0:npad, :]
        if n < npad:
            c = jnp.where(lax.broadcasted_iota(jnp.int32, c.shape, 0) < n, c, -jnp.inf)
        cands.append(c)
        ids.append(x1[a:a + 1, :] * float(N_KEYS) + x2[0:npad, :])
        a += 1
    cands.append(v1[a:K, :] + v2[0:1, :])
    ids.append(x1[a:K, :] * float(N_KEYS) + x2[0:1, :])
    return jnp.concatenate(cands, axis=0), jnp.concatenate(ids, axis=0)


def _peer_kernel(sct_ref, h_ref, u_ref, v_ref, x_ref, g2_ref, fg_ref, o_ref,
                 pk_ref, pt_ref, scr_ref, g_ref, coef_ref, acc_ref, *, final):
    s = pl.program_id(0)
    j = pl.program_id(1)
    K = PEER_TOPK
    slot = s % 2
    eb = u_ref.shape[0]

    @pl.when(jnp.logical_and(s == 0, j == 0))
    def _():
        pk_ref[...] = jnp.zeros_like(pk_ref)
        scr_ref[...] = jnp.zeros_like(scr_ref)
        g_ref[...] = jnp.zeros_like(g_ref)
        coef_ref[...] = jnp.zeros_like(coef_ref)
        acc_ref[...] = jnp.zeros_like(acc_ref)

    @pl.when(j == 0)
    def _():
        for k in range(pk_ref.shape[0]):
            pt_ref[k] = pk_ref[k].T

    v1, x1 = _top_rows(sct_ref[0], K)
    v2, x2 = _top_rows(sct_ref[1], K)
    cand, ids = _pair_candidates(v1, x1, v2, x2)
    best, expert = _top_rows(cand, K, payload=ids)
    e = jnp.exp(best - best[0:1, :])
    gate = e / jnp.sum(e, axis=0, keepdims=True)
    i1 = jnp.floor(expert * (1.0 / N_KEYS))
    r0 = pl.multiple_of(j * K, K)
    pk_ref[0, pl.ds(r0, K), :] = i1
    pk_ref[1, pl.ds(r0, K), :] = expert - i1 * float(N_KEYS)
    pk_ref[2, pl.ds(r0, K), :] = 0.5 * gate

    npair = pt_ref.shape[-1]
    sub = lax.broadcasted_iota(jnp.int32, (N_KEYS, npair), 0).astype(F32)
    t0 = pl.multiple_of(j * GATE_GROUP, GATE_GROUP)
    for tl in range(GATE_GROUP):
        r = pl.ds(t0 + tl, 1)
        lhs = jnp.where(sub == pt_ref[0, r, :], pt_ref[2, r, :], 0.0).astype(BF16)
        rhs = jnp.where(sub == pt_ref[1, r, :], 1.0, 0.0).astype(BF16)
        scr_ref[pl.ds(tl, N_KEYS, stride=GATE_PITCH), :] = _nt_dot(lhs, rhs)
    for a in range(N_KEYS):
        g_ref[1 - slot, a, pl.ds(t0, GATE_GROUP), :] = (
            scr_ref[a * GATE_PITCH:a * GATE_PITCH + GATE_GROUP, :].astype(g_ref.dtype))

    h = h_ref[...]
    d = h.shape[-1]
    acc_ref[...] += jnp.dot(coef_ref[...], v_ref[:, :d], preferred_element_type=F32)

    a0 = j * (eb // N_KEYS)
    pieces = []
    for q in range(eb // DENSE_SUB):
        act = _nt_dot(h, u_ref[q * DENSE_SUB:(q + 1) * DENSE_SUB, :d])
        w = (act * (1.0 + lax.erf(act * (2.0 ** -0.5)))).astype(BF16)
        for a in range(DENSE_SUB // N_KEYS):
            ga = g_ref[slot, a0 + q * (DENSE_SUB // N_KEYS) + a]
            pieces.append(w[:, a * N_KEYS:(a + 1) * N_KEYS] * ga)
    coef_ref[...] = jnp.concatenate(pieces, axis=1)

    @pl.when(j == 0)
    def _():
        x2 = x_ref[...] + g2_ref[0] * acc_ref[...]
        if final:
            ms = jnp.mean(x2 * x2, axis=-1, keepdims=True)
            x2 = x2 * lax.rsqrt(ms + EPS) * fg_ref[...]
        o_ref[...] = x2
        acc_ref[...] = jnp.zeros_like(acc_ref)


def _peer_call(sct, h2, u, v, x1, g2, fg, final):
    bsz, s, d = x1.shape
    t = bsz * s
    ne = u.shape[0]
    nlist, nk, _ = sct.shape
    tb = min(TB_PEER, s)
    eb = ne // PEER_HEADS
    npair = PEER_HEADS * PEER_TOPK
    assert tb == PEER_HEADS * GATE_GROUP and nlist == 2 * PEER_HEADS and eb % MXU_WIDTH == 0
    nt = t // tb
    per_b = s // tb
    nchunk = PEER_HEADS
    dense_tile = lambda i: jnp.clip(i - 2, 0, nt - 1)
    row = lambda i, j: (dense_tile(i), 0)
    out_tile = lambda i, j: jnp.clip(i - 2 - (j == 0).astype(jnp.int32), 0, nt - 1)
    out_row = lambda i, j: (out_tile(i, j), 0)
    dp = d + LANES if (d // LANES) % 2 == 0 else d
    u = jnp.pad(u, ((0, 0), (0, dp - d)))
    v = jnp.pad(v, ((0, 0), (0, dp - d)))
    out = pl.pallas_call(
        functools.partial(_peer_kernel, final=final),
        grid=(nt + 3, nchunk),
        in_specs=[pl.BlockSpec((2, nk, tb), lambda i, j: (j, 0, jnp.minimum(i, nt - 1))),
                  pl.BlockSpec((tb, d), row),
                  pl.BlockSpec((eb, dp), lambda i, j: (j, 0)),
                  pl.BlockSpec((eb, dp), lambda i, j: ((j + nchunk - 1) % nchunk, 0)),
                  pl.BlockSpec((tb, d), out_row),
                  pl.BlockSpec((1, 1, d), lambda i, j: (out_tile(i, j) // per_b, 0, 0)),
                  pl.BlockSpec((1, d), lambda i, j: (0, 0))],
        out_specs=pl.BlockSpec((tb, d), out_row),
        out_shape=jax.ShapeDtypeStruct((t, d), F32),
        scratch_shapes=[pltpu.VMEM((3, npair, tb), F32),
                        pltpu.VMEM((3, tb, npair), F32),
                        pltpu.VMEM((N_KEYS * GATE_PITCH, N_KEYS), F32),
                        pltpu.VMEM((2, N_KEYS, tb, N_KEYS), BF16),
                        pltpu.VMEM((tb, eb), BF16),
                        pltpu.VMEM((tb, d), F32)],
        compiler_params=_cparams(("arbitrary", "arbitrary")),
        name="peer_fused",
    )(sct, h2.reshape(t, d), u, v, x1.reshape(t, d), g2, fg)
    return out.reshape(bsz, s, d)


def kernel(x, c, ada_w, ada_b, mix_norm_g, ffn_norm_g, w_in, conv_w, conv_b, dt_bias, a_log, d_skip,
           ssd_norm_g, pool_w, pool_scale, w_out, peer_wq, peer_keys, peer_u, peer_v, final_norm_g):
    bsz, s, d = x.shape
    depth = ada_w.shape[0]
    ssd_w = ssd_norm_g.shape[-1]
    heads = dt_bias.shape[-1]
    xbc_w = conv_w.shape[-1]
    pool_wd = pool_scale.shape[-1]
    sb_w = (w_in.shape[-1] - ssd_w - xbc_w - heads - pool_wd) // 3
    o_z, o_xbc, o_dt, o_xp = 0, ssd_w, ssd_w + xbc_w, ssd_w + xbc_w + heads
    o_q = o_xp + pool_wd
    widths = (ssd_w, xbc_w, pool_wd, sb_w, sb_w, sb_w, LANES)

    mods = _mod_call(c, ada_w, ada_b)
    for l in range(depth):
        mod = [mods[l, :, k * d:(k + 1) * d].reshape(bsz, 1, d) for k in range(6)]
        sh1, sc1, g1, sh2, sc2, g2 = mod
        wl = w_in[l]
        w_cat = jnp.concatenate(
            [wl[:, o_z:o_xbc], wl[:, o_xbc:o_dt], wl[:, o_xp:o_q], wl[:, o_q:],
             jnp.pad(wl[:, o_dt:o_xp], ((0, 0), (0, LANES - heads)))], axis=1).astype(BF16)
        z, xbc, xp, q, k, v, dt = _in_call(x, sh1, sc1, mix_norm_g[l].reshape(1, d), w_cat, widths)
        y_ssd = _ssd_call(xbc, dt, z, conv_w[l], conv_b[l], dt_bias[l], a_log[l], d_skip[l],
                          ssd_norm_g[l])
        y_pool = _pool_call(xp, pool_w[l], pool_scale[l])
        y_sb = _sb_call(q, k, v)
        keys = peer_keys[l].reshape(-1, N_KEYS, PEER_HALF).astype(BF16)
        x1, h2, sct = _out_call(y_ssd, y_pool, y_sb, x, g1, sh2, sc2, ffn_norm_g[l].reshape(1, d),
                                w_out[l].astype(BF16), peer_wq[l].astype(BF16), keys)
        x = _peer_call(sct, h2, peer_u[l].astype(BF16), peer_v[l].astype(BF16), x1, g2,
                       final_norm_g.reshape(1, d), final=(l == depth - 1))
    return x
```

```python
import functools

import jax
import jax.numpy as jnp
from jax import lax
from jax.experimental import pallas as pl
from jax.experimental.pallas import tpu as pltpu

F32 = jnp.float32
BF16 = jnp.bfloat16
EPS = 1e-6
HIGHEST = lax.Precision.HIGHEST

SSD_HEAD_DIM = 64
SSD_GROUPS = 2
SSD_STATE = 64
CONV_WIDTH = 4
POOL_WINDOWS = (2, 4, 8, 16)
SB_HEAD_DIM = 64
PEER_HEADS = 8
PEER_TOPK = 16
N_KEYS = 128
PEER_HALF = 128

LANES = 128
SUBLANES = 8
MXU_WIDTH = 256
VMEM_LIMIT = 56 * 1024 * 1024

TM_IN = 512
SSD_CHUNK = 256
HALO = 16
TM_POOL = 512
SB_BLOCK = 128
SB_SKIP = 40.0
GATE_GROUP = 32
GATE_PITCH = 36
TB_PEER = 256
DENSE_SUB = 256


def _cparams(sem):
    return pltpu.CompilerParams(dimension_semantics=sem, vmem_limit_bytes=VMEM_LIMIT)


def _nt_dot(a, b):
    return lax.dot_general(a, b, (((1,), (1,)), ((), ())), preferred_element_type=F32)


def _softplus(x):
    return jnp.maximum(x, 0.0) + jnp.log(1.0 + jnp.exp(-jnp.abs(x)))


def _silu(x):
    return x * jax.nn.sigmoid(x)


def _mod_kernel(c_ref, w_ref, b_ref, o_ref):
    cond = _silu(c_ref[...])
    o_ref[0] = jnp.dot(cond, w_ref[0], preferred_element_type=F32, precision=HIGHEST) + b_ref[0]


def _mod_call(c, ada_w, ada_b):
    nl, d, n6 = ada_w.shape
    bsz = c.shape[0]
    tn = 1536
    return pl.pallas_call(
        _mod_kernel,
        grid=(nl, n6 // tn),
        in_specs=[pl.BlockSpec((bsz, d), lambda l, j: (0, 0)),
                  pl.BlockSpec((1, d, tn), lambda l, j: (l, 0, j)),
                  pl.BlockSpec((1, 1, tn), lambda l, j: (l, 0, j))],
        out_specs=pl.BlockSpec((1, bsz, tn), lambda l, j: (l, 0, j)),
        out_shape=jax.ShapeDtypeStruct((nl, bsz, n6), F32),
        compiler_params=_cparams(("arbitrary", "arbitrary")),
        name="adaln_mod",
    )(c, ada_w, ada_b.reshape(nl, 1, n6))


def _norm_mod(x, g, sh, sc):
    ms = jnp.mean(x * x, axis=-1, keepdims=True)
    y = x * lax.rsqrt(ms + EPS) * g
    return y * (1.0 + sc) + sh


def _in_kernel(x_ref, sh_ref, sc_ref, g_ref, w_ref, z_ref, xbc_ref, xp_ref, q_ref, k_ref, v_ref,
               dt_ref, *, cuts):
    h = _norm_mod(x_ref[0], g_ref[...], sh_ref[0], sc_ref[0]).astype(BF16)
    outs = (z_ref, xbc_ref, xp_ref, q_ref, k_ref, v_ref, dt_ref)
    for o_ref, (a, b) in zip(outs, cuts):
        o_ref[0] = jnp.dot(h, w_ref[:, a:b], preferred_element_type=F32).astype(o_ref.dtype)


def _in_call(x, sh, sc, g, w_cat, widths):
    bsz, s, d = x.shape
    tm = min(TM_IN, s)
    cuts, a = [], 0
    for w in widths:
        cuts.append((a, a + w))
        a += w
    dtypes = (BF16, BF16, BF16, BF16, BF16, BF16, F32)
    tok = lambda w: pl.BlockSpec((1, tm, w), lambda b, i: (b, i, 0))
    vec = pl.BlockSpec((1, 1, d), lambda b, i: (b, 0, 0))
    return pl.pallas_call(
        functools.partial(_in_kernel, cuts=tuple(cuts)),
        grid=(bsz, s // tm),
        in_specs=[tok(d), vec, vec,
                  pl.BlockSpec((1, d), lambda b, i: (0, 0)),
                  pl.BlockSpec(w_cat.shape, lambda b, i: (0, 0))],
        out_specs=[tok(w) for w in widths],
        out_shape=[jax.ShapeDtypeStruct((bsz, s, w), dt) for w, dt in zip(widths, dtypes)],
        compiler_params=_cparams(("arbitrary", "arbitrary")),
        name="in_proj",
    )(x, sh, sc, g, w_cat)


def _ssd_kernel(xbc_ref, halo_ref, dt_ref, z_ref, cw_ref, cb_ref, dtb_ref, alog_ref, expand_ref,
                dsk_ref, ng_ref, y_ref, state_ref, *, chunk, width, gn):
    i = pl.program_id(1)
    L = chunk
    heads = width // SSD_HEAD_DIM
    hpg = heads // SSD_GROUPS
    half = width // SSD_GROUPS

    @pl.when(i == 0)
    def _():
        state_ref[...] = jnp.zeros_like(state_ref)

    hal = jnp.where(i > 0, halo_ref[0].astype(F32), 0.0)
    full = jnp.concatenate([hal, xbc_ref[0].astype(F32)], axis=0)
    cw = cw_ref[...]
    acc = full * cw[CONV_WIDTH - 1:CONV_WIDTH]
    for j in range(1, CONV_WIDTH):
        acc = acc + pltpu.roll(full, j, 0) * cw[CONV_WIDTH - 1 - j:CONV_WIDTH - j]
    xbc = _silu(acc[HALO:] + cb_ref[...])
    xs = xbc[:, :width]
    bm = xbc[:, width:width + gn]
    cm = xbc[:, width + gn:width + 2 * gn]

    dt = _softplus(dt_ref[0] + dtb_ref[...])
    a = dt * (-jnp.exp(alog_ref[...]))
    row_i = lax.broadcasted_iota(jnp.int32, (L, L), 0)
    col_i = lax.broadcasted_iota(jnp.int32, (L, L), 1)
    tril = row_i >= col_i
    a_cs = jnp.dot(tril.astype(F32), a, preferred_element_type=F32, precision=HIGHEST)
    a_cs_t = a_cs.T
    expand = expand_ref[...]
    dt_x = jnp.dot(dt, expand, preferred_element_type=F32, precision=HIGHEST)
    acs_x = jnp.dot(a_cs, expand, preferred_element_type=F32, precision=HIGHEST)
    alast_x = acs_x[L - 1:L, :]
    xdt = xs * dt_x

    bm_b = bm.astype(BF16)
    cm_b = cm.astype(BF16)
    lane_g = lax.broadcasted_iota(jnp.int32, (L, gn), 1) // SSD_STATE
    lane_h = lax.broadcasted_iota(jnp.int32, (L, half), 1) // SSD_HEAD_DIM

    state = state_ref[...]
    y_off = jnp.dot(cm_b, state.astype(BF16), preferred_element_type=F32) * jnp.exp(acs_x)

    y_halves = []
    for g in range(SSD_GROUPS):
        cb = _nt_dot(jnp.where(lane_g == g, cm_b, jnp.zeros_like(cm_b)), bm_b)
        xdt_g = xdt[:, g * half:(g + 1) * half]
        yh = jnp.zeros((L, half), F32)
        for hh in range(hpg):
            h = g * hpg + hh
            seg = a_cs[:, h:h + 1] - a_cs_t[h:h + 1, :]
            dec = jnp.exp(jnp.where(tril, seg, -1e30))
            m = (cb * dec).astype(BF16)
            rhs = jnp.where(lane_h == hh, xdt_g, 0.0).astype(BF16)
            yh = yh + jnp.dot(m, rhs, preferred_element_type=F32)
        y_halves.append(yh)

    ds_x = jnp.exp(alast_x - acs_x)
    upd = jnp.dot(bm.T.astype(BF16), (xdt * ds_x).astype(BF16), preferred_element_type=F32)
    srow = lax.broadcasted_iota(jnp.int32, (gn, width), 0) // SSD_STATE
    scol = lax.broadcasted_iota(jnp.int32, (gn, width), 1) // half
    state_ref[...] = jnp.exp(alast_x) * state + jnp.where(srow == scol, upd, 0.0)

    zg = _silu(z_ref[0].astype(F32))
    for g in range(SSD_GROUPS):
        sl = slice(g * half, (g + 1) * half)
        yg = (y_halves[g] + y_off[:, sl] + dsk_ref[:, sl] * xs[:, sl]) * zg[:, sl]
        ms = jnp.mean(yg * yg, axis=-1, keepdims=True)
        y_ref[0, :, sl] = (yg * lax.rsqrt(ms + EPS) * ng_ref[:, sl]).astype(y_ref.dtype)


def _ssd_call(xbc, dt, z, conv_w, conv_b, dt_bias, a_log, d_skip, ssd_norm_g):
    bsz, s, c = xbc.shape
    width = z.shape[-1]
    gn = SSD_GROUPS * SSD_STATE
    heads = width // SSD_HEAD_DIM
    L = min(SSD_CHUNK, s)
    pad = lambda v: jnp.zeros((1, LANES), F32).at[0, :heads].set(v)
    expand = (jnp.arange(LANES)[:, None] == (jnp.arange(width)[None, :] // SSD_HEAD_DIM)).astype(F32)
    dsk = jnp.repeat(d_skip, SSD_HEAD_DIM).reshape(1, width)
    hb = L // HALO
    const = lambda shp: pl.BlockSpec(shp, lambda b, i: (0, 0))
    return pl.pallas_call(
        functools.partial(_ssd_kernel, chunk=L, width=width, gn=gn),
        grid=(bsz, s // L),
        in_specs=[pl.BlockSpec((1, L, c), lambda b, i: (b, i, 0)),
                  pl.BlockSpec((1, HALO, c), lambda b, i: (b, jnp.maximum(i * hb - 1, 0), 0)),
                  pl.BlockSpec((1, L, LANES), lambda b, i: (b, i, 0)),
                  pl.BlockSpec((1, L, width), lambda b, i: (b, i, 0)),
                  const((CONV_WIDTH, c)), const((1, c)), const((1, LANES)), const((1, LANES)),
                  const((LANES, width)), const((1, width)), const((1, width))],
        out_specs=pl.BlockSpec((1, L, width), lambda b, i: (b, i, 0)),
        out_shape=jax.ShapeDtypeStruct((bsz, s, width), BF16),
        scratch_shapes=[pltpu.VMEM((gn, width), F32)],
        compiler_params=_cparams(("arbitrary", "arbitrary")),
        name="ssd_mixer",
    )(xbc, xbc, dt, z, conv_w, conv_b.reshape(1, c), pad(dt_bias), pad(a_log), expand, dsk,
      ssd_norm_g.reshape(1, width))


def _pool_kernel(xp_ref, halo_ref, w_ref, sc_ref, y_ref, *, tm, gdim):
    i = pl.program_id(1)
    hal = jnp.where(i > 0, halo_ref[0].astype(F32), 0.0)
    x = xp_ref[0].astype(F32)
    full = jnp.concatenate([hal, x], axis=0)
    lane_g = lax.broadcasted_iota(jnp.int32, x.shape, 1) // gdim
    tpos = i * tm + lax.broadcasted_iota(jnp.int32, x.shape, 0)
    win_sum = jnp.zeros_like(x)
    cnt = jnp.ones_like(x)
    s = full
    span = 1
    for gi, w in enumerate(POOL_WINDOWS):
        while span < w:
            s = s + pltpu.roll(s, span, 0)
            span *= 2
        win_sum = jnp.where(lane_g == gi, s[HALO:], win_sum)
        cnt = jnp.where(lane_g == gi, jnp.minimum(tpos + 1, w).astype(F32), cnt)
    pooled = (win_sum / cnt - x).astype(BF16)
    y = jnp.dot(pooled, w_ref[...], preferred_element_type=F32) * sc_ref[...]
    y_ref[0] = y.astype(y_ref.dtype)


def _pool_call(xp, pool_w, pool_scale):
    bsz, s, w = xp.shape
    ng, gdim, _ = pool_w.shape
    assert POOL_WINDOWS[-1] <= HALO and all(b == 2 * a for a, b in zip(POOL_WINDOWS, POOL_WINDOWS[1:]))
    tm = min(TM_POOL, s)
    wbd = jnp.zeros((w, w), F32)
    for g in range(ng):
        wbd = wbd.at[g * gdim:(g + 1) * gdim, g * gdim:(g + 1) * gdim].set(pool_w[g])
    hb = tm // HALO
    return pl.pallas_call(
        functools.partial(_pool_kernel, tm=tm, gdim=gdim),
        grid=(bsz, s // tm),
        in_specs=[pl.BlockSpec((1, tm, w), lambda b, i: (b, i, 0)),
                  pl.BlockSpec((1, HALO, w), lambda b, i: (b, jnp.maximum(i * hb - 1, 0), 0)),
                  pl.BlockSpec((w, w), lambda b, i: (0, 0)),
                  pl.BlockSpec((1, w), lambda b, i: (0, 0))],
        out_specs=pl.BlockSpec((1, tm, w), lambda b, i: (b, i, 0)),
        out_shape=jax.ShapeDtypeStruct((bsz, s, w), BF16),
        compiler_params=_cparams(("arbitrary", "arbitrary")),
        name="pool_mixer",
    )(xp, xp, wbd.astype(BF16), pool_scale.reshape(1, w))


def _sb_kernel(q_ref, k_ref, v_ref, y_ref, qm_ref, acc_ref, carry_ref, *, blk, heads):
    i = pl.program_id(1)
    T = blk
    w = q_ref.shape[-1]
    lane_h = lax.broadcasted_iota(jnp.int32, (T, w), 1) // SB_HEAD_DIM
    q = q_ref[0] * (SB_HEAD_DIM ** -0.5)
    for h in range(heads):
        qm_ref[h * T:(h + 1) * T, :] = jnp.where(lane_h == h, q, jnp.zeros_like(q))
    row_i = lax.broadcasted_iota(jnp.int32, (T, T), 0)
    col_i = lax.broadcasted_iota(jnp.int32, (T, T), 1)
    suffix = (row_i > col_i).astype(BF16)
    qrow = lax.broadcasted_iota(jnp.int32, (heads * T, T), 0) % T
    causal = lax.broadcasted_iota(jnp.int32, (heads * T, T), 1) < qrow

    def scores(j, mask):
        ks = k_ref[0, pl.ds(pl.multiple_of(j * T, T), T), :]
        z = _nt_dot(qm_ref[...], ks)
        lk = -_softplus(z)
        if mask is not None:
            lk = jnp.where(mask, lk, 0.0)
        hi = lk.astype(BF16)
        lo = (lk - hi.astype(F32)).astype(BF16)
        later = (jnp.dot(hi, suffix, preferred_element_type=F32)
                 + jnp.dot(lo, suffix, preferred_element_type=F32))
        return z, lk, later

    def weighted(j, z, lk, later, c, mask):
        vs = v_ref[0, pl.ds(pl.multiple_of(j * T, T), T), :]
        wgt = jnp.exp(z + lk + later + c)
        if mask is not None:
            wgt = jnp.where(mask, wgt, 0.0)
        pv = jnp.dot(wgt.astype(BF16), vs, preferred_element_type=F32)
        out = pv[0:T]
        for h in range(1, heads):
            out = jnp.where(lane_h == h, pv[h * T:(h + 1) * T], out)
        return out, c + jnp.sum(lk, axis=1, keepdims=True)

    def alive():
        return (jnp.max(carry_ref[...]) > -SB_SKIP).astype(jnp.int32)

    prev = jnp.maximum(i - 1, 0)
    has_prev = jnp.broadcast_to(i > 0, causal.shape)
    zd, lkd, ltd = scores(i, causal)
    zp, lkp, ltp = scores(prev, has_prev)
    out_d, c = weighted(i, zd, lkd, ltd, jnp.zeros((heads * T, 1), F32), causal)
    out_p, c = weighted(prev, zp, lkp, ltp, c, has_prev)
    acc_ref[...] = out_d + out_p
    carry_ref[...] = c

    def cond(st):
        return jnp.logical_and(st[0] >= 0, st[1] > 0)

    def body(st):
        j = st[0]
        z, lk, later = scores(j, None)
        out, c = weighted(j, z, lk, later, carry_ref[...], None)
        acc_ref[...] += out
        carry_ref[...] = c
        return j - 1, alive()

    lax.while_loop(cond, body, (i - 2, alive()))
    y_ref[0] = acc_ref[...].astype(y_ref.dtype)


def _sb_call(q, k, v):
    bsz, s, w = q.shape
    heads = w // SB_HEAD_DIM
    T = min(SB_BLOCK, s)
    return pl.pallas_call(
        functools.partial(_sb_kernel, blk=T, heads=heads),
        grid=(bsz, s // T),
        in_specs=[pl.BlockSpec((1, T, w), lambda b, i: (b, i, 0)),
                  pl.BlockSpec((1, s, w), lambda b, i: (b, 0, 0)),
                  pl.BlockSpec((1, s, w), lambda b, i: (b, 0, 0))],
        out_specs=pl.BlockSpec((1, T, w), lambda b, i: (b, i, 0)),
        out_shape=jax.ShapeDtypeStruct((bsz, s, w), BF16),
        scratch_shapes=[pltpu.VMEM((heads * T, w), BF16),
                        pltpu.VMEM((T, w), F32),
                        pltpu.VMEM((heads * T, 1), F32)],
        compiler_params=_cparams(("arbitrary", "arbitrary")),
        name="stick_breaking",
    )(q, k, v)


def _out_kernel(ys_ref, yp_ref, yb_ref, x_ref, g1_ref, sh_ref, sc_ref, ng_ref, wo_ref, wq_ref,
                keys_ref, x1_ref, h2_ref, sct_ref, *, cuts):
    mix = jnp.zeros(x_ref.shape[1:], F32)
    for y_ref, (a, b) in zip((ys_ref, yp_ref, yb_ref), cuts):
        mix = mix + jnp.dot(y_ref[0], wo_ref[a:b, :], preferred_element_type=F32)
    x1 = x_ref[0] + g1_ref[0] * mix
    x1_ref[0] = x1
    h2 = _norm_mod(x1, ng_ref[...], sh_ref[0], sc_ref[0]).astype(BF16)
    h2_ref[0] = h2
    qb = jnp.dot(h2, wq_ref[...], preferred_element_type=F32).astype(BF16)
    for lst in range(keys_ref.shape[0]):
        sct_ref[lst] = _nt_dot(keys_ref[lst], qb[:, lst * PEER_HALF:(lst + 1) * PEER_HALF])


def _out_call(ys, yp, yb, x, g1, sh, sc, ng, w_out, wq, keys):
    bsz, s, d = x.shape
    tm = min(TM_IN, s)
    nlist = keys.shape[0]
    cuts, a = [], 0
    for y in (ys, yp, yb):
        cuts.append((a, a + y.shape[-1]))
        a += y.shape[-1]
    tok = lambda w: pl.BlockSpec((1, tm, w), lambda b, i: (b, i, 0))
    vec = pl.BlockSpec((1, 1, d), lambda b, i: (b, 0, 0))
    full2 = lambda arr: pl.BlockSpec(arr.shape, lambda b, i: (0,) * arr.ndim)
    nblk = s // tm
    return pl.pallas_call(
        functools.partial(_out_kernel, cuts=tuple(cuts)),
        grid=(bsz, nblk),
        in_specs=[tok(ys.shape[-1]), tok(yp.shape[-1]), tok(yb.shape[-1]), tok(d), vec, vec, vec,
                  pl.BlockSpec((1, d), lambda b, i: (0, 0)), full2(w_out), full2(wq), full2(keys)],
        out_specs=[tok(d), tok(d),
                   pl.BlockSpec((nlist, N_KEYS, tm), lambda b, i: (0, 0, b * nblk + i))],
        out_shape=[jax.ShapeDtypeStruct((bsz, s, d), F32),
                   jax.ShapeDtypeStruct((bsz, s, d), BF16),
                   jax.ShapeDtypeStruct((nlist, N_KEYS, bsz * s), F32)],
        compiler_params=_cparams(("arbitrary", "arbitrary")),
        name="out_proj_peer_query",
    )(ys, yp, yb, x, g1, sh, sc, ng, w_out, wq, keys)


def _top_rows(s, k, payload=None):
    nrows = s.shape[0]
    rows = lax.broadcasted_iota(jnp.int32, s.shape, 0).astype(F32)
    vals, picks = [], []
    for _ in range(k):
        m = jnp.max(s, axis=0, keepdims=True)
        idx = jnp.min(jnp.where(s == m, rows, float(nrows)), axis=0, keepdims=True)
        hit = rows == idx
        vals.append(m)
        if payload is None:
            picks.append(idx)
        else:
            picks.append(jnp.sum(jnp.where(hit, payload, 0.0), axis=0, keepdims=True))
        s = jnp.where(hit, -jnp.inf, s)
    return jnp.concatenate(vals, axis=0), jnp.concatenate(picks, axis=0)


def _pair_candidates(v1, x1, v2, x2):
    K = PEER_TOPK
    cands, ids = [], []
    a = 0
    while K // (a + 1) > 1:
        n = K // (a + 1)
        npad = -(-n // SUBLANES) * SUBLANES
        c = v1[a:a + 1, :] + v2[0:npad, :]
        if n < npad:
            c = jnp.where(lax.broadcasted_iota(jnp.int32, c.shape, 0) < n, c, -jnp.inf)
        cands.append(c)
        ids.append(x1[a:a + 1, :] * float(N_KEYS) + x2[0:npad, :])
        a += 1
    cands.append(v1[a:K, :] + v2[0:1, :])
    ids.append(x1[a:K, :] * float(N_KEYS) + x2[0:1, :])
    return jnp.concatenate(cands, axis=0), jnp.concatenate(ids, axis=0)


def _peer_kernel(sct_ref, h_ref, u_ref, v_ref, x_ref, g2_ref, fg_ref, o_ref,
                 pk_ref, pt_ref, scr_ref, g_ref, coef_ref, acc_ref, *, final):
    s = pl.program_id(0)
    j = pl.program_id(1)
    K = PEER_TOPK
    slot = s % 2
    eb = u_ref.shape[0]

    @pl.when(jnp.logical_and(s == 0, j == 0))
    def _():
        pk_ref[...] = jnp.zeros_like(pk_ref)
        scr_ref[...] = jnp.zeros_like(scr_ref)
        g_ref[...] = jnp.zeros_like(g_ref)
        coef_ref[...] = jnp.zeros_like(coef_ref)
        acc_ref[...] = jnp.zeros_like(acc_ref)

    @pl.when(j == 0)
    def _():
        for k in range(pk_ref.shape[0]):
            pt_ref[k] = pk_ref[k].T

    v1, x1 = _top_rows(sct_ref[0], K)
    v2, x2 = _top_rows(sct_ref[1], K)
    cand, ids = _pair_candidates(v1, x1, v2, x2)
    best, expert = _top_rows(cand, K, payload=ids)
    e = jnp.exp(best - best[0:1, :])
    gate = e / jnp.sum(e, axis=0, keepdims=True)
    i1 = jnp.floor(expert * (1.0 / N_KEYS))
    r0 = pl.multiple_of(j * K, K)
    pk_ref[0, pl.ds(r0, K), :] = i1
    pk_ref[1, pl.ds(r0, K), :] = expert - i1 * float(N_KEYS)
    pk_ref[2, pl.ds(r0, K), :] = 0.5 * gate

    npair = pt_ref.shape[-1]
    sub = lax.broadcasted_iota(jnp.int32, (N_KEYS, npair), 0).astype(F32)
    t0 = pl.multiple_of(j * GATE_GROUP, GATE_GROUP)
    for tl in range(GATE_GROUP):
        r = pl.ds(t0 + tl, 1)
        lhs = jnp.where(sub == pt_ref[0, r, :], pt_ref[2, r, :], 0.0).astype(BF16)
        rhs = jnp.where(sub == pt_ref[1, r, :], 1.0, 0.0).astype(BF16)
        scr_ref[pl.ds(tl, N_KEYS, stride=GATE_PITCH), :] = _nt_dot(lhs, rhs)
    for a in range(N_KEYS):
        g_ref[1 - slot, a, pl.ds(t0, GATE_GROUP), :] = (
            scr_ref[a * GATE_PITCH:a * GATE_PITCH + GATE_GROUP, :].astype(g_ref.dtype))

    h = h_ref[...]
    d = h.shape[-1]
    acc_ref[...] += jnp.dot(coef_ref[...], v_ref[:, :d], preferred_element_type=F32)

    a0 = j * (eb // N_KEYS)
    pieces = []
    for q in range(eb // DENSE_SUB):
        act = _nt_dot(h, u_ref[q * DENSE_SUB:(q + 1) * DENSE_SUB, :d])
        w = (act * (1.0 + lax.erf(act * (2.0 ** -0.5)))).astype(BF16)
        for a in range(DENSE_SUB // N_KEYS):
            ga = g_ref[slot, a0 + q * (DENSE_SUB // N_KEYS) + a]
            pieces.append(w[:, a * N_KEYS:(a + 1) * N_KEYS] * ga)
    coef_ref[...] = jnp.concatenate(pieces, axis=1)

    @pl.when(j == 0)
    def _():
        x2 = x_ref[...] + g2_ref[0] * acc_ref[...]
        if final:
            ms = jnp.mean(x2 * x2, axis=-1, keepdims=True)
            x2 = x2 * lax.rsqrt(ms + EPS) * fg_ref[...]
        o_ref[...] = x2
        acc_ref[...] = jnp.zeros_like(acc_ref)


def _peer_call(sct, h2, u, v, x1, g2, fg, final):
    bsz, s, d = x1.shape
    t = bsz * s
    ne = u.shape[0]
    nlist, nk, _ = sct.shape
    tb = min(TB_PEER, s)
    eb = ne // PEER_HEADS
    npair = PEER_HEADS * PEER_TOPK
    assert tb == PEER_HEADS * GATE_GROUP and nlist == 2 * PEER_HEADS and eb % MXU_WIDTH == 0
    nt = t // tb
    per_b = s // tb
    nchunk = PEER_HEADS
    dense_tile = lambda i: jnp.clip(i - 2, 0, nt - 1)
    row = lambda i, j: (dense_tile(i), 0)
    out_tile = lambda i, j: jnp.clip(i - 2 - (j == 0).astype(jnp.int32), 0, nt - 1)
    out_row = lambda i, j: (out_tile(i, j), 0)
    dp = d + LANES if (d // LANES) % 2 == 0 else d
    u = jnp.pad(u, ((0, 0), (0, dp - d)))
    v = jnp.pad(v, ((0, 0), (0, dp - d)))
    out = pl.pallas_call(
        functools.partial(_peer_kernel, final=final),
        grid=(nt + 3, nchunk),
        in_specs=[pl.BlockSpec((2, nk, tb), lambda i, j: (j, 0, jnp.minimum(i, nt - 1))),
                  pl.BlockSpec((tb, d), row),
                  pl.BlockSpec((eb, dp), lambda i, j: (j, 0)),
                  pl.BlockSpec((eb, dp), lambda i, j: ((j + nchunk - 1) % nchunk, 0)),
                  pl.BlockSpec((tb, d), out_row),
                  pl.BlockSpec((1, 1, d), lambda i, j: (out_tile(i, j) // per_b, 0, 0)),
                  pl.BlockSpec((1, d), lambda i, j: (0, 0))],
        out_specs=pl.BlockSpec((tb, d), out_row),
        out_shape=jax.ShapeDtypeStruct((t, d), F32),
        scratch_shapes=[pltpu.VMEM((3, npair, tb), F32),
                        pltpu.VMEM((3, tb, npair), F32),
                        pltpu.VMEM((N_KEYS * GATE_PITCH, N_KEYS), F32),
                        pltpu.VMEM((2, N_KEYS, tb, N_KEYS), BF16),
                        pltpu.VMEM((tb, eb), BF16),
                        pltpu.VMEM((tb, d), F32)],
        compiler_params=_cparams(("arbitrary", "arbitrary")),
        name="peer_fused",
    )(sct, h2.reshape(t, d), u, v, x1.reshape(t, d), g2, fg)
    return out.reshape(bsz, s, d)


def kernel(x, c, ada_w, ada_b, mix_norm_g, ffn_norm_g, w_in, conv_w, conv_b, dt_bias, a_log, d_skip,
           ssd_norm_g, pool_w, pool_scale, w_out, peer_wq, peer_keys, peer_u, peer_v, final_norm_g):
    bsz, s, d = x.shape
    depth = ada_w.shape[0]
    ssd_w = ssd_norm_g.shape[-1]
    heads = dt_bias.shape[-1]
    xbc_w = conv_w.shape[-1]
    pool_wd = pool_scale.shape[-1]
    sb_w = (w_in.shape[-1] - ssd_w - xbc_w - heads - pool_wd) // 3
    o_z, o_xbc, o_dt, o_xp = 0, ssd_w, ssd_w + xbc_w, ssd_w + xbc_w + heads
    o_q = o_xp + pool_wd
    widths = (ssd_w, xbc_w, pool_wd, sb_w, sb_w, sb_w, LANES)

    mods = _mod_call(c, ada_w, ada_b)
    for l in range(depth):
        mod = [mods[l, :, k * d:(k + 1) * d].reshape(bsz, 1, d) for k in range(6)]
        sh1, sc1, g1, sh2, sc2, g2 = mod
        wl = w_in[l]
        w_cat = jnp.concatenate(
            [wl[:, o_z:o_xbc], wl[:, o_xbc:o_dt], wl[:, o_xp:o_q], wl[:, o_q:],
             jnp.pad(wl[:, o_dt:o_xp], ((0, 0), (0, LANES - heads)))], axis=1).astype(BF16)
        z, xbc, xp, q, k, v, dt = _in_call(x, sh1, sc1, mix_norm_g[l].reshape(1, d), w_cat, widths)
        y_ssd = _ssd_call(xbc, dt, z, conv_w[l], conv_b[l], dt_bias[l], a_log[l], d_skip[l],
                          ssd_norm_g[l])
        y_pool = _pool_call(xp, pool_w[l], pool_scale[l])
        y_sb = _sb_call(q, k, v)
        keys = peer_keys[l].reshape(-1, N_KEYS, PEER_HALF).astype(BF16)
        x1, h2, sct = _out_call(y_ssd, y_pool, y_sb, x, g1, sh2, sc2, ffn_norm_g[l].reshape(1, d),
                                w_out[l].astype(BF16), peer_wq[l].astype(BF16), keys)
        x = _peer_call(sct, h2, peer_u[l].astype(BF16), peer_v[l].astype(BF16), x1, g2,
                       final_norm_g.reshape(1, d), final=(l == depth - 1))
    return x
```

```python
import functools

import jax
import jax.numpy as jnp
from jax import lax
from jax.experimental import pallas as pl
from jax.experimental.pallas import tpu as pltpu

F32 = jnp.float32
BF16 = jnp.bfloat16
EPS = 1e-6
HIGHEST = lax.Precision.HIGHEST

SSD_HEAD_DIM = 64
SSD_GROUPS = 2
SSD_STATE = 64
CONV_WIDTH = 4
POOL_WINDOWS = (2, 4, 8, 16)
SB_HEAD_DIM = 64
PEER_HEADS = 8
PEER_TOPK = 16
N_KEYS = 128
PEER_HALF = 128

LANES = 128
SUBLANES = 8
MXU_WIDTH = 256
VMEM_LIMIT = 56 * 1024 * 1024

TM_IN = 512
SSD_CHUNK = 256
HALO = 16
TM_POOL = 512
SB_BLOCK = 128
SB_SKIP = 40.0
GATE_GROUP = 32
GATE_PITCH = 36
TB_PEER = 256
DENSE_SUB = 256


def _cparams(sem):
    return pltpu.CompilerParams(dimension_semantics=sem, vmem_limit_bytes=VMEM_LIMIT)


def _nt_dot(a, b):
    return lax.dot_general(a, b, (((1,), (1,)), ((), ())), preferred_element_type=F32)


def _softplus(x):
    return jnp.maximum(x, 0.0) + jnp.log(1.0 + jnp.exp(-jnp.abs(x)))


def _silu(x):
    return x * jax.nn.sigmoid(x)


def _mod_kernel(c_ref, w_ref, b_ref, o_ref):
    cond = _silu(c_ref[...])
    o_ref[0] = jnp.dot(cond, w_ref[0], preferred_element_type=F32, precision=HIGHEST) + b_ref[0]


def _mod_call(c, ada_w, ada_b):
    nl, d, n6 = ada_w.shape
    bsz = c.shape[0]
    tn = 1536
    return pl.pallas_call(
        _mod_kernel,
        grid=(nl, n6 // tn),
        in_specs=[pl.BlockSpec((bsz, d), lambda l, j: (0, 0)),
                  pl.BlockSpec((1, d, tn), lambda l, j: (l, 0, j)),
                  pl.BlockSpec((1, 1, tn), lambda l, j: (l, 0, j))],
        out_specs=pl.BlockSpec((1, bsz, tn), lambda l, j: (l, 0, j)),
        out_shape=jax.ShapeDtypeStruct((nl, bsz, n6), F32),
        compiler_params=_cparams(("arbitrary", "arbitrary")),
        name="adaln_mod",
    )(c, ada_w, ada_b.reshape(nl, 1, n6))


def _norm_mod(x, g, sh, sc):
    ms = jnp.mean(x * x, axis=-1, keepdims=True)
    y = x * lax.rsqrt(ms + EPS) * g
    return y * (1.0 + sc) + sh


def _in_kernel(x_ref, sh_ref, sc_ref, g_ref, w_ref, z_ref, xbc_ref, xp_ref, q_ref, k_ref, v_ref,
               dt_ref, *, cuts):
    h = _norm_mod(x_ref[0], g_ref[...], sh_ref[0], sc_ref[0]).astype(BF16)
    outs = (z_ref, xbc_ref, xp_ref, q_ref, k_ref, v_ref, dt_ref)
    for o_ref, (a, b) in zip(outs, cuts):
        o_ref[0] = jnp.dot(h, w_ref[:, a:b], preferred_element_type=F32).astype(o_ref.dtype)


def _in_call(x, sh, sc, g, w_cat, widths):
    bsz, s, d = x.shape
    tm = min(TM_IN, s)
    cuts, a = [], 0
    for w in widths:
        cuts.append((a, a + w))
        a += w
    dtypes = (BF16, BF16, BF16, BF16, BF16, BF16, F32)
    tok = lambda w: pl.BlockSpec((1, tm, w), lambda b, i: (b, i, 0))
    vec = pl.BlockSpec((1, 1, d), lambda b, i: (b, 0, 0))
    return pl.pallas_call(
        functools.partial(_in_kernel, cuts=tuple(cuts)),
        grid=(bsz, s // tm),
        in_specs=[tok(d), vec, vec,
                  pl.BlockSpec((1, d), lambda b, i: (0, 0)),
                  pl.BlockSpec(w_cat.shape, lambda b, i: (0, 0))],
        out_specs=[tok(w) for w in widths],
        out_shape=[jax.ShapeDtypeStruct((bsz, s, w), dt) for w, dt in zip(widths, dtypes)],
        compiler_params=_cparams(("arbitrary", "arbitrary")),
        name="in_proj",
    )(x, sh, sc, g, w_cat)


def _ssd_kernel(xbc_ref, halo_ref, dt_ref, z_ref, cw_ref, cb_ref, dtb_ref, alog_ref, expand_ref,
                dsk_ref, ng_ref, y_ref, state_ref, *, chunk, width, gn):
    i = pl.program_id(1)
    L = chunk
    heads = width // SSD_HEAD_DIM
    hpg = heads // SSD_GROUPS
    half = width // SSD_GROUPS

    @pl.when(i == 0)
    def _():
        state_ref[...] = jnp.zeros_like(state_ref)

    hal = jnp.where(i > 0, halo_ref[0].astype(F32), 0.0)
    full = jnp.concatenate([hal, xbc_ref[0].astype(F32)], axis=0)
    cw = cw_ref[...]
    acc = full * cw[CONV_WIDTH - 1:CONV_WIDTH]
    for j in range(1, CONV_WIDTH):
        acc = acc + pltpu.roll(full, j, 0) * cw[CONV_WIDTH - 1 - j:CONV_WIDTH - j]
    xbc = _silu(acc[HALO:] + cb_ref[...])
    xs = xbc[:, :width]
    bm = xbc[:, width:width + gn]
    cm = xbc[:, width + gn:width + 2 * gn]

    dt = _softplus(dt_ref[0] + dtb_ref[...])
    a = dt * (-jnp.exp(alog_ref[...]))
    row_i = lax.broadcasted_iota(jnp.int32, (L, L), 0)
    col_i = lax.broadcasted_iota(jnp.int32, (L, L), 1)
    tril = row_i >= col_i
    a_cs = jnp.dot(tril.astype(F32), a, preferred_element_type=F32, precision=HIGHEST)
    a_cs_t = a_cs.T
    expand = expand_ref[...]
    dt_x = jnp.dot(dt, expand, preferred_element_type=F32, precision=HIGHEST)
    acs_x = jnp.dot(a_cs, expand, preferred_element_type=F32, precision=HIGHEST)
    alast_x = acs_x[L - 1:L, :]
    xdt = xs * dt_x

    bm_b = bm.astype(BF16)
    cm_b = cm.astype(BF16)
    lane_g = lax.broadcasted_iota(jnp.int32, (L, gn), 1) // SSD_STATE
    lane_h = lax.broadcasted_iota(jnp.int32, (L, half), 1) // SSD_HEAD_DIM

    state = state_ref[...]
    y_off = jnp.dot(cm_b, state.astype(BF16), preferred_element_type=F32) * jnp.exp(acs_x)

    y_halves = []
    for g in range(SSD_GROUPS):
        cb = _nt_dot(jnp.where(lane_g == g, cm_b, jnp.zeros_like(cm_b)), bm_b)
        xdt_g = xdt[:, g * half:(g + 1) * half]
        yh = jnp.zeros((L, half), F32)
        for hh in range(hpg):
            h = g * hpg + hh
            seg = a_cs[:, h:h + 1] - a_cs_t[h:h + 1, :]
            dec = jnp.exp(jnp.where(tril, seg, -1e30))
            m = (cb * dec).astype(BF16)
            rhs = jnp.where(lane_h == hh, xdt_g, 0.0).astype(BF16)
            yh = yh + jnp.dot(m, rhs, preferred_element_type=F32)
        y_halves.append(yh)

    ds_x = jnp.exp(alast_x - acs_x)
    upd = jnp.dot(bm.T.astype(BF16), (xdt * ds_x).astype(BF16), preferred_element_type=F32)
    srow = lax.broadcasted_iota(jnp.int32, (gn, width), 0) // SSD_STATE
    scol = lax.broadcasted_iota(jnp.int32, (gn, width), 1) // half
    state_ref[...] = jnp.exp(alast_x) * state + jnp.where(srow == scol, upd, 0.0)

    zg = _silu(z_ref[0].astype(F32))
    for g in range(SSD_GROUPS):
        sl = slice(g * half, (g + 1) * half)
        yg = (y_halves[g] + y_off[:, sl] + dsk_ref[:, sl] * xs[:, sl]) * zg[:, sl]
        ms = jnp.mean(yg * yg, axis=-1, keepdims=True)
        y_ref[0, :, sl] = (yg * lax.rsqrt(ms + EPS) * ng_ref[:, sl]).astype(y_ref.dtype)


def _ssd_call(xbc, dt, z, conv_w, conv_b, dt_bias, a_log, d_skip, ssd_norm_g):
    bsz, s, c = xbc.shape
    width = z.shape[-1]
    gn = SSD_GROUPS * SSD_STATE
    heads = width // SSD_HEAD_DIM
    L = min(SSD_CHUNK, s)
    pad = lambda v: jnp.zeros((1, LANES), F32).at[0, :heads].set(v)
    expand = (jnp.arange(LANES)[:, None] == (jnp.arange(width)[None, :] // SSD_HEAD_DIM)).astype(F32)
    dsk = jnp.repeat(d_skip, SSD_HEAD_DIM).reshape(1, width)
    hb = L // HALO
    const = lambda shp: pl.BlockSpec(shp, lambda b, i: (0, 0))
    return pl.pallas_call(
        functools.partial(_ssd_kernel, chunk=L, width=width, gn=gn),
        grid=(bsz, s // L),
        in_specs=[pl.BlockSpec((1, L, c), lambda b, i: (b, i, 0)),
                  pl.BlockSpec((1, HALO, c), lambda b, i: (b, jnp.maximum(i * hb - 1, 0), 0)),
                  pl.BlockSpec((1, L, LANES), lambda b, i: (b, i, 0)),
                  pl.BlockSpec((1, L, width), lambda b, i: (b, i, 0)),
                  const((CONV_WIDTH, c)), const((1, c)), const((1, LANES)), const((1, LANES)),
                  const((LANES, width)), const((1, width)), const((1, width))],
        out_specs=pl.BlockSpec((1, L, width), lambda b, i: (b, i, 0)),
        out_shape=jax.ShapeDtypeStruct((bsz, s, width), BF16),
        scratch_shapes=[pltpu.VMEM((gn, width), F32)],
        compiler_params=_cparams(("arbitrary", "arbitrary")),
        name="ssd_mixer",
    )(xbc, xbc, dt, z, conv_w, conv_b.reshape(1, c), pad(dt_bias), pad(a_log), expand, dsk,
      ssd_norm_g.reshape(1, width))


def _pool_kernel(xp_ref, halo_ref, w_ref, sc_ref, y_ref, *, tm, gdim):
    i = pl.program_id(1)
    hal = jnp.where(i > 0, halo_ref[0].astype(F32), 0.0)
    x = xp_ref[0].astype(F32)
    full = jnp.concatenate([hal, x], axis=0)
    lane_g = lax.broadcasted_iota(jnp.int32, x.shape, 1) // gdim
    tpos = i * tm + lax.broadcasted_iota(jnp.int32, x.shape, 0)
    win_sum = jnp.zeros_like(x)
    cnt = jnp.ones_like(x)
    s = full
    span = 1
    for gi, w in enumerate(POOL_WINDOWS):
        while span < w:
            s = s + pltpu.roll(s, span, 0)
            span *= 2
        win_sum = jnp.where(lane_g == gi, s[HALO:], win_sum)
        cnt = jnp.where(lane_g == gi, jnp.minimum(tpos + 1, w).astype(F32), cnt)
    pooled = (win_sum / cnt - x).astype(BF16)
    y = jnp.dot(pooled, w_ref[...], preferred_element_type=F32) * sc_ref[...]
    y_ref[0] = y.astype(y_ref.dtype)


def _pool_call(xp, pool_w, pool_scale):
    bsz, s, w = xp.shape
    ng, gdim, _ = pool_w.shape
    assert POOL_WINDOWS[-1] <= HALO and all(b == 2 * a for a, b in zip(POOL_WINDOWS, POOL_WINDOWS[1:]))
    tm = min(TM_POOL, s)
    wbd = jnp.zeros((w, w), F32)
    for g in range(ng):
        wbd = wbd.at[g * gdim:(g + 1) * gdim, g * gdim:(g + 1) * gdim].set(pool_w[g])
    hb = tm // HALO
    return pl.pallas_call(
        functools.partial(_pool_kernel, tm=tm, gdim=gdim),
        grid=(bsz, s // tm),
        in_specs=[pl.BlockSpec((1, tm, w), lambda b, i: (b, i, 0)),
                  pl.BlockSpec((1, HALO, w), lambda b, i: (b, jnp.maximum(i * hb - 1, 0), 0)),
                  pl.BlockSpec((w, w), lambda b, i: (0, 0)),
                  pl.BlockSpec((1, w), lambda b, i: (0, 0))],
        out_specs=pl.BlockSpec((1, tm, w), lambda b, i: (b, i, 0)),
        out_shape=jax.ShapeDtypeStruct((bsz, s, w), BF16),
        compiler_params=_cparams(("arbitrary", "arbitrary")),
        name="pool_mixer",
    )(xp, xp, wbd.astype(BF16), pool_scale.reshape(1, w))


def _sb_kernel(q_ref, k_ref, v_ref, y_ref, qm_ref, acc_ref, carry_ref, *, blk, heads):
    i = pl.program_id(1)
    T = blk
    w = q_ref.shape[-1]
    lane_h = lax.broadcasted_iota(jnp.int32, (T, w), 1) // SB_HEAD_DIM
    q = q_ref[0] * (SB_HEAD_DIM ** -0.5)
    for h in range(heads):
        qm_ref[h * T:(h + 1) * T, :] = jnp.where(lane_h == h, q, jnp.zeros_like(q))
    row_i = lax.broadcasted_iota(jnp.int32, (T, T), 0)
    col_i = lax.broadcasted_iota(jnp.int32, (T, T), 1)
    suffix = (row_i > col_i).astype(BF16)
    qrow = lax.broadcasted_iota(jnp.int32, (heads * T, T), 0) % T
    causal = lax.broadcasted_iota(jnp.int32, (heads * T, T), 1) < qrow

    def scores(j, mask):
        ks = k_ref[0, pl.ds(pl.multiple_of(j * T, T), T), :]
        z = _nt_dot(qm_ref[...], ks)
        lk = -_softplus(z)
        if mask is not None:
            lk = jnp.where(mask, lk, 0.0)
        hi = lk.astype(BF16)
        lo = (lk - hi.astype(F32)).astype(BF16)
        later = (jnp.dot(hi, suffix, preferred_element_type=F32)
                 + jnp.dot(lo, suffix, preferred_element_type=F32))
        return z, lk, later

    def weighted(j, z, lk, later, c, mask):
        vs = v_ref[0, pl.ds(pl.multiple_of(j * T, T), T), :]
        wgt = jnp.exp(z + lk + later + c)
        if mask is not None:
            wgt = jnp.where(mask, wgt, 0.0)
        pv = jnp.dot(wgt.astype(BF16), vs, preferred_element_type=F32)
        out = pv[0:T]
        for h in range(1, heads):
            out = jnp.where(lane_h == h, pv[h * T:(h + 1) * T], out)
        return out, c + jnp.sum(lk, axis=1, keepdims=True)

    def alive():
        return (jnp.max(carry_ref[...]) > -SB_SKIP).astype(jnp.int32)

    prev = jnp.maximum(i - 1, 0)
    has_prev = jnp.broadcast_to(i > 0, causal.shape)
    zd, lkd, ltd = scores(i, causal)
    zp, lkp, ltp = scores(prev, has_prev)
    out_d, c = weighted(i, zd, lkd, ltd, jnp.zeros((heads * T, 1), F32), causal)
    out_p, c = weighted(prev, zp, lkp, ltp, c, has_prev)
    acc_ref[...] = out_d + out_p
    carry_ref[...] = c

    def cond(st):
        return jnp.logical_and(st[0] >= 0, st[1] > 0)

    def body(st):
        j = st[0]
        z, lk, later = scores(j, None)
        out, c = weighted(j, z, lk, later, carry_ref[...], None)
        acc_ref[...] += out
        carry_ref[...] = c
        return j - 1, alive()

    lax.while_loop(cond, body, (i - 2, alive()))
    y_ref[0] = acc_ref[...].astype(y_ref.dtype)


def _sb_call(q, k, v):
    bsz, s, w = q.shape
    heads = w // SB_HEAD_DIM
    T = min(SB_BLOCK, s)
    return pl.pallas_call(
        functools.partial(_sb_kernel, blk=T, heads=heads),
        grid=(bsz, s // T),
        in_specs=[pl.BlockSpec((1, T, w), lambda b, i: (b, i, 0)),
                  pl.BlockSpec((1, s, w), lambda b, i: (b, 0, 0)),
                  pl.BlockSpec((1, s, w), lambda b, i: (b, 0, 0))],
        out_specs=pl.BlockSpec((1, T, w), lambda b, i: (b, i, 0)),
        out_shape=jax.ShapeDtypeStruct((bsz, s, w), BF16),
        scratch_shapes=[pltpu.VMEM((heads * T, w), BF16),
                        pltpu.VMEM((T, w), F32),
                        pltpu.VMEM((heads * T, 1), F32)],
        compiler_params=_cparams(("arbitrary", "arbitrary")),
        name="stick_breaking",
    )(q, k, v)


def _out_kernel(ys_ref, yp_ref, yb_ref, x_ref, g1_ref, sh_ref, sc_ref, ng_ref, wo_ref, wq_ref,
                keys_ref, x1_ref, h2_ref, sct_ref, *, cuts):
    mix = jnp.zeros(x_ref.shape[1:], F32)
    for y_ref, (a, b) in zip((ys_ref, yp_ref, yb_ref), cuts):
        mix = mix + jnp.dot(y_ref[0], wo_ref[a:b, :], preferred_element_type=F32)
    x1 = x_ref[0] + g1_ref[0] * mix
    x1_ref[0] = x1
    h2 = _norm_mod(x1, ng_ref[...], sh_ref[0], sc_ref[0]).astype(BF16)
    h2_ref[0] = h2
    qb = jnp.dot(h2, wq_ref[...], preferred_element_type=F32).astype(BF16)
    tp = sct_ref.shape[-1]
    for lst in range(keys_ref.shape[0]):
        sc_t = _nt_dot(keys_ref[lst], qb[:, lst * PEER_HALF:(lst + 1) * PEER_HALF])
        for k in range(sct_ref.shape[0]):
            sct_ref[k, lst] = sc_t[:, k * tp:(k + 1) * tp]


def _out_call(ys, yp, yb, x, g1, sh, sc, ng, w_out, wq, keys):
    bsz, s, d = x.shape
    tm = min(TM_IN, s)
    nlist = keys.shape[0]
    cuts, a = [], 0
    for y in (ys, yp, yb):
        cuts.append((a, a + y.shape[-1]))
        a += y.shape[-1]
    tok = lambda w: pl.BlockSpec((1, tm, w), lambda b, i: (b, i, 0))
    vec = pl.BlockSpec((1, 1, d), lambda b, i: (b, 0, 0))
    full2 = lambda arr: pl.BlockSpec(arr.shape, lambda b, i: (0,) * arr.ndim)
    nblk = s // tm
    tp = min(TB_PEER, s)
    per = tm // tp
    return pl.pallas_call(
        functools.partial(_out_kernel, cuts=tuple(cuts)),
        grid=(bsz, nblk),
        in_specs=[tok(ys.shape[-1]), tok(yp.shape[-1]), tok(yb.shape[-1]), tok(d), vec, vec, vec,
                  pl.BlockSpec((1, d), lambda b, i: (0, 0)), full2(w_out), full2(wq), full2(keys)],
        out_specs=[tok(d), tok(d),
                   pl.BlockSpec((per, nlist, N_KEYS, tp), lambda b, i: (b * nblk + i, 0, 0, 0))],
        out_shape=[jax.ShapeDtypeStruct((bsz, s, d), F32),
                   jax.ShapeDtypeStruct((bsz, s, d), BF16),
                   jax.ShapeDtypeStruct((bsz * s // tp, nlist, N_KEYS, tp), F32)],
        compiler_params=_cparams(("arbitrary", "arbitrary")),
        name="out_proj_peer_query",
    )(ys, yp, yb, x, g1, sh, sc, ng, w_out, wq, keys)


def _top_rows(s, k, payload=None):
    nrows = s.shape[0]
    rows = lax.broadcasted_iota(jnp.int32, s.shape, 0).astype(F32)
    vals, picks = [], []
    for _ in range(k):
        m = jnp.max(s, axis=0, keepdims=True)
        idx = jnp.min(jnp.where(s == m, rows, float(nrows)), axis=0, keepdims=True)
        hit = rows == idx
        vals.append(m)
        if payload is None:
            picks.append(idx)
        else:
            picks.append(jnp.sum(jnp.where(hit, payload, 0.0), axis=0, keepdims=True))
        s = jnp.where(hit, -jnp.inf, s)
    return jnp.concatenate(vals, axis=0), jnp.concatenate(picks, axis=0)


def _pair_candidates(v1, x1, v2, x2):
    K = PEER_TOPK
    cands, ids = [], []
    a = 0
    while K // (a + 1) > 1:
        n = K // (a + 1)
        npad = -(-n // SUBLANES) * SUBLANES
        c = v1[a:a + 1, :] + v2[0:npad, :]
        if n < npad:
            c = jnp.where(lax.broadcasted_iota(jnp.int32, c.shape, 0) < n, c, -jnp.inf)
        cands.append(c)
        ids.append(x1[a:a + 1, :] * float(N_KEYS) + x2[0:npad, :])
        a += 1
    cands.append(v1[a:K, :] + v2[0:1, :])
    ids.append(x1[a:K, :] * float(N_KEYS) + x2[0:1, :])
    return jnp.concatenate(cands, axis=0), jnp.concatenate(ids, axis=0)


def _peer_kernel(sct_ref, h_ref, u_ref, v_ref, x_ref, g2_ref, fg_ref, o_ref,
                 pk_ref, pt_ref, scr_ref, g_ref, coef_ref, acc_ref, *, final):
    s = pl.program_id(0)
    j = pl.program_id(1)
    K = PEER_TOPK
    slot = s % 2
    eb = u_ref.shape[0]

    @pl.when(jnp.logical_and(s == 0, j == 0))
    def _():
        pk_ref[...] = jnp.zeros_like(pk_ref)
        scr_ref[...] = jnp.zeros_like(scr_ref)
        g_ref[...] = jnp.zeros_like(g_ref)
        coef_ref[...] = jnp.zeros_like(coef_ref)
        acc_ref[...] = jnp.zeros_like(acc_ref)

    @pl.when(j == 0)
    def _():
        for k in range(pk_ref.shape[0]):
            pt_ref[k] = pk_ref[k].T

    v1, x1 = _top_rows(sct_ref[0], K)
    v2, x2 = _top_rows(sct_ref[1], K)
    cand, ids = _pair_candidates(v1, x1, v2, x2)
    best, expert = _top_rows(cand, K, payload=ids)
    e = jnp.exp(best - best[0:1, :])
    gate = e / jnp.sum(e, axis=0, keepdims=True)
    i1 = jnp.floor(expert * (1.0 / N_KEYS))
    r0 = pl.multiple_of(j * K, K)
    pk_ref[0, pl.ds(r0, K), :] = i1
    pk_ref[1, pl.ds(r0, K), :] = expert - i1 * float(N_KEYS)
    pk_ref[2, pl.ds(r0, K), :] = 0.5 * gate

    npair = pt_ref.shape[-1]
    sub = lax.broadcasted_iota(jnp.int32, (N_KEYS, npair), 0).astype(F32)
    t0 = pl.multiple_of(j * GATE_GROUP, GATE_GROUP)
    for tl in range(GATE_GROUP):
        r = pl.ds(t0 + tl, 1)
        lhs = jnp.where(sub == pt_ref[0, r, :], pt_ref[2, r, :], 0.0).astype(BF16)
        rhs = jnp.where(sub == pt_ref[1, r, :], 1.0, 0.0).astype(BF16)
        scr_ref[pl.ds(tl, N_KEYS, stride=GATE_PITCH), :] = _nt_dot(lhs, rhs)
    for a in range(N_KEYS):
        g_ref[1 - slot, a, pl.ds(t0, GATE_GROUP), :] = (
            scr_ref[a * GATE_PITCH:a * GATE_PITCH + GATE_GROUP, :].astype(g_ref.dtype))

    h = h_ref[...]
    d = h.shape[-1]
    acc_ref[...] += jnp.dot(coef_ref[...], v_ref[:, :d], preferred_element_type=F32)

    a0 = j * (eb // N_KEYS)
    pieces = []
    for q in range(eb // DENSE_SUB):
        act = _nt_dot(h, u_ref[q * DENSE_SUB:(q + 1) * DENSE_SUB, :d])
        w = (act * (1.0 + lax.erf(act * (2.0 ** -0.5)))).astype(BF16)
        for a in range(DENSE_SUB // N_KEYS):
            ga = g_ref[slot, a0 + q * (DENSE_SUB // N_KEYS) + a]
            pieces.append(w[:, a * N_KEYS:(a + 1) * N_KEYS] * ga)
    coef_ref[...] = jnp.concatenate(pieces, axis=1)

    @pl.when(j == 0)
    def _():
        x2 = x_ref[...] + g2_ref[0] * acc_ref[...]
        if final:
            ms = jnp.mean(x2 * x2, axis=-1, keepdims=True)
            x2 = x2 * lax.rsqrt(ms + EPS) * fg_ref[...]
        o_ref[...] = x2
        acc_ref[...] = jnp.zeros_like(acc_ref)


def _peer_call(sct, h2, u, v, x1, g2, fg, final):
    bsz, s, d = x1.shape
    t = bsz * s
    ne = u.shape[0]
    _, nlist, nk, _ = sct.shape
    tb = min(TB_PEER, s)
    eb = ne // PEER_HEADS
    npair = PEER_HEADS * PEER_TOPK
    assert tb == PEER_HEADS * GATE_GROUP and nlist == 2 * PEER_HEADS and eb % MXU_WIDTH == 0
    nt = t // tb
    per_b = s // tb
    nchunk = PEER_HEADS
    dense_tile = lambda i: jnp.clip(i - 2, 0, nt - 1)
    row = lambda i, j: (dense_tile(i), 0)
    out_tile = lambda i, j: jnp.clip(i - 2 - (j == 0).astype(jnp.int32), 0, nt - 1)
    out_row = lambda i, j: (out_tile(i, j), 0)
    dp = d + LANES if (d // LANES) % 2 == 0 else d
    u = jnp.pad(u, ((0, 0), (0, dp - d)))
    v = jnp.pad(v, ((0, 0), (0, dp - d)))
    out = pl.pallas_call(
        functools.partial(_peer_kernel, final=final),
        grid=(nt + 3, nchunk),
        in_specs=[pl.BlockSpec((None, 2, nk, tb), lambda i, j: (jnp.minimum(i, nt - 1), j, 0, 0)),
                  pl.BlockSpec((tb, d), row),
                  pl.BlockSpec((eb, dp), lambda i, j: (j, 0)),
                  pl.BlockSpec((eb, dp), lambda i, j: ((j + nchunk - 1) % nchunk, 0)),
                  pl.BlockSpec((tb, d), out_row),
                  pl.BlockSpec((1, 1, d), lambda i, j: (out_tile(i, j) // per_b, 0, 0)),
                  pl.BlockSpec((1, d), lambda i, j: (0, 0))],
        out_specs=pl.BlockSpec((tb, d), out_row),
        out_shape=jax.ShapeDtypeStruct((t, d), F32),
        scratch_shapes=[pltpu.VMEM((3, npair, tb), F32),
                        pltpu.VMEM((3, tb, npair), F32),
                        pltpu.VMEM((N_KEYS * GATE_PITCH, N_KEYS), F32),
                        pltpu.VMEM((2, N_KEYS, tb, N_KEYS), BF16),
                        pltpu.VMEM((tb, eb), BF16),
                        pltpu.VMEM((tb, d), F32)],
        compiler_params=_cparams(("arbitrary", "arbitrary")),
        name="peer_fused",
    )(sct, h2.reshape(t, d), u, v, x1.reshape(t, d), g2, fg)
    return out.reshape(bsz, s, d)


def kernel(x, c, ada_w, ada_b, mix_norm_g, ffn_norm_g, w_in, conv_w, conv_b, dt_bias, a_log, d_skip,
           ssd_norm_g, pool_w, pool_scale, w_out, peer_wq, peer_keys, peer_u, peer_v, final_norm_g):
    bsz, s, d = x.shape
    depth = ada_w.shape[0]
    ssd_w = ssd_norm_g.shape[-1]
    heads = dt_bias.shape[-1]
    xbc_w = conv_w.shape[-1]
    pool_wd = pool_scale.shape[-1]
    sb_w = (w_in.shape[-1] - ssd_w - xbc_w - heads - pool_wd) // 3
    o_z, o_xbc, o_dt, o_xp = 0, ssd_w, ssd_w + xbc_w, ssd_w + xbc_w + heads
    o_q = o_xp + pool_wd
    widths = (ssd_w, xbc_w, pool_wd, sb_w, sb_w, sb_w, LANES)

    mods = _mod_call(c, ada_w, ada_b)
    for l in range(depth):
        mod = [mods[l, :, k * d:(k + 1) * d].reshape(bsz, 1, d) for k in range(6)]
        sh1, sc1, g1, sh2, sc2, g2 = mod
        wl = w_in[l]
        w_cat = jnp.concatenate(
            [wl[:, o_z:o_xbc], wl[:, o_xbc:o_dt], wl[:, o_xp:o_q], wl[:, o_q:],
             jnp.pad(wl[:, o_dt:o_xp], ((0, 0), (0, LANES - heads)))], axis=1).astype(BF16)
        z, xbc, xp, q, k, v, dt = _in_call(x, sh1, sc1, mix_norm_g[l].reshape(1, d), w_cat, widths)
        y_ssd = _ssd_call(xbc, dt, z, conv_w[l], conv_b[l], dt_bias[l], a_log[l], d_skip[l],
                          ssd_norm_g[l])
        y_pool = _pool_call(xp, pool_w[l], pool_scale[l])
        y_sb = _sb_call(q, k, v)
        keys = peer_keys[l].reshape(-1, N_KEYS, PEER_HALF).astype(BF16)
        x1, h2, sct = _out_call(y_ssd, y_pool, y_sb, x, g1, sh2, sc2, ffn_norm_g[l].reshape(1, d),
                                w_out[l].astype(BF16), peer_wq[l].astype(BF16), keys)
        x = _peer_call(sct, h2, peer_u[l].astype(BF16), peer_v[l].astype(BF16), x1, g2,
                       final_norm_g.reshape(1, d), final=(l == depth - 1))
    return x
```

```python
import functools

import jax
import jax.numpy as jnp
from jax import lax
from jax.experimental import pallas as pl
from jax.experimental.pallas import tpu as pltpu

F32 = jnp.float32
BF16 = jnp.bfloat16
EPS = 1e-6
HIGHEST = lax.Precision.HIGHEST

SSD_HEAD_DIM = 64
SSD_GROUPS = 2
SSD_STATE = 64
CONV_WIDTH = 4
POOL_WINDOWS = (2, 4, 8, 16)
SB_HEAD_DIM = 64
PEER_HEADS = 8
PEER_TOPK = 16
N_KEYS = 128
PEER_HALF = 128

LANES = 128
SUBLANES = 8
MXU_WIDTH = 256
VMEM_LIMIT = 56 * 1024 * 1024

TM_IN = 512
SSD_CHUNK = 256
HALO = 16
TM_POOL = 512
SB_BLOCK = 128
SB_SKIP = 40.0
GATE_GROUP = 32
GATE_PITCH = 36
TB_PEER = 256
DENSE_SUB = 256


def _cparams(sem):
    return pltpu.CompilerParams(dimension_semantics=sem, vmem_limit_bytes=VMEM_LIMIT)


def _nt_dot(a, b):
    return lax.dot_general(a, b, (((1,), (1,)), ((), ())), preferred_element_type=F32)


def _softplus(x):
    return jnp.maximum(x, 0.0) + jnp.log(1.0 + jnp.exp(-jnp.abs(x)))


def _silu(x):
    return x * jax.nn.sigmoid(x)


def _mod_kernel(c_ref, w_ref, b_ref, o_ref):
    cond = _silu(c_ref[...])
    o_ref[0] = jnp.dot(cond, w_ref[0], preferred_element_type=F32, precision=HIGHEST) + b_ref[0]


def _mod_call(c, ada_w, ada_b):
    nl, d, n6 = ada_w.shape
    bsz = c.shape[0]
    tn = 1536
    return pl.pallas_call(
        _mod_kernel,
        grid=(nl, n6 // tn),
        in_specs=[pl.BlockSpec((bsz, d), lambda l, j: (0, 0)),
                  pl.BlockSpec((1, d, tn), lambda l, j: (l, 0, j)),
                  pl.BlockSpec((1, 1, tn), lambda l, j: (l, 0, j))],
        out_specs=pl.BlockSpec((1, bsz, tn), lambda l, j: (l, 0, j)),
        out_shape=jax.ShapeDtypeStruct((nl, bsz, n6), F32),
        compiler_params=_cparams(("arbitrary", "arbitrary")),
        name="adaln_mod",
    )(c, ada_w, ada_b.reshape(nl, 1, n6))


def _norm_mod(x, g, sh, sc):
    ms = jnp.mean(x * x, axis=-1, keepdims=True)
    y = x * lax.rsqrt(ms + EPS) * g
    return y * (1.0 + sc) + sh


def _in_kernel(x_ref, sh_ref, sc_ref, g_ref, w_ref, z_ref, xbc_ref, xp_ref, q_ref, k_ref, v_ref,
               dt_ref, *, cuts):
    h = _norm_mod(x_ref[0], g_ref[...], sh_ref[0], sc_ref[0]).astype(BF16)
    outs = (z_ref, xbc_ref, xp_ref, q_ref, k_ref, v_ref, dt_ref)
    for o_ref, (a, b) in zip(outs, cuts):
        o_ref[0] = jnp.dot(h, w_ref[:, a:b], preferred_element_type=F32).astype(o_ref.dtype)


def _in_call(x, sh, sc, g, w_cat, widths):
    bsz, s, d = x.shape
    tm = min(TM_IN, s)
    cuts, a = [], 0
    for w in widths:
        cuts.append((a, a + w))
        a += w
    dtypes = (BF16, BF16, BF16, BF16, BF16, BF16, F32)
    tok = lambda w: pl.BlockSpec((1, tm, w), lambda b, i: (b, i, 0))
    vec = pl.BlockSpec((1, 1, d), lambda b, i: (b, 0, 0))
    return pl.pallas_call(
        functools.partial(_in_kernel, cuts=tuple(cuts)),
        grid=(bsz, s // tm),
        in_specs=[tok(d), vec, vec,
                  pl.BlockSpec((1, d), lambda b, i: (0, 0)),
                  pl.BlockSpec(w_cat.shape, lambda b, i: (0, 0))],
        out_specs=[tok(w) for w in widths],
        out_shape=[jax.ShapeDtypeStruct((bsz, s, w), dt) for w, dt in zip(widths, dtypes)],
        compiler_params=_cparams(("arbitrary", "arbitrary")),
        name="in_proj",
    )(x, sh, sc, g, w_cat)


def _ssd_kernel(xbc_ref, halo_ref, dt_ref, z_ref, cw_ref, cb_ref, dtb_ref, alog_ref, expand_ref,
                dsk_ref, ng_ref, y_ref, state_ref, *, chunk, width, gn):
    i = pl.program_id(1)
    L = chunk
    heads = width // SSD_HEAD_DIM
    hpg = heads // SSD_GROUPS
    half = width // SSD_GROUPS

    @pl.when(i == 0)
    def _():
        state_ref[...] = jnp.zeros_like(state_ref)

    hal = jnp.where(i > 0, halo_ref[0].astype(F32), 0.0)
    full = jnp.concatenate([hal, xbc_ref[0].astype(F32)], axis=0)
    cw = cw_ref[...]
    acc = full * cw[CONV_WIDTH - 1:CONV_WIDTH]
    for j in range(1, CONV_WIDTH):
        acc = acc + pltpu.roll(full, j, 0) * cw[CONV_WIDTH - 1 - j:CONV_WIDTH - j]
    xbc = _silu(acc[HALO:] + cb_ref[...])
    xs = xbc[:, :width]
    bm = xbc[:, width:width + gn]
    cm = xbc[:, width + gn:width + 2 * gn]

    dt = _softplus(dt_ref[0] + dtb_ref[...])
    a = dt * (-jnp.exp(alog_ref[...]))
    row_i = lax.broadcasted_iota(jnp.int32, (L, L), 0)
    col_i = lax.broadcasted_iota(jnp.int32, (L, L), 1)
    tril = row_i >= col_i
    a_cs = jnp.dot(tril.astype(F32), a, preferred_element_type=F32, precision=HIGHEST)
    a_cs_t = a_cs.T
    expand = expand_ref[...]
    dt_x = jnp.dot(dt, expand, preferred_element_type=F32, precision=HIGHEST)
    acs_x = jnp.dot(a_cs, expand, preferred_element_type=F32, precision=HIGHEST)
    alast_x = acs_x[L - 1:L, :]
    xdt = xs * dt_x

    bm_b = bm.astype(BF16)
    cm_b = cm.astype(BF16)
    lane_g = lax.broadcasted_iota(jnp.int32, (L, gn), 1) // SSD_STATE
    lane_h = lax.broadcasted_iota(jnp.int32, (L, half), 1) // SSD_HEAD_DIM

    state = state_ref[...]
    y_off = jnp.dot(cm_b, state.astype(BF16), preferred_element_type=F32) * jnp.exp(acs_x)

    y_halves = []
    for g in range(SSD_GROUPS):
        cb = _nt_dot(jnp.where(lane_g == g, cm_b, jnp.zeros_like(cm_b)), bm_b)
        xdt_g = xdt[:, g * half:(g + 1) * half]
        yh = jnp.zeros((L, half), F32)
        for hh in range(hpg):
            h = g * hpg + hh
            seg = a_cs[:, h:h + 1] - a_cs_t[h:h + 1, :]
            dec = jnp.exp(jnp.where(tril, seg, -1e30))
            m = (cb * dec).astype(BF16)
            rhs = jnp.where(lane_h == hh, xdt_g, 0.0).astype(BF16)
            yh = yh + jnp.dot(m, rhs, preferred_element_type=F32)
        y_halves.append(yh)

    ds_x = jnp.exp(alast_x - acs_x)
    upd = jnp.dot(bm.T.astype(BF16), (xdt * ds_x).astype(BF16), preferred_element_type=F32)
    srow = lax.broadcasted_iota(jnp.int32, (gn, width), 0) // SSD_STATE
    scol = lax.broadcasted_iota(jnp.int32, (gn, width), 1) // half
    state_ref[...] = jnp.exp(alast_x) * state + jnp.where(srow == scol, upd, 0.0)

    zg = _silu(z_ref[0].astype(F32))
    for g in range(SSD_GROUPS):
        sl = slice(g * half, (g + 1) * half)
        yg = (y_halves[g] + y_off[:, sl] + dsk_ref[:, sl] * xs[:, sl]) * zg[:, sl]
        ms = jnp.mean(yg * yg, axis=-1, keepdims=True)
        y_ref[0, :, sl] = (yg * lax.rsqrt(ms + EPS) * ng_ref[:, sl]).astype(y_ref.dtype)


def _ssd_call(xbc, dt, z, conv_w, conv_b, dt_bias, a_log, d_skip, ssd_norm_g):
    bsz, s, c = xbc.shape
    width = z.shape[-1]
    gn = SSD_GROUPS * SSD_STATE
    heads = width // SSD_HEAD_DIM
    L = min(SSD_CHUNK, s)
    pad = lambda v: jnp.zeros((1, LANES), F32).at[0, :heads].set(v)
    expand = (jnp.arange(LANES)[:, None] == (jnp.arange(width)[None, :] // SSD_HEAD_DIM)).astype(F32)
    dsk = jnp.repeat(d_skip, SSD_HEAD_DIM).reshape(1, width)
    hb = L // HALO
    const = lambda shp: pl.BlockSpec(shp, lambda b, i: (0, 0))
    return pl.pallas_call(
        functools.partial(_ssd_kernel, chunk=L, width=width, gn=gn),
        grid=(bsz, s // L),
        in_specs=[pl.BlockSpec((1, L, c), lambda b, i: (b, i, 0)),
                  pl.BlockSpec((1, HALO, c), lambda b, i: (b, jnp.maximum(i * hb - 1, 0), 0)),
                  pl.BlockSpec((1, L, LANES), lambda b, i: (b, i, 0)),
                  pl.BlockSpec((1, L, width), lambda b, i: (b, i, 0)),
                  const((CONV_WIDTH, c)), const((1, c)), const((1, LANES)), const((1, LANES)),
                  const((LANES, width)), const((1, width)), const((1, width))],
        out_specs=pl.BlockSpec((1, L, width), lambda b, i: (b, i, 0)),
        out_shape=jax.ShapeDtypeStruct((bsz, s, width), BF16),
        scratch_shapes=[pltpu.VMEM((gn, width), F32)],
        compiler_params=_cparams(("arbitrary", "arbitrary")),
        name="ssd_mixer",
    )(xbc, xbc, dt, z, conv_w, conv_b.reshape(1, c), pad(dt_bias), pad(a_log), expand, dsk,
      ssd_norm_g.reshape(1, width))


def _pool_kernel(xp_ref, halo_ref, w_ref, sc_ref, y_ref, *, tm, gdim):
    i = pl.program_id(1)
    hal = jnp.where(i > 0, halo_ref[0].astype(F32), 0.0)
    x = xp_ref[0].astype(F32)
    full = jnp.concatenate([hal, x], axis=0)
    lane_g = lax.broadcasted_iota(jnp.int32, x.shape, 1) // gdim
    tpos = i * tm + lax.broadcasted_iota(jnp.int32, x.shape, 0)
    win_sum = jnp.zeros_like(x)
    cnt = jnp.ones_like(x)
    s = full
    span = 1
    for gi, w in enumerate(POOL_WINDOWS):
        while span < w:
            s = s + pltpu.roll(s, span, 0)
            span *= 2
        win_sum = jnp.where(lane_g == gi, s[HALO:], win_sum)
        cnt = jnp.where(lane_g == gi, jnp.minimum(tpos + 1, w).astype(F32), cnt)
    pooled = (win_sum / cnt - x).astype(BF16)
    y = jnp.dot(pooled, w_ref[...], preferred_element_type=F32) * sc_ref[...]
    y_ref[0] = y.astype(y_ref.dtype)


def _pool_call(xp, pool_w, pool_scale):
    bsz, s, w = xp.shape
    ng, gdim, _ = pool_w.shape
    assert POOL_WINDOWS[-1] <= HALO and all(b == 2 * a for a, b in zip(POOL_WINDOWS, POOL_WINDOWS[1:]))
    tm = min(TM_POOL, s)
    wbd = jnp.zeros((w, w), F32)
    for g in range(ng):
        wbd = wbd.at[g * gdim:(g + 1) * gdim, g * gdim:(g + 1) * gdim].set(pool_w[g])
    hb = tm // HALO
    return pl.pallas_call(
        functools.partial(_pool_kernel, tm=tm, gdim=gdim),
        grid=(bsz, s // tm),
        in_specs=[pl.BlockSpec((1, tm, w), lambda b, i: (b, i, 0)),
                  pl.BlockSpec((1, HALO, w), lambda b, i: (b, jnp.maximum(i * hb - 1, 0), 0)),
                  pl.BlockSpec((w, w), lambda b, i: (0, 0)),
                  pl.BlockSpec((1, w), lambda b, i: (0, 0))],
        out_specs=pl.BlockSpec((1, tm, w), lambda b, i: (b, i, 0)),
        out_shape=jax.ShapeDtypeStruct((bsz, s, w), BF16),
        compiler_params=_cparams(("arbitrary", "arbitrary")),
        name="pool_mixer",
    )(xp, xp, wbd.astype(BF16), pool_scale.reshape(1, w))


def _sb_kernel(q_ref, k_ref, v_ref, y_ref, qm_ref, acc_ref, carry_ref, *, blk, heads):
    i = pl.program_id(1)
    T = blk
    w = q_ref.shape[-1]
    lane_h = lax.broadcasted_iota(jnp.int32, (T, w), 1) // SB_HEAD_DIM
    q = q_ref[0] * (SB_HEAD_DIM ** -0.5)
    for h in range(heads):
        qm_ref[h * T:(h + 1) * T, :] = jnp.where(lane_h == h, q, jnp.zeros_like(q))
    row_i = lax.broadcasted_iota(jnp.int32, (T, T), 0)
    col_i = lax.broadcasted_iota(jnp.int32, (T, T), 1)
    suffix = (row_i > col_i).astype(BF16)
    qrow = lax.broadcasted_iota(jnp.int32, (heads * T, T), 0) % T
    causal = lax.broadcasted_iota(jnp.int32, (heads * T, T), 1) < qrow

    def scores(j, mask):
        ks = k_ref[0, pl.ds(pl.multiple_of(j * T, T), T), :]
        z = _nt_dot(qm_ref[...], ks)
        lk = -_softplus(z)
        if mask is not None:
            lk = jnp.where(mask, lk, 0.0)
        hi = lk.astype(BF16)
        lo = (lk - hi.astype(F32)).astype(BF16)
        later = (jnp.dot(hi, suffix, preferred_element_type=F32)
                 + jnp.dot(lo, suffix, preferred_element_type=F32))
        return z, lk, later

    def weighted(j, z, lk, later, c, mask):
        vs = v_ref[0, pl.ds(pl.multiple_of(j * T, T), T), :]
        wgt = jnp.exp(z + lk + later + c)
        if mask is not None:
            wgt = jnp.where(mask, wgt, 0.0)
        pv = jnp.dot(wgt.astype(BF16), vs, preferred_element_type=F32)
        out = pv[0:T]
        for h in range(1, heads):
            out = jnp.where(lane_h == h, pv[h * T:(h + 1) * T], out)
        return out, c + jnp.sum(lk, axis=1, keepdims=True)

    def alive():
        return (jnp.max(carry_ref[...]) > -SB_SKIP).astype(jnp.int32)

    prev = jnp.maximum(i - 1, 0)
    has_prev = jnp.broadcast_to(i > 0, causal.shape)
    zd, lkd, ltd = scores(i, causal)
    zp, lkp, ltp = scores(prev, has_prev)
    out_d, c = weighted(i, zd, lkd, ltd, jnp.zeros((heads * T, 1), F32), causal)
    out_p, c = weighted(prev, zp, lkp, ltp, c, has_prev)
    acc_ref[...] = out_d + out_p
    carry_ref[...] = c

    def cond(st):
        return jnp.logical_and(st[0] >= 0, st[1] > 0)

    def body(st):
        j = st[0]
        z, lk, later = scores(j, None)
        out, c = weighted(j, z, lk, later, carry_ref[...], None)
        acc_ref[...] += out
        carry_ref[...] = c
        return j - 1, alive()

    lax.while_loop(cond, body, (i - 2, alive()))
    y_ref[0] = acc_ref[...].astype(y_ref.dtype)


def _sb_call(q, k, v):
    bsz, s, w = q.shape
    heads = w // SB_HEAD_DIM
    T = min(SB_BLOCK, s)
    return pl.pallas_call(
        functools.partial(_sb_kernel, blk=T, heads=heads),
        grid=(bsz, s // T),
        in_specs=[pl.BlockSpec((1, T, w), lambda b, i: (b, i, 0)),
                  pl.BlockSpec((1, s, w), lambda b, i: (b, 0, 0)),
                  pl.BlockSpec((1, s, w), lambda b, i: (b, 0, 0))],
        out_specs=pl.BlockSpec((1, T, w), lambda b, i: (b, i, 0)),
        out_shape=jax.ShapeDtypeStruct((bsz, s, w), BF16),
        scratch_shapes=[pltpu.VMEM((heads * T, w), BF16),
                        pltpu.VMEM((T, w), F32),
                        pltpu.VMEM((heads * T, 1), F32)],
        compiler_params=_cparams(("arbitrary", "arbitrary")),
        name="stick_breaking",
    )(q, k, v)


def _out_kernel(ys_ref, yp_ref, yb_ref, x_ref, g1_ref, sh_ref, sc_ref, ng_ref, wo_ref, wq_ref,
                keys_ref, x1_ref, h2_ref, sct_ref, *, cuts):
    mix = jnp.zeros(x_ref.shape[1:], F32)
    for y_ref, (a, b) in zip((ys_ref, yp_ref, yb_ref), cuts):
        mix = mix + jnp.dot(y_ref[0], wo_ref[a:b, :], preferred_element_type=F32)
    x1 = x_ref[0] + g1_ref[0] * mix
    x1_ref[0] = x1
    h2 = _norm_mod(x1, ng_ref[...], sh_ref[0], sc_ref[0]).astype(BF16)
    h2_ref[0] = h2
    qb = jnp.dot(h2, wq_ref[...], preferred_element_type=F32).astype(BF16)
    tp = sct_ref.shape[-1]
    for lst in range(keys_ref.shape[0]):
        sc_t = _nt_dot(keys_ref[lst], qb[:, lst * PEER_HALF:(lst + 1) * PEER_HALF])
        for k in range(sct_ref.shape[0]):
            sct_ref[k, lst] = sc_t[:, k * tp:(k + 1) * tp]


def _out_call(ys, yp, yb, x, g1, sh, sc, ng, w_out, wq, keys):
    bsz, s, d = x.shape
    tm = min(TM_IN, s)
    nlist = keys.shape[0]
    cuts, a = [], 0
    for y in (ys, yp, yb):
        cuts.append((a, a + y.shape[-1]))
        a += y.shape[-1]
    tok = lambda w: pl.BlockSpec((1, tm, w), lambda b, i: (b, i, 0))
    vec = pl.BlockSpec((1, 1, d), lambda b, i: (b, 0, 0))
    full2 = lambda arr: pl.BlockSpec(arr.shape, lambda b, i: (0,) * arr.ndim)
    nblk = s // tm
    tp = min(TB_PEER, s)
    per = tm // tp
    return pl.pallas_call(
        functools.partial(_out_kernel, cuts=tuple(cuts)),
        grid=(bsz, nblk),
        in_specs=[tok(ys.shape[-1]), tok(yp.shape[-1]), tok(yb.shape[-1]), tok(d), vec, vec, vec,
                  pl.BlockSpec((1, d), lambda b, i: (0, 0)), full2(w_out), full2(wq), full2(keys)],
        out_specs=[tok(d), tok(d),
                   pl.BlockSpec((per, nlist, N_KEYS, tp), lambda b, i: (b * nblk + i, 0, 0, 0))],
        out_shape=[jax.ShapeDtypeStruct((bsz, s, d), F32),
                   jax.ShapeDtypeStruct((bsz, s, d), BF16),
                   jax.ShapeDtypeStruct((bsz * s // tp, nlist, N_KEYS, tp), F32)],
        compiler_params=_cparams(("arbitrary", "arbitrary")),
        name="out_proj_peer_query",
    )(ys, yp, yb, x, g1, sh, sc, ng, w_out, wq, keys)


def _top_rows(s, k, payload=None):
    nrows = s.shape[0]
    rows = lax.broadcasted_iota(jnp.int32, s.shape, 0).astype(F32)
    vals, picks = [], []
    for _ in range(k):
        m = jnp.max(s, axis=0, keepdims=True)
        idx = jnp.min(jnp.where(s == m, rows, float(nrows)), axis=0, keepdims=True)
        hit = rows == idx
        vals.append(m)
        if payload is None:
            picks.append(idx)
        else:
            picks.append(jnp.sum(jnp.where(hit, payload, 0.0), axis=0, keepdims=True))
        s = jnp.where(hit, -jnp.inf, s)
    return jnp.concatenate(vals, axis=0), jnp.concatenate(picks, axis=0)


def _pair_candidates(v1, x1, v2, x2):
    K = PEER_TOPK
    cands, ids = [], []
    a = 0
    while K // (a + 1) > 1:
        n = K // (a + 1)
        npad = -(-n // SUBLANES) * SUBLANES
        c = v1[a:a + 1, :] + v2[0:npad, :]
        if n < npad:
            c = jnp.where(lax.broadcasted_iota(jnp.int32, c.shape, 0) < n, c, -jnp.inf)
        cands.append(c)
        ids.append(x1[a:a + 1, :] * float(N_KEYS) + x2[0:npad, :])
        a += 1
    cands.append(v1[a:K, :] + v2[0:1, :])
    ids.append(x1[a:K, :] * float(N_KEYS) + x2[0:1, :])
    return jnp.concatenate(cands, axis=0), jnp.concatenate(ids, axis=0)


def _peer_kernel(sct_ref, h_ref, u_ref, v_ref, x_ref, g2_ref, fg_ref, o_ref,
                 pk_ref, pt_ref, scr_ref, g0_ref, g1_ref, coef_ref, acc_ref, *, final):
    s = pl.program_id(0)
    j = pl.program_id(1)
    K = PEER_TOPK
    eb = u_ref.shape[0]

    @pl.when(jnp.logical_and(s == 0, j == 0))
    def _():
        pk_ref[...] = jnp.zeros_like(pk_ref)
        scr_ref[...] = jnp.zeros_like(scr_ref)
        g0_ref[...] = jnp.zeros_like(g0_ref)
        g1_ref[...] = jnp.zeros_like(g1_ref)
        coef_ref[...] = jnp.zeros_like(coef_ref)
        acc_ref[...] = jnp.zeros_like(acc_ref)

    @pl.when(j == 0)
    def _():
        for k in range(pk_ref.shape[0]):
            pt_ref[k] = pk_ref[k].T

    def stages(gr_ref, gw_ref):
        v1, x1 = _top_rows(sct_ref[0], K)
        v2, x2 = _top_rows(sct_ref[1], K)
        cand, ids = _pair_candidates(v1, x1, v2, x2)
        best, expert = _top_rows(cand, K, payload=ids)
        e = jnp.exp(best - best[0:1, :])
        gate = e / jnp.sum(e, axis=0, keepdims=True)
        i1 = jnp.floor(expert * (1.0 / N_KEYS))
        r0 = pl.multiple_of(j * K, K)
        pk_ref[0, pl.ds(r0, K), :] = i1
        pk_ref[1, pl.ds(r0, K), :] = expert - i1 * float(N_KEYS)
        pk_ref[2, pl.ds(r0, K), :] = 0.5 * gate

        npair = pt_ref.shape[-1]
        sub = lax.broadcasted_iota(jnp.int32, (N_KEYS, npair), 0).astype(F32)
        t0 = pl.multiple_of(j * GATE_GROUP, GATE_GROUP)
        for tl in range(GATE_GROUP):
            r = pl.ds(t0 + tl, 1)
            lhs = jnp.where(sub == pt_ref[0, r, :], pt_ref[2, r, :], 0.0).astype(BF16)
            rhs = jnp.where(sub == pt_ref[1, r, :], 1.0, 0.0).astype(BF16)
            scr_ref[pl.ds(tl, N_KEYS, stride=GATE_PITCH), :] = _nt_dot(lhs, rhs)
        for a in range(N_KEYS):
            gw_ref[a, pl.ds(t0, GATE_GROUP), :] = (
                scr_ref[a * GATE_PITCH:a * GATE_PITCH + GATE_GROUP, :].astype(gw_ref.dtype))

        h = h_ref[...]
        d = h.shape[-1]
        acc_ref[...] += jnp.dot(coef_ref[...], v_ref[:, :d], preferred_element_type=F32)

        a0 = j * (eb // N_KEYS)
        pieces = []
        for q in range(eb // DENSE_SUB):
            act = _nt_dot(h, u_ref[q * DENSE_SUB:(q + 1) * DENSE_SUB, :d])
            w = (act * (1.0 + lax.erf(act * (2.0 ** -0.5)))).astype(BF16)
            for a in range(DENSE_SUB // N_KEYS):
                ga = gr_ref[a0 + q * (DENSE_SUB // N_KEYS) + a]
                pieces.append(w[:, a * N_KEYS:(a + 1) * N_KEYS] * ga)
        coef_ref[...] = jnp.concatenate(pieces, axis=1)

    @pl.when(s % 2 == 0)
    def _():
        stages(g0_ref, g1_ref)

    @pl.when(s % 2 == 1)
    def _():
        stages(g1_ref, g0_ref)

    @pl.when(j == 0)
    def _():
        x2 = x_ref[...] + g2_ref[0] * acc_ref[...]
        if final:
            ms = jnp.mean(x2 * x2, axis=-1, keepdims=True)
            x2 = x2 * lax.rsqrt(ms + EPS) * fg_ref[...]
        o_ref[...] = x2
        acc_ref[...] = jnp.zeros_like(acc_ref)


def _peer_call(sct, h2, u, v, x1, g2, fg, final):
    bsz, s, d = x1.shape
    t = bsz * s
    ne = u.shape[0]
    _, nlist, nk, _ = sct.shape
    tb = min(TB_PEER, s)
    eb = ne // PEER_HEADS
    npair = PEER_HEADS * PEER_TOPK
    assert tb == PEER_HEADS * GATE_GROUP and nlist == 2 * PEER_HEADS and eb % MXU_WIDTH == 0
    nt = t // tb
    per_b = s // tb
    nchunk = PEER_HEADS
    dense_tile = lambda i: jnp.clip(i - 2, 0, nt - 1)
    row = lambda i, j: (dense_tile(i), 0)
    out_tile = lambda i, j: jnp.clip(i - 2 - (j == 0).astype(jnp.int32), 0, nt - 1)
    out_row = lambda i, j: (out_tile(i, j), 0)
    dp = d + LANES if (d // LANES) % 2 == 0 else d
    u = jnp.pad(u, ((0, 0), (0, dp - d)))
    v = jnp.pad(v, ((0, 0), (0, dp - d)))
    out = pl.pallas_call(
        functools.partial(_peer_kernel, final=final),
        grid=(nt + 3, nchunk),
        in_specs=[pl.BlockSpec((None, 2, nk, tb), lambda i, j: (jnp.minimum(i, nt - 1), j, 0, 0)),
                  pl.BlockSpec((tb, d), row),
                  pl.BlockSpec((eb, dp), lambda i, j: (j, 0)),
                  pl.BlockSpec((eb, dp), lambda i, j: ((j + nchunk - 1) % nchunk, 0)),
                  pl.BlockSpec((tb, d), out_row),
                  pl.BlockSpec((1, 1, d), lambda i, j: (out_tile(i, j) // per_b, 0, 0)),
                  pl.BlockSpec((1, d), lambda i, j: (0, 0))],
        out_specs=pl.BlockSpec((tb, d), out_row),
        out_shape=jax.ShapeDtypeStruct((t, d), F32),
        scratch_shapes=[pltpu.VMEM((3, npair, tb), F32),
                        pltpu.VMEM((3, tb, npair), F32),
                        pltpu.VMEM((N_KEYS * GATE_PITCH, N_KEYS), F32),
                        pltpu.VMEM((N_KEYS, tb, N_KEYS), BF16),
                        pltpu.VMEM((N_KEYS, tb, N_KEYS), BF16),
                        pltpu.VMEM((tb, eb), BF16),
                        pltpu.VMEM((tb, d), F32)],
        compiler_params=_cparams(("arbitrary", "arbitrary")),
        name="peer_fused",
    )(sct, h2.reshape(t, d), u, v, x1.reshape(t, d), g2, fg)
    return out.reshape(bsz, s, d)


def kernel(x, c, ada_w, ada_b, mix_norm_g, ffn_norm_g, w_in, conv_w, conv_b, dt_bias, a_log, d_skip,
           ssd_norm_g, pool_w, pool_scale, w_out, peer_wq, peer_keys, peer_u, peer_v, final_norm_g):
    bsz, s, d = x.shape
    depth = ada_w.shape[0]
    ssd_w = ssd_norm_g.shape[-1]
    heads = dt_bias.shape[-1]
    xbc_w = conv_w.shape[-1]
    pool_wd = pool_scale.shape[-1]
    sb_w = (w_in.shape[-1] - ssd_w - xbc_w - heads - pool_wd) // 3
    o_z, o_xbc, o_dt, o_xp = 0, ssd_w, ssd_w + xbc_w, ssd_w + xbc_w + heads
    o_q = o_xp + pool_wd
    widths = (ssd_w, xbc_w, pool_wd, sb_w, sb_w, sb_w, LANES)

    mods = _mod_call(c, ada_w, ada_b)
    for l in range(depth):
        mod = [mods[l, :, k * d:(k + 1) * d].reshape(bsz, 1, d) for k in range(6)]
        sh1, sc1, g1, sh2, sc2, g2 = mod
        wl = w_in[l]
        w_cat = jnp.concatenate(
            [wl[:, o_z:o_xbc], wl[:, o_xbc:o_dt], wl[:, o_xp:o_q], wl[:, o_q:],
             jnp.pad(wl[:, o_dt:o_xp], ((0, 0), (0, LANES - heads)))], axis=1).astype(BF16)
        z, xbc, xp, q, k, v, dt = _in_call(x, sh1, sc1, mix_norm_g[l].reshape(1, d), w_cat, widths)
        y_ssd = _ssd_call(xbc, dt, z, conv_w[l], conv_b[l], dt_bias[l], a_log[l], d_skip[l],
                          ssd_norm_g[l])
        y_pool = _pool_call(xp, pool_w[l], pool_scale[l])
        y_sb = _sb_call(q, k, v)
        keys = peer_keys[l].reshape(-1, N_KEYS, PEER_HALF).astype(BF16)
        x1, h2, sct = _out_call(y_ssd, y_pool, y_sb, x, g1, sh2, sc2, ffn_norm_g[l].reshape(1, d),
                                w_out[l].astype(BF16), peer_wq[l].astype(BF16), keys)
        x = _peer_call(sct, h2, peer_u[l].astype(BF16), peer_v[l].astype(BF16), x1, g2,
                       final_norm_g.reshape(1, d), final=(l == depth - 1))
    return x
```

```python
import functools

import jax
import jax.numpy as jnp
from jax import lax
from jax.experimental import pallas as pl
from jax.experimental.pallas import tpu as pltpu

F32 = jnp.float32
BF16 = jnp.bfloat16
EPS = 1e-6
HIGHEST = lax.Precision.HIGHEST

SSD_HEAD_DIM = 64
SSD_GROUPS = 2
SSD_STATE = 64
CONV_WIDTH = 4
POOL_WINDOWS = (2, 4, 8, 16)
SB_HEAD_DIM = 64
PEER_HEADS = 8
PEER_TOPK = 16
N_KEYS = 128
PEER_HALF = 128

LANES = 128
SUBLANES = 8
MXU_WIDTH = 256
VMEM_LIMIT = 56 * 1024 * 1024

TM_IN = 512
SSD_CHUNK = 256
HALO = 16
TM_POOL = 512
SB_BLOCK = 128
SB_SKIP = 40.0
GATE_GROUP = 32
GATE_PITCH = 36
TB_PEER = 256
DENSE_SUB = 256


def _cparams(sem):
    return pltpu.CompilerParams(dimension_semantics=sem, vmem_limit_bytes=VMEM_LIMIT)


def _nt_dot(a, b):
    return lax.dot_general(a, b, (((1,), (1,)), ((), ())), preferred_element_type=F32)


def _softplus(x):
    return jnp.maximum(x, 0.0) + jnp.log(1.0 + jnp.exp(-jnp.abs(x)))


def _silu(x):
    return x * jax.nn.sigmoid(x)


def _mod_kernel(c_ref, w_ref, b_ref, o_ref):
    cond = _silu(c_ref[...])
    o_ref[0] = jnp.dot(cond, w_ref[0], preferred_element_type=F32, precision=HIGHEST) + b_ref[0]


def _mod_call(c, ada_w, ada_b):
    nl, d, n6 = ada_w.shape
    bsz = c.shape[0]
    tn = 1536
    return pl.pallas_call(
        _mod_kernel,
        grid=(nl, n6 // tn),
        in_specs=[pl.BlockSpec((bsz, d), lambda l, j: (0, 0)),
                  pl.BlockSpec((1, d, tn), lambda l, j: (l, 0, j)),
                  pl.BlockSpec((1, 1, tn), lambda l, j: (l, 0, j))],
        out_specs=pl.BlockSpec((1, bsz, tn), lambda l, j: (l, 0, j)),
        out_shape=jax.ShapeDtypeStruct((nl, bsz, n6), F32),
        compiler_params=_cparams(("arbitrary", "arbitrary")),
        name="adaln_mod",
    )(c, ada_w, ada_b.reshape(nl, 1, n6))


def _norm_mod(x, g, sh, sc):
    ms = jnp.mean(x * x, axis=-1, keepdims=True)
    y = x * lax.rsqrt(ms + EPS) * g
    return y * (1.0 + sc) + sh


def _in_kernel(x_ref, sh_ref, sc_ref, g_ref, w_ref, z_ref, xbc_ref, xp_ref, q_ref, k_ref, v_ref,
               dt_ref, *, cuts):
    h = _norm_mod(x_ref[0], g_ref[...], sh_ref[0], sc_ref[0]).astype(BF16)
    outs = (z_ref, xbc_ref, xp_ref, q_ref, k_ref, v_ref, dt_ref)
    for o_ref, (a, b) in zip(outs, cuts):
        o_ref[0] = jnp.dot(h, w_ref[:, a:b], preferred_element_type=F32).astype(o_ref.dtype)


def _in_call(x, sh, sc, g, w_cat, widths):
    bsz, s, d = x.shape
    tm = min(TM_IN, s)
    cuts, a = [], 0
    for w in widths:
        cuts.append((a, a + w))
        a += w
    dtypes = (BF16, BF16, BF16, BF16, BF16, BF16, F32)
    tok = lambda w: pl.BlockSpec((1, tm, w), lambda b, i: (b, i, 0))
    vec = pl.BlockSpec((1, 1, d), lambda b, i: (b, 0, 0))
    return pl.pallas_call(
        functools.partial(_in_kernel, cuts=tuple(cuts)),
        grid=(bsz, s // tm),
        in_specs=[tok(d), vec, vec,
                  pl.BlockSpec((1, d), lambda b, i: (0, 0)),
                  pl.BlockSpec(w_cat.shape, lambda b, i: (0, 0))],
        out_specs=[tok(w) for w in widths],
        out_shape=[jax.ShapeDtypeStruct((bsz, s, w), dt) for w, dt in zip(widths, dtypes)],
        compiler_params=_cparams(("arbitrary", "arbitrary")),
        name="in_proj",
    )(x, sh, sc, g, w_cat)


def _ssd_kernel(xbc_ref, halo_ref, dt_ref, z_ref, cw_ref, cb_ref, dtb_ref, alog_ref, expand_ref,
                dsk_ref, ng_ref, y_ref, state_ref, *, chunk, width, gn):
    i = pl.program_id(1)
    L = chunk
    heads = width // SSD_HEAD_DIM
    hpg = heads // SSD_GROUPS
    half = width // SSD_GROUPS

    @pl.when(i == 0)
    def _():
        state_ref[...] = jnp.zeros_like(state_ref)

    hal = jnp.where(i > 0, halo_ref[0].astype(F32), 0.0)
    full = jnp.concatenate([hal, xbc_ref[0].astype(F32)], axis=0)
    cw = cw_ref[...]
    acc = full * cw[CONV_WIDTH - 1:CONV_WIDTH]
    for j in range(1, CONV_WIDTH):
        acc = acc + pltpu.roll(full, j, 0) * cw[CONV_WIDTH - 1 - j:CONV_WIDTH - j]
    xbc = _silu(acc[HALO:] + cb_ref[...])
    xs = xbc[:, :width]
    bm = xbc[:, width:width + gn]
    cm = xbc[:, width + gn:width + 2 * gn]

    dt = _softplus(dt_ref[0] + dtb_ref[...])
    a = dt * (-jnp.exp(alog_ref[...]))
    row_i = lax.broadcasted_iota(jnp.int32, (L, L), 0)
    col_i = lax.broadcasted_iota(jnp.int32, (L, L), 1)
    tril = row_i >= col_i
    a_cs = jnp.dot(tril.astype(F32), a, preferred_element_type=F32, precision=HIGHEST)
    a_cs_t = a_cs.T
    expand = expand_ref[...]
    dt_x = jnp.dot(dt, expand, preferred_element_type=F32, precision=HIGHEST)
    acs_x = jnp.dot(a_cs, expand, preferred_element_type=F32, precision=HIGHEST)
    alast_x = acs_x[L - 1:L, :]
    xdt = xs * dt_x

    bm_b = bm.astype(BF16)
    cm_b = cm.astype(BF16)
    lane_g = lax.broadcasted_iota(jnp.int32, (L, gn), 1) // SSD_STATE
    lane_h = lax.broadcasted_iota(jnp.int32, (L, half), 1) // SSD_HEAD_DIM

    state = state_ref[...]
    y_off = jnp.dot(cm_b, state.astype(BF16), preferred_element_type=F32) * jnp.exp(acs_x)

    y_halves = []
    for g in range(SSD_GROUPS):
        cb = _nt_dot(jnp.where(lane_g == g, cm_b, jnp.zeros_like(cm_b)), bm_b)
        xdt_g = xdt[:, g * half:(g + 1) * half]
        yh = jnp.zeros((L, half), F32)
        for hh in range(hpg):
            h = g * hpg + hh
            seg = a_cs[:, h:h + 1] - a_cs_t[h:h + 1, :]
            dec = jnp.exp(jnp.where(tril, seg, -1e30))
            m = (cb * dec).astype(BF16)
            rhs = jnp.where(lane_h == hh, xdt_g, 0.0).astype(BF16)
            yh = yh + jnp.dot(m, rhs, preferred_element_type=F32)
        y_halves.append(yh)

    ds_x = jnp.exp(alast_x - acs_x)
    upd = jnp.dot(bm.T.astype(BF16), (xdt * ds_x).astype(BF16), preferred_element_type=F32)
    srow = lax.broadcasted_iota(jnp.int32, (gn, width), 0) // SSD_STATE
    scol = lax.broadcasted_iota(jnp.int32, (gn, width), 1) // half
    state_ref[...] = jnp.exp(alast_x) * state + jnp.where(srow == scol, upd, 0.0)

    zg = _silu(z_ref[0].astype(F32))
    for g in range(SSD_GROUPS):
        sl = slice(g * half, (g + 1) * half)
        yg = (y_halves[g] + y_off[:, sl] + dsk_ref[:, sl] * xs[:, sl]) * zg[:, sl]
        ms = jnp.mean(yg * yg, axis=-1, keepdims=True)
        y_ref[0, :, sl] = (yg * lax.rsqrt(ms + EPS) * ng_ref[:, sl]).astype(y_ref.dtype)


def _ssd_call(xbc, dt, z, conv_w, conv_b, dt_bias, a_log, d_skip, ssd_norm_g):
    bsz, s, c = xbc.shape
    width = z.shape[-1]
    gn = SSD_GROUPS * SSD_STATE
    heads = width // SSD_HEAD_DIM
    L = min(SSD_CHUNK, s)
    pad = lambda v: jnp.zeros((1, LANES), F32).at[0, :heads].set(v)
    expand = (jnp.arange(LANES)[:, None] == (jnp.arange(width)[None, :] // SSD_HEAD_DIM)).astype(F32)
    dsk = jnp.repeat(d_skip, SSD_HEAD_DIM).reshape(1, width)
    hb = L // HALO
    const = lambda shp: pl.BlockSpec(shp, lambda b, i: (0, 0))
    return pl.pallas_call(
        functools.partial(_ssd_kernel, chunk=L, width=width, gn=gn),
        grid=(bsz, s // L),
        in_specs=[pl.BlockSpec((1, L, c), lambda b, i: (b, i, 0)),
                  pl.BlockSpec((1, HALO, c), lambda b, i: (b, jnp.maximum(i * hb - 1, 0), 0)),
                  pl.BlockSpec((1, L, LANES), lambda b, i: (b, i, 0)),
                  pl.BlockSpec((1, L, width), lambda b, i: (b, i, 0)),
                  const((CONV_WIDTH, c)), const((1, c)), const((1, LANES)), const((1, LANES)),
                  const((LANES, width)), const((1, width)), const((1, width))],
        out_specs=pl.BlockSpec((1, L, width), lambda b, i: (b, i, 0)),
        out_shape=jax.ShapeDtypeStruct((bsz, s, width), BF16),
        scratch_shapes=[pltpu.VMEM((gn, width), F32)],
        compiler_params=_cparams(("arbitrary", "arbitrary")),
        name="ssd_mixer",
    )(xbc, xbc, dt, z, conv_w, conv_b.reshape(1, c), pad(dt_bias), pad(a_log), expand, dsk,
      ssd_norm_g.reshape(1, width))


def _pool_kernel(xp_ref, halo_ref, w_ref, sc_ref, y_ref, *, tm, gdim):
    i = pl.program_id(1)
    hal = jnp.where(i > 0, halo_ref[0].astype(F32), 0.0)
    x = xp_ref[0].astype(F32)
    full = jnp.concatenate([hal, x], axis=0)
    lane_g = lax.broadcasted_iota(jnp.int32, x.shape, 1) // gdim
    tpos = i * tm + lax.broadcasted_iota(jnp.int32, x.shape, 0)
    win_sum = jnp.zeros_like(x)
    cnt = jnp.ones_like(x)
    s = full
    span = 1
    for gi, w in enumerate(POOL_WINDOWS):
        while span < w:
            s = s + pltpu.roll(s, span, 0)
            span *= 2
        win_sum = jnp.where(lane_g == gi, s[HALO:], win_sum)
        cnt = jnp.where(lane_g == gi, jnp.minimum(tpos + 1, w).astype(F32), cnt)
    pooled = (win_sum / cnt - x).astype(BF16)
    y = jnp.dot(pooled, w_ref[...], preferred_element_type=F32) * sc_ref[...]
    y_ref[0] = y.astype(y_ref.dtype)


def _pool_call(xp, pool_w, pool_scale):
    bsz, s, w = xp.shape
    ng, gdim, _ = pool_w.shape
    assert POOL_WINDOWS[-1] <= HALO and all(b == 2 * a for a, b in zip(POOL_WINDOWS, POOL_WINDOWS[1:]))
    tm = min(TM_POOL, s)
    wbd = jnp.zeros((w, w), F32)
    for g in range(ng):
        wbd = wbd.at[g * gdim:(g + 1) * gdim, g * gdim:(g + 1) * gdim].set(pool_w[g])
    hb = tm // HALO
    return pl.pallas_call(
        functools.partial(_pool_kernel, tm=tm, gdim=gdim),
        grid=(bsz, s // tm),
        in_specs=[pl.BlockSpec((1, tm, w), lambda b, i: (b, i, 0)),
                  pl.BlockSpec((1, HALO, w), lambda b, i: (b, jnp.maximum(i * hb - 1, 0), 0)),
                  pl.BlockSpec((w, w), lambda b, i: (0, 0)),
                  pl.BlockSpec((1, w), lambda b, i: (0, 0))],
        out_specs=pl.BlockSpec((1, tm, w), lambda b, i: (b, i, 0)),
        out_shape=jax.ShapeDtypeStruct((bsz, s, w), BF16),
        compiler_params=_cparams(("arbitrary", "arbitrary")),
        name="pool_mixer",
    )(xp, xp, wbd.astype(BF16), pool_scale.reshape(1, w))


def _sb_kernel(q_ref, k_ref, v_ref, y_ref, qm_ref, acc_ref, carry_ref, *, blk, heads):
    i = pl.program_id(1)
    T = blk
    w = q_ref.shape[-1]
    lane_h = lax.broadcasted_iota(jnp.int32, (T, w), 1) // SB_HEAD_DIM
    q = q_ref[0] * (SB_HEAD_DIM ** -0.5)
    for h in range(heads):
        qm_ref[h * T:(h + 1) * T, :] = jnp.where(lane_h == h, q, jnp.zeros_like(q))
    row_i = lax.broadcasted_iota(jnp.int32, (T, T), 0)
    col_i = lax.broadcasted_iota(jnp.int32, (T, T), 1)
    suffix = (row_i > col_i).astype(BF16)
    qrow = lax.broadcasted_iota(jnp.int32, (heads * T, T), 0) % T
    causal = lax.broadcasted_iota(jnp.int32, (heads * T, T), 1) < qrow

    def scores(j, mask):
        ks = k_ref[0, pl.ds(pl.multiple_of(j * T, T), T), :]
        z = _nt_dot(qm_ref[...], ks)
        lk = -_softplus(z)
        if mask is not None:
            lk = jnp.where(mask, lk, 0.0)
        hi = lk.astype(BF16)
        lo = (lk - hi.astype(F32)).astype(BF16)
        later = (jnp.dot(hi, suffix, preferred_element_type=F32)
                 + jnp.dot(lo, suffix, preferred_element_type=F32))
        return z, lk, later

    def weighted(j, z, lk, later, c, mask):
        vs = v_ref[0, pl.ds(pl.multiple_of(j * T, T), T), :]
        wgt = jnp.exp(z + lk + later + c)
        if mask is not None:
            wgt = jnp.where(mask, wgt, 0.0)
        pv = jnp.dot(wgt.astype(BF16), vs, preferred_element_type=F32)
        out = pv[0:T]
        for h in range(1, heads):
            out = jnp.where(lane_h == h, pv[h * T:(h + 1) * T], out)
        return out, c + jnp.sum(lk, axis=1, keepdims=True)

    def alive():
        return (jnp.max(carry_ref[...]) > -SB_SKIP).astype(jnp.int32)

    prev = jnp.maximum(i - 1, 0)
    has_prev = jnp.broadcast_to(i > 0, causal.shape)
    zd, lkd, ltd = scores(i, causal)
    zp, lkp, ltp = scores(prev, has_prev)
    out_d, c = weighted(i, zd, lkd, ltd, jnp.zeros((heads * T, 1), F32), causal)
    out_p, c = weighted(prev, zp, lkp, ltp, c, has_prev)
    acc_ref[...] = out_d + out_p
    carry_ref[...] = c

    def cond(st):
        return jnp.logical_and(st[0] >= 0, st[1] > 0)

    def body(st):
        j = st[0]
        z, lk, later = scores(j, None)
        out, c = weighted(j, z, lk, later, carry_ref[...], None)
        acc_ref[...] += out
        carry_ref[...] = c
        return j - 1, alive()

    lax.while_loop(cond, body, (i - 2, alive()))
    y_ref[0] = acc_ref[...].astype(y_ref.dtype)


def _sb_call(q, k, v):
    bsz, s, w = q.shape
    heads = w // SB_HEAD_DIM
    T = min(SB_BLOCK, s)
    return pl.pallas_call(
        functools.partial(_sb_kernel, blk=T, heads=heads),
        grid=(bsz, s // T),
        in_specs=[pl.BlockSpec((1, T, w), lambda b, i: (b, i, 0)),
                  pl.BlockSpec((1, s, w), lambda b, i: (b, 0, 0)),
                  pl.BlockSpec((1, s, w), lambda b, i: (b, 0, 0))],
        out_specs=pl.BlockSpec((1, T, w), lambda b, i: (b, i, 0)),
        out_shape=jax.ShapeDtypeStruct((bsz, s, w), BF16),
        scratch_shapes=[pltpu.VMEM((heads * T, w), BF16),
                        pltpu.VMEM((T, w), F32),
                        pltpu.VMEM((heads * T, 1), F32)],
        compiler_params=_cparams(("arbitrary", "arbitrary")),
        name="stick_breaking",
    )(q, k, v)


def _out_kernel(ys_ref, yp_ref, yb_ref, x_ref, g1_ref, sh_ref, sc_ref, ng_ref, wo_ref, wq_ref,
                keys_ref, x1_ref, h2_ref, sct_ref, *, cuts):
    mix = jnp.zeros(x_ref.shape[1:], F32)
    for y_ref, (a, b) in zip((ys_ref, yp_ref, yb_ref), cuts):
        mix = mix + jnp.dot(y_ref[0], wo_ref[a:b, :], preferred_element_type=F32)
    x1 = x_ref[0] + g1_ref[0] * mix
    x1_ref[0] = x1
    h2 = _norm_mod(x1, ng_ref[...], sh_ref[0], sc_ref[0]).astype(BF16)
    h2_ref[0] = h2
    qb = jnp.dot(h2, wq_ref[...], preferred_element_type=F32).astype(BF16)
    tp = sct_ref.shape[-1]
    for lst in range(keys_ref.shape[0]):
        sc_t = _nt_dot(keys_ref[lst], qb[:, lst * PEER_HALF:(lst + 1) * PEER_HALF])
        for k in range(sct_ref.shape[0]):
            sct_ref[k, lst] = sc_t[:, k * tp:(k + 1) * tp]


def _out_call(ys, yp, yb, x, g1, sh, sc, ng, w_out, wq, keys):
    bsz, s, d = x.shape
    tm = min(TM_IN, s)
    nlist = keys.shape[0]
    cuts, a = [], 0
    for y in (ys, yp, yb):
        cuts.append((a, a + y.shape[-1]))
        a += y.shape[-1]
    tok = lambda w: pl.BlockSpec((1, tm, w), lambda b, i: (b, i, 0))
    vec = pl.BlockSpec((1, 1, d), lambda b, i: (b, 0, 0))
    full2 = lambda arr: pl.BlockSpec(arr.shape, lambda b, i: (0,) * arr.ndim)
    nblk = s // tm
    tp = min(TB_PEER, s)
    per = tm // tp
    return pl.pallas_call(
        functools.partial(_out_kernel, cuts=tuple(cuts)),
        grid=(bsz, nblk),
        in_specs=[tok(ys.shape[-1]), tok(yp.shape[-1]), tok(yb.shape[-1]), tok(d), vec, vec, vec,
                  pl.BlockSpec((1, d), lambda b, i: (0, 0)), full2(w_out), full2(wq), full2(keys)],
        out_specs=[tok(d), tok(d),
                   pl.BlockSpec((per, nlist, N_KEYS, tp), lambda b, i: (b * nblk + i, 0, 0, 0))],
        out_shape=[jax.ShapeDtypeStruct((bsz, s, d), F32),
                   jax.ShapeDtypeStruct((bsz, s, d), BF16),
                   jax.ShapeDtypeStruct((bsz * s // tp, nlist, N_KEYS, tp), F32)],
        compiler_params=_cparams(("arbitrary", "arbitrary")),
        name="out_proj_peer_query",
    )(ys, yp, yb, x, g1, sh, sc, ng, w_out, wq, keys)


def _top_rows(s, k, payload=None):
    nrows = s.shape[0]
    rows = lax.broadcasted_iota(jnp.int32, s.shape, 0).astype(F32)
    vals, picks = [], []
    for _ in range(k):
        m = jnp.max(s, axis=0, keepdims=True)
        idx = jnp.min(jnp.where(s == m, rows, float(nrows)), axis=0, keepdims=True)
        hit = rows == idx
        vals.append(m)
        if payload is None:
            picks.append(idx)
        else:
            picks.append(jnp.sum(jnp.where(hit, payload, 0.0), axis=0, keepdims=True))
        s = jnp.where(hit, -jnp.inf, s)
    return jnp.concatenate(vals, axis=0), jnp.concatenate(picks, axis=0)


def _pair_candidates(v1, x1, v2, x2):
    K = PEER_TOPK
    cands, ids = [], []
    a = 0
    while K // (a + 1) > 1:
        n = K // (a + 1)
        npad = -(-n // SUBLANES) * SUBLANES
        c = v1[a:a + 1, :] + v2[0:npad, :]
        if n < npad:
            c = jnp.where(lax.broadcasted_iota(jnp.int32, c.shape, 0) < n, c, -jnp.inf)
        cands.append(c)
        ids.append(x1[a:a + 1, :] * float(N_KEYS) + x2[0:npad, :])
        a += 1
    cands.append(v1[a:K, :] + v2[0:1, :])
    ids.append(x1[a:K, :] * float(N_KEYS) + x2[0:1, :])
    return jnp.concatenate(cands, axis=0), jnp.concatenate(ids, axis=0)


def _peer_kernel(sct_ref, h_ref, u_ref, v_ref, x_ref, g2_ref, fg_ref, o_ref,
                 pk_ref, pt_ref, scr_ref, g0_ref, g1_ref, coef_ref, acc_ref, *, final):
    s = pl.program_id(0)
    j = pl.program_id(1)
    K = PEER_TOPK
    eb = u_ref.shape[0]

    @pl.when(jnp.logical_and(s == 0, j == 0))
    def _():
        pk_ref[...] = jnp.zeros_like(pk_ref)
        scr_ref[...] = jnp.zeros_like(scr_ref)
        g0_ref[...] = jnp.zeros_like(g0_ref)
        g1_ref[...] = jnp.zeros_like(g1_ref)
        coef_ref[...] = jnp.zeros_like(coef_ref)
        acc_ref[...] = jnp.zeros_like(acc_ref)

    @pl.when(j == 0)
    def _():
        for k in range(pk_ref.shape[0]):
            pt_ref[k] = pk_ref[k].T

    def stages(gr_ref, gw_ref):
        v1, x1 = _top_rows(sct_ref[0], K)
        v2, x2 = _top_rows(sct_ref[1], K)
        cand, ids = _pair_candidates(v1, x1, v2, x2)
        best, expert = _top_rows(cand, K, payload=ids)
        e = jnp.exp(best - best[0:1, :])
        gate = e / jnp.sum(e, axis=0, keepdims=True)
        i1 = jnp.floor(expert * (1.0 / N_KEYS))
        r0 = pl.multiple_of(j * K, K)
        pk_ref[0, pl.ds(r0, K), :] = i1
        pk_ref[1, pl.ds(r0, K), :] = expert - i1 * float(N_KEYS)
        pk_ref[2, pl.ds(r0, K), :] = 0.5 * gate

        npair = pt_ref.shape[-1]
        sub = lax.broadcasted_iota(jnp.int32, (N_KEYS, npair), 0).astype(F32)
        t0 = pl.multiple_of(j * GATE_GROUP, GATE_GROUP)
        for tl in range(GATE_GROUP):
            r = pl.ds(t0 + tl, 1)
            lhs = jnp.where(sub == pt_ref[0, r, :], pt_ref[2, r, :], 0.0).astype(BF16)
            rhs = jnp.where(sub == pt_ref[1, r, :], 1.0, 0.0).astype(BF16)
            scr_ref[pl.ds(tl, N_KEYS, stride=GATE_PITCH), :] = _nt_dot(lhs, rhs)
        for a in range(N_KEYS):
            gw_ref[a, pl.ds(t0, GATE_GROUP), :] = (
                scr_ref[a * GATE_PITCH:a * GATE_PITCH + GATE_GROUP, :].astype(gw_ref.dtype))

        h = h_ref[...]
        d = h.shape[-1]
        acc_ref[...] += jnp.dot(coef_ref[...], v_ref[:, :d], preferred_element_type=F32)

        a0 = j * (eb // N_KEYS)
        pieces = []
        for q in range(eb // DENSE_SUB):
            act = _nt_dot(h, u_ref[q * DENSE_SUB:(q + 1) * DENSE_SUB, :d])
            w = (act * (1.0 + lax.erf(act * (2.0 ** -0.5)))).astype(BF16)
            for a in range(DENSE_SUB // N_KEYS):
                ga = gr_ref[a0 + q * (DENSE_SUB // N_KEYS) + a]
                pieces.append(w[:, a * N_KEYS:(a + 1) * N_KEYS] * ga)
        coef_ref[...] = jnp.concatenate(pieces, axis=1)

    @pl.when(s % 2 == 0)
    def _():
        stages(g0_ref, g1_ref)

    @pl.when(s % 2 == 1)
    def _():
        stages(g1_ref, g0_ref)

    @pl.when(j == 0)
    def _():
        x2 = x_ref[...] + g2_ref[0] * acc_ref[...]
        if final:
            ms = jnp.mean(x2 * x2, axis=-1, keepdims=True)
            x2 = x2 * lax.rsqrt(ms + EPS) * fg_ref[...]
        o_ref[...] = x2
        acc_ref[...] = jnp.zeros_like(acc_ref)


def _peer_call(sct, h2, u, v, x1, g2, fg, final):
    bsz, s, d = x1.shape
    t = bsz * s
    ne = u.shape[0]
    _, nlist, nk, _ = sct.shape
    tb = min(TB_PEER, s)
    eb = ne // PEER_HEADS
    npair = PEER_HEADS * PEER_TOPK
    assert tb == PEER_HEADS * GATE_GROUP and nlist == 2 * PEER_HEADS and eb % MXU_WIDTH == 0
    nt = t // tb
    per_b = s // tb
    nchunk = PEER_HEADS
    dense_tile = lambda i: jnp.clip(i - 2, 0, nt - 1)
    row = lambda i, j: (dense_tile(i), 0)
    out_tile = lambda i, j: jnp.clip(i - 2 - (j == 0).astype(jnp.int32), 0, nt - 1)
    out_row = lambda i, j: (out_tile(i, j), 0)
    dp = d
    out = pl.pallas_call(
        functools.partial(_peer_kernel, final=final),
        grid=(nt + 3, nchunk),
        in_specs=[pl.BlockSpec((None, 2, nk, tb), lambda i, j: (jnp.minimum(i, nt - 1), j, 0, 0)),
                  pl.BlockSpec((tb, d), row),
                  pl.BlockSpec((eb, dp), lambda i, j: (j, 0)),
                  pl.BlockSpec((eb, dp), lambda i, j: ((j + nchunk - 1) % nchunk, 0)),
                  pl.BlockSpec((tb, d), out_row),
                  pl.BlockSpec((1, 1, d), lambda i, j: (out_tile(i, j) // per_b, 0, 0)),
                  pl.BlockSpec((1, d), lambda i, j: (0, 0))],
        out_specs=pl.BlockSpec((tb, d), out_row),
        out_shape=jax.ShapeDtypeStruct((t, d), F32),
        scratch_shapes=[pltpu.VMEM((3, npair, tb), F32),
                        pltpu.VMEM((3, tb, npair), F32),
                        pltpu.VMEM((N_KEYS * GATE_PITCH, N_KEYS), F32),
                        pltpu.VMEM((N_KEYS, tb, N_KEYS), BF16),
                        pltpu.VMEM((N_KEYS, tb, N_KEYS), BF16),
                        pltpu.VMEM((tb, eb), BF16),
                        pltpu.VMEM((tb, d), F32)],
        compiler_params=_cparams(("arbitrary", "arbitrary")),
        name="peer_fused",
    )(sct, h2.reshape(t, d), u, v, x1.reshape(t, d), g2, fg)
    return out.reshape(bsz, s, d)


def kernel(x, c, ada_w, ada_b, mix_norm_g, ffn_norm_g, w_in, conv_w, conv_b, dt_bias, a_log, d_skip,
           ssd_norm_g, pool_w, pool_scale, w_out, peer_wq, peer_keys, peer_u, peer_v, final_norm_g):
    bsz, s, d = x.shape
    depth = ada_w.shape[0]
    ssd_w = ssd_norm_g.shape[-1]
    heads = dt_bias.shape[-1]
    xbc_w = conv_w.shape[-1]
    pool_wd = pool_scale.shape[-1]
    sb_w = (w_in.shape[-1] - ssd_w - xbc_w - heads - pool_wd) // 3
    o_z, o_xbc, o_dt, o_xp = 0, ssd_w, ssd_w + xbc_w, ssd_w + xbc_w + heads
    o_q = o_xp + pool_wd
    widths = (ssd_w, xbc_w, pool_wd, sb_w, sb_w, sb_w, LANES)

    mods = _mod_call(c, ada_w, ada_b)
    for l in range(depth):
        mod = [mods[l, :, k * d:(k + 1) * d].reshape(bsz, 1, d) for k in range(6)]
        sh1, sc1, g1, sh2, sc2, g2 = mod
        wl = w_in[l]
        w_cat = jnp.concatenate(
            [wl[:, o_z:o_xbc], wl[:, o_xbc:o_dt], wl[:, o_xp:o_q], wl[:, o_q:],
             jnp.pad(wl[:, o_dt:o_xp], ((0, 0), (0, LANES - heads)))], axis=1).astype(BF16)
        z, xbc, xp, q, k, v, dt = _in_call(x, sh1, sc1, mix_norm_g[l].reshape(1, d), w_cat, widths)
        y_ssd = _ssd_call(xbc, dt, z, conv_w[l], conv_b[l], dt_bias[l], a_log[l], d_skip[l],
                          ssd_norm_g[l])
        y_pool = _pool_call(xp, pool_w[l], pool_scale[l])
        y_sb = _sb_call(q, k, v)
        keys = peer_keys[l].reshape(-1, N_KEYS, PEER_HALF).astype(BF16)
        x1, h2, sct = _out_call(y_ssd, y_pool, y_sb, x, g1, sh2, sc2, ffn_norm_g[l].reshape(1, d),
                                w_out[l].astype(BF16), peer_wq[l].astype(BF16), keys)
        x = _peer_call(sct, h2, peer_u[l].astype(BF16), peer_v[l].astype(BF16), x1, g2,
                       final_norm_g.reshape(1, d), final=(l == depth - 1))
    return x
```

```python
import functools

import jax
import jax.numpy as jnp
from jax import lax
from jax.experimental import pallas as pl
from jax.experimental.pallas import tpu as pltpu

F32 = jnp.float32
BF16 = jnp.bfloat16
EPS = 1e-6
HIGHEST = lax.Precision.HIGHEST

SSD_HEAD_DIM = 64
SSD_GROUPS = 2
SSD_STATE = 64
CONV_WIDTH = 4
POOL_WINDOWS = (2, 4, 8, 16)
SB_HEAD_DIM = 64
PEER_HEADS = 8
PEER_TOPK = 16
N_KEYS = 128
PEER_HALF = 128

LANES = 128
SUBLANES = 8
MXU_WIDTH = 256
VMEM_LIMIT = 56 * 1024 * 1024

TM_IN = 512
SSD_CHUNK = 256
HALO = 16
TM_POOL = 512
SB_BLOCK = 128
SB_SKIP = 40.0
GATE_GROUP = 32
GATE_PITCH = 36
TB_PEER = 256
DENSE_SUB = 256


def _cparams(sem):
    return pltpu.CompilerParams(dimension_semantics=sem, vmem_limit_bytes=VMEM_LIMIT)


def _nt_dot(a, b):
    return lax.dot_general(a, b, (((1,), (1,)), ((), ())), preferred_element_type=F32)


def _softplus(x):
    return jnp.maximum(x, 0.0) + jnp.log(1.0 + jnp.exp(-jnp.abs(x)))


def _silu(x):
    return x * jax.nn.sigmoid(x)


def _mod_kernel(c_ref, w_ref, b_ref, o_ref):
    cond = _silu(c_ref[...])
    o_ref[0] = jnp.dot(cond, w_ref[0], preferred_element_type=F32, precision=HIGHEST) + b_ref[0]


def _mod_call(c, ada_w, ada_b):
    nl, d, n6 = ada_w.shape
    bsz = c.shape[0]
    tn = 1536
    return pl.pallas_call(
        _mod_kernel,
        grid=(nl, n6 // tn),
        in_specs=[pl.BlockSpec((bsz, d), lambda l, j: (0, 0)),
                  pl.BlockSpec((1, d, tn), lambda l, j: (l, 0, j)),
                  pl.BlockSpec((1, 1, tn), lambda l, j: (l, 0, j))],
        out_specs=pl.BlockSpec((1, bsz, tn), lambda l, j: (l, 0, j)),
        out_shape=jax.ShapeDtypeStruct((nl, bsz, n6), F32),
        compiler_params=_cparams(("arbitrary", "arbitrary")),
        name="adaln_mod",
    )(c, ada_w, ada_b.reshape(nl, 1, n6))


def _norm_mod(x, g, sh, sc):
    ms = jnp.mean(x * x, axis=-1, keepdims=True)
    y = x * lax.rsqrt(ms + EPS) * g
    return y * (1.0 + sc) + sh


def _in_kernel(x_ref, sh_ref, sc_ref, g_ref, w_ref, z_ref, xbc_ref, xp_ref, q_ref, k_ref, v_ref,
               dt_ref, *, cuts):
    h = _norm_mod(x_ref[0], g_ref[...], sh_ref[0], sc_ref[0]).astype(BF16)
    outs = (z_ref, xbc_ref, xp_ref, q_ref, k_ref, v_ref, dt_ref)
    for o_ref, (a, b) in zip(outs, cuts):
        o_ref[0] = jnp.dot(h, w_ref[:, a:b], preferred_element_type=F32).astype(o_ref.dtype)


def _in_call(x, sh, sc, g, w_cat, widths):
    bsz, s, d = x.shape
    tm = min(TM_IN, s)
    cuts, a = [], 0
    for w in widths:
        cuts.append((a, a + w))
        a += w
    dtypes = (BF16, BF16, BF16, BF16, BF16, BF16, F32)
    tok = lambda w: pl.BlockSpec((1, tm, w), lambda b, i: (b, i, 0))
    vec = pl.BlockSpec((1, 1, d), lambda b, i: (b, 0, 0))
    return pl.pallas_call(
        functools.partial(_in_kernel, cuts=tuple(cuts)),
        grid=(bsz, s // tm),
        in_specs=[tok(d), vec, vec,
                  pl.BlockSpec((1, d), lambda b, i: (0, 0)),
                  pl.BlockSpec(w_cat.shape, lambda b, i: (0, 0))],
        out_specs=[tok(w) for w in widths],
        out_shape=[jax.ShapeDtypeStruct((bsz, s, w), dt) for w, dt in zip(widths, dtypes)],
        compiler_params=_cparams(("arbitrary", "arbitrary")),
        name="in_proj",
    )(x, sh, sc, g, w_cat)


def _ssd_kernel(xbc_ref, halo_ref, dt_ref, z_ref, cw_ref, cb_ref, dtb_ref, alog_ref, expand_ref,
                dsk_ref, ng_ref, y_ref, state_ref, *, chunk, width, gn):
    i = pl.program_id(1)
    L = chunk
    heads = width // SSD_HEAD_DIM
    hpg = heads // SSD_GROUPS
    half = width // SSD_GROUPS

    @pl.when(i == 0)
    def _():
        state_ref[...] = jnp.zeros_like(state_ref)

    hal = jnp.where(i > 0, halo_ref[0].astype(F32), 0.0)
    full = jnp.concatenate([hal, xbc_ref[0].astype(F32)], axis=0)
    cw = cw_ref[...]
    acc = full * cw[CONV_WIDTH - 1:CONV_WIDTH]
    for j in range(1, CONV_WIDTH):
        acc = acc + pltpu.roll(full, j, 0) * cw[CONV_WIDTH - 1 - j:CONV_WIDTH - j]
    xbc = _silu(acc[HALO:] + cb_ref[...])
    xs = xbc[:, :width]
    bm = xbc[:, width:width + gn]
    cm = xbc[:, width + gn:width + 2 * gn]

    dt = _softplus(dt_ref[0] + dtb_ref[...])
    a = dt * (-jnp.exp(alog_ref[...]))
    row_i = lax.broadcasted_iota(jnp.int32, (L, L), 0)
    col_i = lax.broadcasted_iota(jnp.int32, (L, L), 1)
    tril = row_i >= col_i
    a_cs = jnp.dot(tril.astype(F32), a, preferred_element_type=F32, precision=HIGHEST)
    a_cs_t = a_cs.T
    expand = expand_ref[...]
    dt_x = jnp.dot(dt, expand, preferred_element_type=F32, precision=HIGHEST)
    acs_x = jnp.dot(a_cs, expand, preferred_element_type=F32, precision=HIGHEST)
    alast_x = acs_x[L - 1:L, :]
    xdt = xs * dt_x

    bm_b = bm.astype(BF16)
    cm_b = cm.astype(BF16)
    lane_g = lax.broadcasted_iota(jnp.int32, (L, gn), 1) // SSD_STATE
    lane_h = lax.broadcasted_iota(jnp.int32, (L, half), 1) // SSD_HEAD_DIM

    state = state_ref[...]
    y_off = jnp.dot(cm_b, state.astype(BF16), preferred_element_type=F32) * jnp.exp(acs_x)

    y_halves = []
    for g in range(SSD_GROUPS):
        cb = _nt_dot(jnp.where(lane_g == g, cm_b, jnp.zeros_like(cm_b)), bm_b)
        xdt_g = xdt[:, g * half:(g + 1) * half]
        yh = jnp.zeros((L, half), F32)
        for hh in range(hpg):
            h = g * hpg + hh
            seg = a_cs[:, h:h + 1] - a_cs_t[h:h + 1, :]
            dec = jnp.exp(jnp.where(tril, seg, -1e30))
            m = (cb * dec).astype(BF16)
            rhs = jnp.where(lane_h == hh, xdt_g, 0.0).astype(BF16)
            yh = yh + jnp.dot(m, rhs, preferred_element_type=F32)
        y_halves.append(yh)

    ds_x = jnp.exp(alast_x - acs_x)
    upd = jnp.dot(bm.T.astype(BF16), (xdt * ds_x).astype(BF16), preferred_element_type=F32)
    srow = lax.broadcasted_iota(jnp.int32, (gn, width), 0) // SSD_STATE
    scol = lax.broadcasted_iota(jnp.int32, (gn, width), 1) // half
    state_ref[...] = jnp.exp(alast_x) * state + jnp.where(srow == scol, upd, 0.0)

    zg = _silu(z_ref[0].astype(F32))
    for g in range(SSD_GROUPS):
        sl = slice(g * half, (g + 1) * half)
        yg = (y_halves[g] + y_off[:, sl] + dsk_ref[:, sl] * xs[:, sl]) * zg[:, sl]
        ms = jnp.mean(yg * yg, axis=-1, keepdims=True)
        y_ref[0, :, sl] = (yg * lax.rsqrt(ms + EPS) * ng_ref[:, sl]).astype(y_ref.dtype)


def _ssd_call(xbc, dt, z, conv_w, conv_b, dt_bias, a_log, d_skip, ssd_norm_g):
    bsz, s, c = xbc.shape
    width = z.shape[-1]
    gn = SSD_GROUPS * SSD_STATE
    heads = width // SSD_HEAD_DIM
    L = min(SSD_CHUNK, s)
    pad = lambda v: jnp.zeros((1, LANES), F32).at[0, :heads].set(v)
    expand = (jnp.arange(LANES)[:, None] == (jnp.arange(width)[None, :] // SSD_HEAD_DIM)).astype(F32)
    dsk = jnp.repeat(d_skip, SSD_HEAD_DIM).reshape(1, width)
    hb = L // HALO
    const = lambda shp: pl.BlockSpec(shp, lambda b, i: (0, 0))
    return pl.pallas_call(
        functools.partial(_ssd_kernel, chunk=L, width=width, gn=gn),
        grid=(bsz, s // L),
        in_specs=[pl.BlockSpec((1, L, c), lambda b, i: (b, i, 0)),
                  pl.BlockSpec((1, HALO, c), lambda b, i: (b, jnp.maximum(i * hb - 1, 0), 0)),
                  pl.BlockSpec((1, L, LANES), lambda b, i: (b, i, 0)),
                  pl.BlockSpec((1, L, width), lambda b, i: (b, i, 0)),
                  const((CONV_WIDTH, c)), const((1, c)), const((1, LANES)), const((1, LANES)),
                  const((LANES, width)), const((1, width)), const((1, width))],
        out_specs=pl.BlockSpec((1, L, width), lambda b, i: (b, i, 0)),
        out_shape=jax.ShapeDtypeStruct((bsz, s, width), BF16),
        scratch_shapes=[pltpu.VMEM((gn, width), F32)],
        compiler_params=_cparams(("arbitrary", "arbitrary")),
        name="ssd_mixer",
    )(xbc, xbc, dt, z, conv_w, conv_b.reshape(1, c), pad(dt_bias), pad(a_log), expand, dsk,
      ssd_norm_g.reshape(1, width))


def _pool_kernel(xp_ref, halo_ref, w_ref, sc_ref, y_ref, *, tm, gdim):
    i = pl.program_id(1)
    hal = jnp.where(i > 0, halo_ref[0].astype(F32), 0.0)
    x = xp_ref[0].astype(F32)
    full = jnp.concatenate([hal, x], axis=0)
    lane_g = lax.broadcasted_iota(jnp.int32, x.shape, 1) // gdim
    tpos = i * tm + lax.broadcasted_iota(jnp.int32, x.shape, 0)
    win_sum = jnp.zeros_like(x)
    cnt = jnp.ones_like(x)
    s = full
    span = 1
    for gi, w in enumerate(POOL_WINDOWS):
        while span < w:
            s = s + pltpu.roll(s, span, 0)
            span *= 2
        win_sum = jnp.where(lane_g == gi, s[HALO:], win_sum)
        cnt = jnp.where(lane_g == gi, jnp.minimum(tpos + 1, w).astype(F32), cnt)
    pooled = (win_sum / cnt - x).astype(BF16)
    y = jnp.dot(pooled, w_ref[...], preferred_element_type=F32) * sc_ref[...]
    y_ref[0] = y.astype(y_ref.dtype)


def _pool_call(xp, pool_w, pool_scale):
    bsz, s, w = xp.shape
    ng, gdim, _ = pool_w.shape
    assert POOL_WINDOWS[-1] <= HALO and all(b == 2 * a for a, b in zip(POOL_WINDOWS, POOL_WINDOWS[1:]))
    tm = min(TM_POOL, s)
    wbd = jnp.zeros((w, w), F32)
    for g in range(ng):
        wbd = wbd.at[g * gdim:(g + 1) * gdim, g * gdim:(g + 1) * gdim].set(pool_w[g])
    hb = tm // HALO
    return pl.pallas_call(
        functools.partial(_pool_kernel, tm=tm, gdim=gdim),
        grid=(bsz, s // tm),
        in_specs=[pl.BlockSpec((1, tm, w), lambda b, i: (b, i, 0)),
                  pl.BlockSpec((1, HALO, w), lambda b, i: (b, jnp.maximum(i * hb - 1, 0), 0)),
                  pl.BlockSpec((w, w), lambda b, i: (0, 0)),
                  pl.BlockSpec((1, w), lambda b, i: (0, 0))],
        out_specs=pl.BlockSpec((1, tm, w), lambda b, i: (b, i, 0)),
        out_shape=jax.ShapeDtypeStruct((bsz, s, w), BF16),
        compiler_params=_cparams(("arbitrary", "arbitrary")),
        name="pool_mixer",
    )(xp, xp, wbd.astype(BF16), pool_scale.reshape(1, w))


def _sb_kernel(q_ref, k_ref, v_ref, y_ref, qm_ref, acc_ref, carry_ref, *, blk, heads):
    i = pl.program_id(1)
    T = blk
    w = q_ref.shape[-1]
    lane_h = lax.broadcasted_iota(jnp.int32, (T, w), 1) // SB_HEAD_DIM
    q = q_ref[0] * (SB_HEAD_DIM ** -0.5)
    for h in range(heads):
        qm_ref[h * T:(h + 1) * T, :] = jnp.where(lane_h == h, q, jnp.zeros_like(q))
    row_i = lax.broadcasted_iota(jnp.int32, (T, T), 0)
    col_i = lax.broadcasted_iota(jnp.int32, (T, T), 1)
    suffix = (row_i > col_i).astype(BF16)
    qrow = lax.broadcasted_iota(jnp.int32, (heads * T, T), 0) % T
    causal = lax.broadcasted_iota(jnp.int32, (heads * T, T), 1) < qrow

    def scores(j, mask):
        ks = k_ref[0, pl.ds(pl.multiple_of(j * T, T), T), :]
        z = _nt_dot(qm_ref[...], ks)
        lk = -_softplus(z)
        if mask is not None:
            lk = jnp.where(mask, lk, 0.0)
        hi = lk.astype(BF16)
        lo = (lk - hi.astype(F32)).astype(BF16)
        later = (jnp.dot(hi, suffix, preferred_element_type=F32)
                 + jnp.dot(lo, suffix, preferred_element_type=F32))
        return z, lk, later

    def weighted(j, z, lk, later, c, mask):
        vs = v_ref[0, pl.ds(pl.multiple_of(j * T, T), T), :]
        wgt = jnp.exp(z + lk + later + c)
        if mask is not None:
            wgt = jnp.where(mask, wgt, 0.0)
        pv = jnp.dot(wgt.astype(BF16), vs, preferred_element_type=F32)
        out = pv[0:T]
        for h in range(1, heads):
            out = jnp.where(lane_h == h, pv[h * T:(h + 1) * T], out)
        return out, c + jnp.sum(lk, axis=1, keepdims=True)

    def alive():
        return (jnp.max(carry_ref[...]) > -SB_SKIP).astype(jnp.int32)

    prev = jnp.maximum(i - 1, 0)
    has_prev = jnp.broadcast_to(i > 0, causal.shape)
    zd, lkd, ltd = scores(i, causal)
    zp, lkp, ltp = scores(prev, has_prev)
    out_d, c = weighted(i, zd, lkd, ltd, jnp.zeros((heads * T, 1), F32), causal)
    out_p, c = weighted(prev, zp, lkp, ltp, c, has_prev)
    acc_ref[...] = out_d + out_p
    carry_ref[...] = c

    def cond(st):
        return jnp.logical_and(st[0] >= 0, st[1] > 0)

    def body(st):
        j = st[0]
        z, lk, later = scores(j, None)
        out, c = weighted(j, z, lk, later, carry_ref[...], None)
        acc_ref[...] += out
        carry_ref[...] = c
        return j - 1, alive()

    lax.while_loop(cond, body, (i - 2, alive()))
    y_ref[0] = acc_ref[...].astype(y_ref.dtype)


def _sb_call(q, k, v):
    bsz, s, w = q.shape
    heads = w // SB_HEAD_DIM
    T = min(SB_BLOCK, s)
    return pl.pallas_call(
        functools.partial(_sb_kernel, blk=T, heads=heads),
        grid=(bsz, s // T),
        in_specs=[pl.BlockSpec((1, T, w), lambda b, i: (b, i, 0)),
                  pl.BlockSpec((1, s, w), lambda b, i: (b, 0, 0)),
                  pl.BlockSpec((1, s, w), lambda b, i: (b, 0, 0))],
        out_specs=pl.BlockSpec((1, T, w), lambda b, i: (b, i, 0)),
        out_shape=jax.ShapeDtypeStruct((bsz, s, w), BF16),
        scratch_shapes=[pltpu.VMEM((heads * T, w), BF16),
                        pltpu.VMEM((T, w), F32),
                        pltpu.VMEM((heads * T, 1), F32)],
        compiler_params=_cparams(("arbitrary", "arbitrary")),
        name="stick_breaking",
    )(q, k, v)


def _out_kernel(ys_ref, yp_ref, yb_ref, x_ref, g1_ref, sh_ref, sc_ref, ng_ref, wo_ref, wq_ref,
                keys_ref, x1_ref, h2_ref, sct_ref, *, cuts):
    mix = jnp.zeros(x_ref.shape[1:], F32)
    for y_ref, (a, b) in zip((ys_ref, yp_ref, yb_ref), cuts):
        mix = mix + jnp.dot(y_ref[0], wo_ref[a:b, :], preferred_element_type=F32)
    x1 = x_ref[0] + g1_ref[0] * mix
    x1_ref[0] = x1
    h2 = _norm_mod(x1, ng_ref[...], sh_ref[0], sc_ref[0]).astype(BF16)
    h2_ref[0] = h2
    qb = jnp.dot(h2, wq_ref[...], preferred_element_type=F32).astype(BF16)
    tp = sct_ref.shape[-1]
    for lst in range(keys_ref.shape[0]):
        sc_t = _nt_dot(keys_ref[lst], qb[:, lst * PEER_HALF:(lst + 1) * PEER_HALF])
        for k in range(sct_ref.shape[0]):
            sct_ref[k, lst] = sc_t[:, k * tp:(k + 1) * tp]


def _out_call(ys, yp, yb, x, g1, sh, sc, ng, w_out, wq, keys):
    bsz, s, d = x.shape
    tm = min(TM_IN, s)
    nlist = keys.shape[0]
    cuts, a = [], 0
    for y in (ys, yp, yb):
        cuts.append((a, a + y.shape[-1]))
        a += y.shape[-1]
    tok = lambda w: pl.BlockSpec((1, tm, w), lambda b, i: (b, i, 0))
    vec = pl.BlockSpec((1, 1, d), lambda b, i: (b, 0, 0))
    full2 = lambda arr: pl.BlockSpec(arr.shape, lambda b, i: (0,) * arr.ndim)
    nblk = s // tm
    tp = min(TB_PEER, s)
    per = tm // tp
    return pl.pallas_call(
        functools.partial(_out_kernel, cuts=tuple(cuts)),
        grid=(bsz, nblk),
        in_specs=[tok(ys.shape[-1]), tok(yp.shape[-1]), tok(yb.shape[-1]), tok(d), vec, vec, vec,
                  pl.BlockSpec((1, d), lambda b, i: (0, 0)), full2(w_out), full2(wq), full2(keys)],
        out_specs=[tok(d), tok(d),
                   pl.BlockSpec((per, nlist, N_KEYS, tp), lambda b, i: (b * nblk + i, 0, 0, 0))],
        out_shape=[jax.ShapeDtypeStruct((bsz, s, d), F32),
                   jax.ShapeDtypeStruct((bsz, s, d), BF16),
                   jax.ShapeDtypeStruct((bsz * s // tp, nlist, N_KEYS, tp), F32)],
        compiler_params=_cparams(("arbitrary", "arbitrary")),
        name="out_proj_peer_query",
    )(ys, yp, yb, x, g1, sh, sc, ng, w_out, wq, keys)


def _top_rows(s, k, payload=None, exact=True):
    nrows = s.shape[0]
    rows = lax.broadcasted_iota(jnp.int32, s.shape, 0).astype(F32)
    vals, picks = [], []
    for _ in range(k):
        m = jnp.max(s, axis=0, keepdims=True)
        hit = s == m
        if exact or payload is None:
            idx = jnp.min(jnp.where(hit, rows, float(nrows)), axis=0, keepdims=True)
        if exact:
            hit = rows == idx
        vals.append(m)
        if payload is None:
            picks.append(idx)
        else:
            picks.append(jnp.sum(jnp.where(hit, payload, 0.0), axis=0, keepdims=True))
        s = jnp.where(hit, -jnp.inf, s)
    removed = jnp.sum(jnp.where(s == -jnp.inf, 1.0, 0.0), axis=0, keepdims=True)
    return jnp.concatenate(vals, axis=0), jnp.concatenate(picks, axis=0), removed


def _head_topk(s1, s2, exact):
    K = PEER_TOPK
    v1, x1, n1 = _top_rows(s1, K, exact=exact)
    v2, x2, n2 = _top_rows(s2, K, exact=exact)
    cand, ids = _pair_candidates(v1, x1, v2, x2)
    best, expert, n3 = _top_rows(cand, K, payload=ids, exact=exact)
    pad_rows = cand.shape[0] - sum(K // (a + 1) for a in range(K))
    bad = (jnp.abs(n1 - K) + jnp.abs(n2 - K) + jnp.abs(n3 - (K + pad_rows)))
    e = jnp.exp(best - best[0:1, :])
    gate = e / jnp.sum(e, axis=0, keepdims=True)
    i1 = jnp.floor(expert * (1.0 / N_KEYS))
    return i1, expert - i1 * float(N_KEYS), 0.5 * gate, bad


def _pair_candidates(v1, x1, v2, x2):
    K = PEER_TOPK
    cands, ids = [], []
    a = 0
    while K // (a + 1) > 1:
        n = K // (a + 1)
        npad = -(-n // SUBLANES) * SUBLANES
        c = v1[a:a + 1, :] + v2[0:npad, :]
        if n < npad:
            c = jnp.where(lax.broadcasted_iota(jnp.int32, c.shape, 0) < n, c, -jnp.inf)
        cands.append(c)
        ids.append(x1[a:a + 1, :] * float(N_KEYS) + x2[0:npad, :])
        a += 1
    cands.append(v1[a:K, :] + v2[0:1, :])
    ids.append(x1[a:K, :] * float(N_KEYS) + x2[0:1, :])
    return jnp.concatenate(cands, axis=0), jnp.concatenate(ids, axis=0)


def _peer_kernel(sct_ref, h_ref, u_ref, v_ref, x_ref, g2_ref, fg_ref, o_ref,
                 pk_ref, pt_ref, scr_ref, g_ref, coef_ref, acc_ref, *, final):
    s = pl.program_id(0)
    j = pl.program_id(1)
    K = PEER_TOPK
    slot = s % 2
    eb = u_ref.shape[0]

    @pl.when(jnp.logical_and(s == 0, j == 0))
    def _():
        pk_ref[...] = jnp.zeros_like(pk_ref)
        scr_ref[...] = jnp.zeros_like(scr_ref)
        g_ref[...] = jnp.zeros_like(g_ref)
        coef_ref[...] = jnp.zeros_like(coef_ref)
        acc_ref[...] = jnp.zeros_like(acc_ref)

    @pl.when(j == 0)
    def _():
        for k in range(pk_ref.shape[0]):
            pt_ref[k] = pk_ref[k].T

    r0 = pl.multiple_of(j * K, K)
    i1, i2, gate, bad = _head_topk(sct_ref[0], sct_ref[1], exact=False)
    pk_ref[0, pl.ds(r0, K), :] = i1
    pk_ref[1, pl.ds(r0, K), :] = i2
    pk_ref[2, pl.ds(r0, K), :] = gate
    tied = jnp.max(bad) > 0.0

    npair = pt_ref.shape[-1]
    sub = lax.broadcasted_iota(jnp.int32, (N_KEYS, npair), 0).astype(F32)
    t0 = pl.multiple_of(j * GATE_GROUP, GATE_GROUP)
    for tl in range(GATE_GROUP):
        r = pl.ds(t0 + tl, 1)
        lhs = jnp.where(sub == pt_ref[0, r, :], pt_ref[2, r, :], 0.0).astype(BF16)
        rhs = jnp.where(sub == pt_ref[1, r, :], 1.0, 0.0).astype(BF16)
        scr_ref[pl.ds(tl, N_KEYS, stride=GATE_PITCH), :] = _nt_dot(lhs, rhs)
    for a in range(N_KEYS):
        g_ref[1 - slot, a, pl.ds(t0, GATE_GROUP), :] = (
            scr_ref[a * GATE_PITCH:a * GATE_PITCH + GATE_GROUP, :].astype(g_ref.dtype))

    h = h_ref[...]
    d = h.shape[-1]
    acc_ref[...] += jnp.dot(coef_ref[...], v_ref[:, :d], preferred_element_type=F32)

    a0 = j * (eb // N_KEYS)
    pieces = []
    for q in range(eb // DENSE_SUB):
        act = _nt_dot(h, u_ref[q * DENSE_SUB:(q + 1) * DENSE_SUB, :d])
        w = (act * (1.0 + lax.erf(act * (2.0 ** -0.5)))).astype(BF16)
        for a in range(DENSE_SUB // N_KEYS):
            ga = g_ref[slot, a0 + q * (DENSE_SUB // N_KEYS) + a]
            pieces.append(w[:, a * N_KEYS:(a + 1) * N_KEYS] * ga)
    coef_ref[...] = jnp.concatenate(pieces, axis=1)

    @pl.when(tied)
    def _():
        i1, i2, gate, _ = _head_topk(sct_ref[0], sct_ref[1], exact=True)
        pk_ref[0, pl.ds(r0, K), :] = i1
        pk_ref[1, pl.ds(r0, K), :] = i2
        pk_ref[2, pl.ds(r0, K), :] = gate

    @pl.when(j == 0)
    def _():
        x2 = x_ref[...] + g2_ref[0] * acc_ref[...]
        if final:
            ms = jnp.mean(x2 * x2, axis=-1, keepdims=True)
            x2 = x2 * lax.rsqrt(ms + EPS) * fg_ref[...]
        o_ref[...] = x2
        acc_ref[...] = jnp.zeros_like(acc_ref)


def _peer_call(sct, h2, u, v, x1, g2, fg, final):
    bsz, s, d = x1.shape
    t = bsz * s
    ne = u.shape[0]
    _, nlist, nk, _ = sct.shape
    tb = min(TB_PEER, s)
    eb = ne // PEER_HEADS
    npair = PEER_HEADS * PEER_TOPK
    assert tb == PEER_HEADS * GATE_GROUP and nlist == 2 * PEER_HEADS and eb % MXU_WIDTH == 0
    nt = t // tb
    per_b = s // tb
    nchunk = PEER_HEADS
    dense_tile = lambda i: jnp.clip(i - 2, 0, nt - 1)
    row = lambda i, j: (dense_tile(i), 0)
    out_tile = lambda i, j: jnp.clip(i - 2 - (j == 0).astype(jnp.int32), 0, nt - 1)
    out_row = lambda i, j: (out_tile(i, j), 0)
    dp = d + LANES if (d // LANES) % 2 == 0 else d
    u = jnp.pad(u, ((0, 0), (0, dp - d)))
    v = jnp.pad(v, ((0, 0), (0, dp - d)))
    out = pl.pallas_call(
        functools.partial(_peer_kernel, final=final),
        grid=(nt + 3, nchunk),
        in_specs=[pl.BlockSpec((None, 2, nk, tb), lambda i, j: (jnp.minimum(i, nt - 1), j, 0, 0)),
                  pl.BlockSpec((tb, d), row),
                  pl.BlockSpec((eb, dp), lambda i, j: (j, 0)),
                  pl.BlockSpec((eb, dp), lambda i, j: ((j + nchunk - 1) % nchunk, 0)),
                  pl.BlockSpec((tb, d), out_row),
                  pl.BlockSpec((1, 1, d), lambda i, j: (out_tile(i, j) // per_b, 0, 0)),
                  pl.BlockSpec((1, d), lambda i, j: (0, 0))],
        out_specs=pl.BlockSpec((tb, d), out_row),
        out_shape=jax.ShapeDtypeStruct((t, d), F32),
        scratch_shapes=[pltpu.VMEM((3, npair, tb), F32),
                        pltpu.VMEM((3, tb, npair), F32),
                        pltpu.VMEM((N_KEYS * GATE_PITCH, N_KEYS), F32),
                        pltpu.VMEM((2, N_KEYS, tb, N_KEYS), BF16),
                        pltpu.VMEM((tb, eb), BF16),
                        pltpu.VMEM((tb, d), F32)],
        compiler_params=_cparams(("arbitrary", "arbitrary")),
        name="peer_fused",
    )(sct, h2.reshape(t, d), u, v, x1.reshape(t, d), g2, fg)
    return out.reshape(bsz, s, d)


def kernel(x, c, ada_w, ada_b, mix_norm_g, ffn_norm_g, w_in, conv_w, conv_b, dt_bias, a_log, d_skip,
           ssd_norm_g, pool_w, pool_scale, w_out, peer_wq, peer_keys, peer_u, peer_v, final_norm_g):
    bsz, s, d = x.shape
    depth = ada_w.shape[0]
    ssd_w = ssd_norm_g.shape[-1]
    heads = dt_bias.shape[-1]
    xbc_w = conv_w.shape[-1]
    pool_wd = pool_scale.shape[-1]
    sb_w = (w_in.shape[-1] - ssd_w - xbc_w - heads - pool_wd) // 3
    o_z, o_xbc, o_dt, o_xp = 0, ssd_w, ssd_w + xbc_w, ssd_w + xbc_w + heads
    o_q = o_xp + pool_wd
    widths = (ssd_w, xbc_w, pool_wd, sb_w, sb_w, sb_w, LANES)

    mods = _mod_call(c, ada_w, ada_b)
    for l in range(depth):
        mod = [mods[l, :, k * d:(k + 1) * d].reshape(bsz, 1, d) for k in range(6)]
        sh1, sc1, g1, sh2, sc2, g2 = mod
        wl = w_in[l]
        w_cat = jnp.concatenate(
            [wl[:, o_z:o_xbc], wl[:, o_xbc:o_dt], wl[:, o_xp:o_q], wl[:, o_q:],
             jnp.pad(wl[:, o_dt:o_xp], ((0, 0), (0, LANES - heads)))], axis=1).astype(BF16)
        z, xbc, xp, q, k, v, dt = _in_call(x, sh1, sc1, mix_norm_g[l].reshape(1, d), w_cat, widths)
        y_ssd = _ssd_call(xbc, dt, z, conv_w[l], conv_b[l], dt_bias[l], a_log[l], d_skip[l],
                          ssd_norm_g[l])
        y_pool = _pool_call(xp, pool_w[l], pool_scale[l])
        y_sb = _sb_call(q, k, v)
        keys = peer_keys[l].reshape(-1, N_KEYS, PEER_HALF).astype(BF16)
        x1, h2, sct = _out_call(y_ssd, y_pool, y_sb, x, g1, sh2, sc2, ffn_norm_g[l].reshape(1, d),
                                w_out[l].astype(BF16), peer_wq[l].astype(BF16), keys)
        x = _peer_call(sct, h2, peer_u[l].astype(BF16), peer_v[l].astype(BF16), x1, g2,
                       final_norm_g.reshape(1, d), final=(l == depth - 1))
    return x
```

```python
import functools

import jax
import jax.numpy as jnp
from jax import lax
from jax.experimental import pallas as pl
from jax.experimental.pallas import tpu as pltpu

F32 = jnp.float32
BF16 = jnp.bfloat16
EPS = 1e-6
HIGHEST = lax.Precision.HIGHEST

SSD_HEAD_DIM = 64
SSD_GROUPS = 2
SSD_STATE = 64
CONV_WIDTH = 4
POOL_WINDOWS = (2, 4, 8, 16)
SB_HEAD_DIM = 64
PEER_HEADS = 8
PEER_TOPK = 16
N_KEYS = 128
PEER_HALF = 128

LANES = 128
SUBLANES = 8
MXU_WIDTH = 256
VMEM_LIMIT = 56 * 1024 * 1024

TM_IN = 512
SSD_CHUNK = 256
HALO = 16
TM_POOL = 512
SB_BLOCK = 128
SB_SKIP = 40.0
GATE_GROUP = 32
GATE_PITCH = 36
TB_PEER = 256
DENSE_SUB = 256
PEER_SPLIT = 2


def _cparams(sem):
    return pltpu.CompilerParams(dimension_semantics=sem, vmem_limit_bytes=VMEM_LIMIT)


def _nt_dot(a, b):
    return lax.dot_general(a, b, (((1,), (1,)), ((), ())), preferred_element_type=F32)


def _softplus(x):
    return jnp.maximum(x, 0.0) + jnp.log(1.0 + jnp.exp(-jnp.abs(x)))


def _silu(x):
    return x * jax.nn.sigmoid(x)


def _mod_kernel(c_ref, w_ref, b_ref, o_ref):
    cond = _silu(c_ref[...])
    o_ref[0] = jnp.dot(cond, w_ref[0], preferred_element_type=F32, precision=HIGHEST) + b_ref[0]


def _mod_call(c, ada_w, ada_b):
    nl, d, n6 = ada_w.shape
    bsz = c.shape[0]
    tn = 1536
    return pl.pallas_call(
        _mod_kernel,
        grid=(nl, n6 // tn),
        in_specs=[pl.BlockSpec((bsz, d), lambda l, j: (0, 0)),
                  pl.BlockSpec((1, d, tn), lambda l, j: (l, 0, j)),
                  pl.BlockSpec((1, 1, tn), lambda l, j: (l, 0, j))],
        out_specs=pl.BlockSpec((1, bsz, tn), lambda l, j: (l, 0, j)),
        out_shape=jax.ShapeDtypeStruct((nl, bsz, n6), F32),
        compiler_params=_cparams(("arbitrary", "arbitrary")),
        name="adaln_mod",
    )(c, ada_w, ada_b.reshape(nl, 1, n6))


def _norm_mod(x, g, sh, sc):
    ms = jnp.mean(x * x, axis=-1, keepdims=True)
    y = x * lax.rsqrt(ms + EPS) * g
    return y * (1.0 + sc) + sh


def _in_kernel(x_ref, sh_ref, sc_ref, g_ref, w_ref, z_ref, xbc_ref, xp_ref, q_ref, k_ref, v_ref,
               dt_ref, *, cuts):
    h = _norm_mod(x_ref[0], g_ref[...], sh_ref[0], sc_ref[0]).astype(BF16)
    outs = (z_ref, xbc_ref, xp_ref, q_ref, k_ref, v_ref, dt_ref)
    for o_ref, (a, b) in zip(outs, cuts):
        o_ref[0] = jnp.dot(h, w_ref[:, a:b], preferred_element_type=F32).astype(o_ref.dtype)


def _in_call(x, sh, sc, g, w_cat, widths):
    bsz, s, d = x.shape
    tm = min(TM_IN, s)
    cuts, a = [], 0
    for w in widths:
        cuts.append((a, a + w))
        a += w
    dtypes = (BF16, BF16, BF16, BF16, BF16, BF16, F32)
    tok = lambda w: pl.BlockSpec((1, tm, w), lambda b, i: (b, i, 0))
    vec = pl.BlockSpec((1, 1, d), lambda b, i: (b, 0, 0))
    return pl.pallas_call(
        functools.partial(_in_kernel, cuts=tuple(cuts)),
        grid=(bsz, s // tm),
        in_specs=[tok(d), vec, vec,
                  pl.BlockSpec((1, d), lambda b, i: (0, 0)),
                  pl.BlockSpec(w_cat.shape, lambda b, i: (0, 0))],
        out_specs=[tok(w) for w in widths],
        out_shape=[jax.ShapeDtypeStruct((bsz, s, w), dt) for w, dt in zip(widths, dtypes)],
        compiler_params=_cparams(("arbitrary", "arbitrary")),
        name="in_proj",
    )(x, sh, sc, g, w_cat)


def _ssd_kernel(xbc_ref, halo_ref, dt_ref, z_ref, cw_ref, cb_ref, dtb_ref, alog_ref, expand_ref,
                dsk_ref, ng_ref, y_ref, state_ref, *, chunk, width, gn):
    i = pl.program_id(1)
    L = chunk
    heads = width // SSD_HEAD_DIM
    hpg = heads // SSD_GROUPS
    half = width // SSD_GROUPS

    @pl.when(i == 0)
    def _():
        state_ref[...] = jnp.zeros_like(state_ref)

    hal = jnp.where(i > 0, halo_ref[0].astype(F32), 0.0)
    full = jnp.concatenate([hal, xbc_ref[0].astype(F32)], axis=0)
    cw = cw_ref[...]
    acc = full * cw[CONV_WIDTH - 1:CONV_WIDTH]
    for j in range(1, CONV_WIDTH):
        acc = acc + pltpu.roll(full, j, 0) * cw[CONV_WIDTH - 1 - j:CONV_WIDTH - j]
    xbc = _silu(acc[HALO:] + cb_ref[...])
    xs = xbc[:, :width]
    bm = xbc[:, width:width + gn]
    cm = xbc[:, width + gn:width + 2 * gn]

    dt = _softplus(dt_ref[0] + dtb_ref[...])
    a = dt * (-jnp.exp(alog_ref[...]))
    row_i = lax.broadcasted_iota(jnp.int32, (L, L), 0)
    col_i = lax.broadcasted_iota(jnp.int32, (L, L), 1)
    tril = row_i >= col_i
    a_cs = jnp.dot(tril.astype(F32), a, preferred_element_type=F32, precision=HIGHEST)
    a_cs_t = a_cs.T
    expand = expand_ref[...]
    dt_x = jnp.dot(dt, expand, preferred_element_type=F32, precision=HIGHEST)
    acs_x = jnp.dot(a_cs, expand, preferred_element_type=F32, precision=HIGHEST)
    alast_x = acs_x[L - 1:L, :]
    xdt = xs * dt_x

    bm_b = bm.astype(BF16)
    cm_b = cm.astype(BF16)
    lane_g = lax.broadcasted_iota(jnp.int32, (L, gn), 1) // SSD_STATE
    lane_h = lax.broadcasted_iota(jnp.int32, (L, half), 1) // SSD_HEAD_DIM

    state = state_ref[...]
    y_off = jnp.dot(cm_b, state.astype(BF16), preferred_element_type=F32) * jnp.exp(acs_x)

    y_halves = []
    for g in range(SSD_GROUPS):
        cb = _nt_dot(jnp.where(lane_g == g, cm_b, jnp.zeros_like(cm_b)), bm_b)
        xdt_g = xdt[:, g * half:(g + 1) * half]
        yh = jnp.zeros((L, half), F32)
        for hh in range(hpg):
            h = g * hpg + hh
            seg = a_cs[:, h:h + 1] - a_cs_t[h:h + 1, :]
            dec = jnp.exp(jnp.where(tril, seg, -1e30))
            m = (cb * dec).astype(BF16)
            rhs = jnp.where(lane_h == hh, xdt_g, 0.0).astype(BF16)
            yh = yh + jnp.dot(m, rhs, preferred_element_type=F32)
        y_halves.append(yh)

    ds_x = jnp.exp(alast_x - acs_x)
    upd = jnp.dot(bm.T.astype(BF16), (xdt * ds_x).astype(BF16), preferred_element_type=F32)
    srow = lax.broadcasted_iota(jnp.int32, (gn, width), 0) // SSD_STATE
    scol = lax.broadcasted_iota(jnp.int32, (gn, width), 1) // half
    state_ref[...] = jnp.exp(alast_x) * state + jnp.where(srow == scol, upd, 0.0)

    zg = _silu(z_ref[0].astype(F32))
    for g in range(SSD_GROUPS):
        sl = slice(g * half, (g + 1) * half)
        yg = (y_halves[g] + y_off[:, sl] + dsk_ref[:, sl] * xs[:, sl]) * zg[:, sl]
        ms = jnp.mean(yg * yg, axis=-1, keepdims=True)
        y_ref[0, :, sl] = (yg * lax.rsqrt(ms + EPS) * ng_ref[:, sl]).astype(y_ref.dtype)


def _ssd_call(xbc, dt, z, conv_w, conv_b, dt_bias, a_log, d_skip, ssd_norm_g):
    bsz, s, c = xbc.shape
    width = z.shape[-1]
    gn = SSD_GROUPS * SSD_STATE
    heads = width // SSD_HEAD_DIM
    L = min(SSD_CHUNK, s)
    pad = lambda v: jnp.zeros((1, LANES), F32).at[0, :heads].set(v)
    expand = (jnp.arange(LANES)[:, None] == (jnp.arange(width)[None, :] // SSD_HEAD_DIM)).astype(F32)
    dsk = jnp.repeat(d_skip, SSD_HEAD_DIM).reshape(1, width)
    hb = L // HALO
    const = lambda shp: pl.BlockSpec(shp, lambda b, i: (0, 0))
    return pl.pallas_call(
        functools.partial(_ssd_kernel, chunk=L, width=width, gn=gn),
        grid=(bsz, s // L),
        in_specs=[pl.BlockSpec((1, L, c), lambda b, i: (b, i, 0)),
                  pl.BlockSpec((1, HALO, c), lambda b, i: (b, jnp.maximum(i * hb - 1, 0), 0)),
                  pl.BlockSpec((1, L, LANES), lambda b, i: (b, i, 0)),
                  pl.BlockSpec((1, L, width), lambda b, i: (b, i, 0)),
                  const((CONV_WIDTH, c)), const((1, c)), const((1, LANES)), const((1, LANES)),
                  const((LANES, width)), const((1, width)), const((1, width))],
        out_specs=pl.BlockSpec((1, L, width), lambda b, i: (b, i, 0)),
        out_shape=jax.ShapeDtypeStruct((bsz, s, width), BF16),
        scratch_shapes=[pltpu.VMEM((gn, width), F32)],
        compiler_params=_cparams(("arbitrary", "arbitrary")),
        name="ssd_mixer",
    )(xbc, xbc, dt, z, conv_w, conv_b.reshape(1, c), pad(dt_bias), pad(a_log), expand, dsk,
      ssd_norm_g.reshape(1, width))


def _pool_kernel(xp_ref, halo_ref, w_ref, sc_ref, y_ref, *, tm, gdim):
    i = pl.program_id(1)
    hal = jnp.where(i > 0, halo_ref[0].astype(F32), 0.0)
    x = xp_ref[0].astype(F32)
    full = jnp.concatenate([hal, x], axis=0)
    lane_g = lax.broadcasted_iota(jnp.int32, x.shape, 1) // gdim
    tpos = i * tm + lax.broadcasted_iota(jnp.int32, x.shape, 0)
    win_sum = jnp.zeros_like(x)
    cnt = jnp.ones_like(x)
    s = full
    span = 1
    for gi, w in enumerate(POOL_WINDOWS):
        while span < w:
            s = s + pltpu.roll(s, span, 0)
            span *= 2
        win_sum = jnp.where(lane_g == gi, s[HALO:], win_sum)
        cnt = jnp.where(lane_g == gi, jnp.minimum(tpos + 1, w).astype(F32), cnt)
    pooled = (win_sum / cnt - x).astype(BF16)
    y = jnp.dot(pooled, w_ref[...], preferred_element_type=F32) * sc_ref[...]
    y_ref[0] = y.astype(y_ref.dtype)


def _pool_call(xp, pool_w, pool_scale):
    bsz, s, w = xp.shape
    ng, gdim, _ = pool_w.shape
    assert POOL_WINDOWS[-1] <= HALO and all(b == 2 * a for a, b in zip(POOL_WINDOWS, POOL_WINDOWS[1:]))
    tm = min(TM_POOL, s)
    wbd = jnp.zeros((w, w), F32)
    for g in range(ng):
        wbd = wbd.at[g * gdim:(g + 1) * gdim, g * gdim:(g + 1) * gdim].set(pool_w[g])
    hb = tm // HALO
    return pl.pallas_call(
        functools.partial(_pool_kernel, tm=tm, gdim=gdim),
        grid=(bsz, s // tm),
        in_specs=[pl.BlockSpec((1, tm, w), lambda b, i: (b, i, 0)),
                  pl.BlockSpec((1, HALO, w), lambda b, i: (b, jnp.maximum(i * hb - 1, 0), 0)),
                  pl.BlockSpec((w, w), lambda b, i: (0, 0)),
                  pl.BlockSpec((1, w), lambda b, i: (0, 0))],
        out_specs=pl.BlockSpec((1, tm, w), lambda b, i: (b, i, 0)),
        out_shape=jax.ShapeDtypeStruct((bsz, s, w), BF16),
        compiler_params=_cparams(("arbitrary", "arbitrary")),
        name="pool_mixer",
    )(xp, xp, wbd.astype(BF16), pool_scale.reshape(1, w))


def _sb_kernel(q_ref, k_ref, v_ref, y_ref, qm_ref, acc_ref, carry_ref, *, blk, heads):
    i = pl.program_id(1)
    T = blk
    w = q_ref.shape[-1]
    lane_h = lax.broadcasted_iota(jnp.int32, (T, w), 1) // SB_HEAD_DIM
    q = q_ref[0] * (SB_HEAD_DIM ** -0.5)
    for h in range(heads):
        qm_ref[h * T:(h + 1) * T, :] = jnp.where(lane_h == h, q, jnp.zeros_like(q))
    row_i = lax.broadcasted_iota(jnp.int32, (T, T), 0)
    col_i = lax.broadcasted_iota(jnp.int32, (T, T), 1)
    suffix = (row_i > col_i).astype(BF16)
    qrow = lax.broadcasted_iota(jnp.int32, (heads * T, T), 0) % T
    causal = lax.broadcasted_iota(jnp.int32, (heads * T, T), 1) < qrow

    def scores(j, mask):
        ks = k_ref[0, pl.ds(pl.multiple_of(j * T, T), T), :]
        z = _nt_dot(qm_ref[...], ks)
        lk = -_softplus(z)
        if mask is not None:
            lk = jnp.where(mask, lk, 0.0)
        hi = lk.astype(BF16)
        lo = (lk - hi.astype(F32)).astype(BF16)
        later = (jnp.dot(hi, suffix, preferred_element_type=F32)
                 + jnp.dot(lo, suffix, preferred_element_type=F32))
        return z, lk, later

    def weighted(j, z, lk, later, c, mask):
        vs = v_ref[0, pl.ds(pl.multiple_of(j * T, T), T), :]
        wgt = jnp.exp(z + lk + later + c)
        if mask is not None:
            wgt = jnp.where(mask, wgt, 0.0)
        pv = jnp.dot(wgt.astype(BF16), vs, preferred_element_type=F32)
        out = pv[0:T]
        for h in range(1, heads):
            out = jnp.where(lane_h == h, pv[h * T:(h + 1) * T], out)
        return out, c + jnp.sum(lk, axis=1, keepdims=True)

    def alive():
        return (jnp.max(carry_ref[...]) > -SB_SKIP).astype(jnp.int32)

    prev = jnp.maximum(i - 1, 0)
    has_prev = jnp.broadcast_to(i > 0, causal.shape)
    zd, lkd, ltd = scores(i, causal)
    zp, lkp, ltp = scores(prev, has_prev)
    out_d, c = weighted(i, zd, lkd, ltd, jnp.zeros((heads * T, 1), F32), causal)
    out_p, c = weighted(prev, zp, lkp, ltp, c, has_prev)
    acc_ref[...] = out_d + out_p
    carry_ref[...] = c

    def cond(st):
        return jnp.logical_and(st[0] >= 0, st[1] > 0)

    def body(st):
        j = st[0]
        z, lk, later = scores(j, None)
        out, c = weighted(j, z, lk, later, carry_ref[...], None)
        acc_ref[...] += out
        carry_ref[...] = c
        return j - 1, alive()

    lax.while_loop(cond, body, (i - 2, alive()))
    y_ref[0] = acc_ref[...].astype(y_ref.dtype)


def _sb_call(q, k, v):
    bsz, s, w = q.shape
    heads = w // SB_HEAD_DIM
    T = min(SB_BLOCK, s)
    return pl.pallas_call(
        functools.partial(_sb_kernel, blk=T, heads=heads),
        grid=(bsz, s // T),
        in_specs=[pl.BlockSpec((1, T, w), lambda b, i: (b, i, 0)),
                  pl.BlockSpec((1, s, w), lambda b, i: (b, 0, 0)),
                  pl.BlockSpec((1, s, w), lambda b, i: (b, 0, 0))],
        out_specs=pl.BlockSpec((1, T, w), lambda b, i: (b, i, 0)),
        out_shape=jax.ShapeDtypeStruct((bsz, s, w), BF16),
        scratch_shapes=[pltpu.VMEM((heads * T, w), BF16),
                        pltpu.VMEM((T, w), F32),
                        pltpu.VMEM((heads * T, 1), F32)],
        compiler_params=_cparams(("arbitrary", "arbitrary")),
        name="stick_breaking",
    )(q, k, v)


def _out_kernel(ys_ref, yp_ref, yb_ref, x_ref, g1_ref, sh_ref, sc_ref, ng_ref, wo_ref, wq_ref,
                keys_ref, x1_ref, h2_ref, sct_ref, *, cuts):
    mix = jnp.zeros(x_ref.shape[1:], F32)
    for y_ref, (a, b) in zip((ys_ref, yp_ref, yb_ref), cuts):
        mix = mix + jnp.dot(y_ref[0], wo_ref[a:b, :], preferred_element_type=F32)
    x1 = x_ref[0] + g1_ref[0] * mix
    x1_ref[0] = x1
    h2 = _norm_mod(x1, ng_ref[...], sh_ref[0], sc_ref[0]).astype(BF16)
    h2_ref[0] = h2
    qb = jnp.dot(h2, wq_ref[...], preferred_element_type=F32).astype(BF16)
    tp = sct_ref.shape[-1]
    for lst in range(keys_ref.shape[0]):
        sc_t = _nt_dot(keys_ref[lst], qb[:, lst * PEER_HALF:(lst + 1) * PEER_HALF])
        for k in range(sct_ref.shape[0]):
            sct_ref[k, lst] = sc_t[:, k * tp:(k + 1) * tp]


def _out_call(ys, yp, yb, x, g1, sh, sc, ng, w_out, wq, keys):
    bsz, s, d = x.shape
    tm = min(TM_IN, s)
    nlist = keys.shape[0]
    cuts, a = [], 0
    for y in (ys, yp, yb):
        cuts.append((a, a + y.shape[-1]))
        a += y.shape[-1]
    tok = lambda w: pl.BlockSpec((1, tm, w), lambda b, i: (b, i, 0))
    vec = pl.BlockSpec((1, 1, d), lambda b, i: (b, 0, 0))
    full2 = lambda arr: pl.BlockSpec(arr.shape, lambda b, i: (0,) * arr.ndim)
    nblk = s // tm
    tp = min(TB_PEER, s)
    per = tm // tp
    return pl.pallas_call(
        functools.partial(_out_kernel, cuts=tuple(cuts)),
        grid=(bsz, nblk),
        in_specs=[tok(ys.shape[-1]), tok(yp.shape[-1]), tok(yb.shape[-1]), tok(d), vec, vec, vec,
                  pl.BlockSpec((1, d), lambda b, i: (0, 0)), full2(w_out), full2(wq), full2(keys)],
        out_specs=[tok(d), tok(d),
                   pl.BlockSpec((per, nlist, N_KEYS, tp), lambda b, i: (b * nblk + i, 0, 0, 0))],
        out_shape=[jax.ShapeDtypeStruct((bsz, s, d), F32),
                   jax.ShapeDtypeStruct((bsz, s, d), BF16),
                   jax.ShapeDtypeStruct((bsz * s // tp, nlist, N_KEYS, tp), F32)],
        compiler_params=_cparams(("arbitrary", "arbitrary")),
        name="out_proj_peer_query",
    )(ys, yp, yb, x, g1, sh, sc, ng, w_out, wq, keys)


def _top_rows(s, k, payload=None):
    nrows = s.shape[0]
    rows = lax.broadcasted_iota(jnp.int32, s.shape, 0).astype(F32)
    vals, picks = [], []
    for _ in range(k):
        m = jnp.max(s, axis=0, keepdims=True)
        idx = jnp.min(jnp.where(s == m, rows, float(nrows)), axis=0, keepdims=True)
        hit = rows == idx
        vals.append(m)
        if payload is None:
            picks.append(idx)
        else:
            picks.append(jnp.sum(jnp.where(hit, payload, 0.0), axis=0, keepdims=True))
        s = jnp.where(hit, -jnp.inf, s)
    return jnp.concatenate(vals, axis=0), jnp.concatenate(picks, axis=0)


def _pair_candidates(v1, x1, v2, x2):
    K = PEER_TOPK
    cands, ids = [], []
    a = 0
    while K // (a + 1) > 1:
        n = K // (a + 1)
        npad = -(-n // SUBLANES) * SUBLANES
        c = v1[a:a + 1, :] + v2[0:npad, :]
        if n < npad:
            c = jnp.where(lax.broadcasted_iota(jnp.int32, c.shape, 0) < n, c, -jnp.inf)
        cands.append(c)
        ids.append(x1[a:a + 1, :] * float(N_KEYS) + x2[0:npad, :])
        a += 1
    cands.append(v1[a:K, :] + v2[0:1, :])
    ids.append(x1[a:K, :] * float(N_KEYS) + x2[0:1, :])
    return jnp.concatenate(cands, axis=0), jnp.concatenate(ids, axis=0)


def _peer_kernel(sct_ref, h_ref, *refs, final):
    u_refs, v_refs = refs[:PEER_SPLIT], refs[PEER_SPLIT:2 * PEER_SPLIT]
    (x_ref, g2_ref, fg_ref, o_ref,
     pk_ref, pt_ref, scr_ref, g_ref, coef_ref, acc_ref) = refs[2 * PEER_SPLIT:]
    _peer_body(sct_ref, h_ref, u_refs, v_refs, x_ref, g2_ref, fg_ref, o_ref,
               pk_ref, pt_ref, scr_ref, g_ref, coef_ref, acc_ref, final=final)


def _peer_body(sct_ref, h_ref, u_refs, v_refs, x_ref, g2_ref, fg_ref, o_ref,
               pk_ref, pt_ref, scr_ref, g_ref, coef_ref, acc_ref, *, final):
    s = pl.program_id(0)
    j = pl.program_id(1)
    K = PEER_TOPK
    slot = s % 2
    ebs = u_refs[0].shape[0]
    eb = ebs * len(u_refs)

    @pl.when(jnp.logical_and(s == 0, j == 0))
    def _():
        pk_ref[...] = jnp.zeros_like(pk_ref)
        scr_ref[...] = jnp.zeros_like(scr_ref)
        g_ref[...] = jnp.zeros_like(g_ref)
        coef_ref[...] = jnp.zeros_like(coef_ref)
        acc_ref[...] = jnp.zeros_like(acc_ref)

    @pl.when(j == 0)
    def _():
        for k in range(pk_ref.shape[0]):
            pt_ref[k] = pk_ref[k].T

    v1, x1 = _top_rows(sct_ref[0], K)
    v2, x2 = _top_rows(sct_ref[1], K)
    cand, ids = _pair_candidates(v1, x1, v2, x2)
    best, expert = _top_rows(cand, K, payload=ids)
    e = jnp.exp(best - best[0:1, :])
    gate = e / jnp.sum(e, axis=0, keepdims=True)
    i1 = jnp.floor(expert * (1.0 / N_KEYS))
    r0 = pl.multiple_of(j * K, K)
    pk_ref[0, pl.ds(r0, K), :] = i1
    pk_ref[1, pl.ds(r0, K), :] = expert - i1 * float(N_KEYS)
    pk_ref[2, pl.ds(r0, K), :] = 0.5 * gate

    npair = pt_ref.shape[-1]
    sub = lax.broadcasted_iota(jnp.int32, (N_KEYS, npair), 0).astype(F32)
    t0 = pl.multiple_of(j * GATE_GROUP, GATE_GROUP)
    for tl in range(GATE_GROUP):
        r = pl.ds(t0 + tl, 1)
        lhs = jnp.where(sub == pt_ref[0, r, :], pt_ref[2, r, :], 0.0).astype(BF16)
        rhs = jnp.where(sub == pt_ref[1, r, :], 1.0, 0.0).astype(BF16)
        scr_ref[pl.ds(tl, N_KEYS, stride=GATE_PITCH), :] = _nt_dot(lhs, rhs)
    for a in range(N_KEYS):
        g_ref[1 - slot, a, pl.ds(t0, GATE_GROUP), :] = (
            scr_ref[a * GATE_PITCH:a * GATE_PITCH + GATE_GROUP, :].astype(g_ref.dtype))

    h = h_ref[...]
    d = h.shape[-1]
    upd = None
    for k, v_ref in enumerate(v_refs):
        part = jnp.dot(coef_ref[:, k * ebs:(k + 1) * ebs], v_ref[:, :d], preferred_element_type=F32)
        upd = part if upd is None else upd + part
    acc_ref[...] += upd

    a0 = j * (eb // N_KEYS)
    pieces = []
    for q in range(eb // DENSE_SUB):
        u_ref, r0u = u_refs[q * DENSE_SUB // ebs], (q * DENSE_SUB) % ebs
        act = _nt_dot(h, u_ref[r0u:r0u + DENSE_SUB, :d])
        w = (act * (1.0 + lax.erf(act * (2.0 ** -0.5)))).astype(BF16)
        for a in range(DENSE_SUB // N_KEYS):
            ga = g_ref[slot, a0 + q * (DENSE_SUB // N_KEYS) + a]
            pieces.append(w[:, a * N_KEYS:(a + 1) * N_KEYS] * ga)
    coef_ref[...] = jnp.concatenate(pieces, axis=1)

    @pl.when(j == 0)
    def _():
        x2 = x_ref[...] + g2_ref[0] * acc_ref[...]
        if final:
            ms = jnp.mean(x2 * x2, axis=-1, keepdims=True)
            x2 = x2 * lax.rsqrt(ms + EPS) * fg_ref[...]
        o_ref[...] = x2
        acc_ref[...] = jnp.zeros_like(acc_ref)


def _peer_call(sct, h2, u, v, x1, g2, fg, final):
    bsz, s, d = x1.shape
    t = bsz * s
    ne = u.shape[0]
    _, nlist, nk, _ = sct.shape
    tb = min(TB_PEER, s)
    eb = ne // PEER_HEADS
    npair = PEER_HEADS * PEER_TOPK
    assert tb == PEER_HEADS * GATE_GROUP and nlist == 2 * PEER_HEADS and eb % MXU_WIDTH == 0
    nt = t // tb
    per_b = s // tb
    nchunk = PEER_HEADS
    dense_tile = lambda i: jnp.clip(i - 2, 0, nt - 1)
    row = lambda i, j: (dense_tile(i), 0)
    out_tile = lambda i, j: jnp.clip(i - 2 - (j == 0).astype(jnp.int32), 0, nt - 1)
    out_row = lambda i, j: (out_tile(i, j), 0)
    dp = d + LANES if (d // LANES) % 2 == 0 else d
    u = jnp.pad(u, ((0, 0), (0, dp - d)))
    v = jnp.pad(v, ((0, 0), (0, dp - d)))
    ebs = eb // PEER_SPLIT
    assert ebs % DENSE_SUB == 0
    u_spec = lambda k: pl.BlockSpec((ebs, dp), lambda i, j: (j * PEER_SPLIT + k, 0))
    v_spec = lambda k: pl.BlockSpec(
        (ebs, dp), lambda i, j: (((j + nchunk - 1) % nchunk) * PEER_SPLIT + k, 0))
    out = pl.pallas_call(
        functools.partial(_peer_kernel, final=final),
        grid=(nt + 3, nchunk),
        in_specs=[pl.BlockSpec((None, 2, nk, tb), lambda i, j: (jnp.minimum(i, nt - 1), j, 0, 0)),
                  pl.BlockSpec((tb, d), row),
                  *[u_spec(k) for k in range(PEER_SPLIT)],
                  *[v_spec(k) for k in range(PEER_SPLIT)],
                  pl.BlockSpec((tb, d), out_row),
                  pl.BlockSpec((1, 1, d), lambda i, j: (out_tile(i, j) // per_b, 0, 0)),
                  pl.BlockSpec((1, d), lambda i, j: (0, 0))],
        out_specs=pl.BlockSpec((tb, d), out_row),
        out_shape=jax.ShapeDtypeStruct((t, d), F32),
        scratch_shapes=[pltpu.VMEM((3, npair, tb), F32),
                        pltpu.VMEM((3, tb, npair), F32),
                        pltpu.VMEM((N_KEYS * GATE_PITCH, N_KEYS), F32),
                        pltpu.VMEM((2, N_KEYS, tb, N_KEYS), BF16),
                        pltpu.VMEM((tb, eb), BF16),
                        pltpu.VMEM((tb, d), F32)],
        compiler_params=_cparams(("arbitrary", "arbitrary")),
        name="peer_fused",
    )(sct, h2.reshape(t, d), *([u] * PEER_SPLIT), *([v] * PEER_SPLIT), x1.reshape(t, d), g2, fg)
    return out.reshape(bsz, s, d)


def kernel(x, c, ada_w, ada_b, mix_norm_g, ffn_norm_g, w_in, conv_w, conv_b, dt_bias, a_log, d_skip,
           ssd_norm_g, pool_w, pool_scale, w_out, peer_wq, peer_keys, peer_u, peer_v, final_norm_g):
    bsz, s, d = x.shape
    depth = ada_w.shape[0]
    ssd_w = ssd_norm_g.shape[-1]
    heads = dt_bias.shape[-1]
    xbc_w = conv_w.shape[-1]
    pool_wd = pool_scale.shape[-1]
    sb_w = (w_in.shape[-1] - ssd_w - xbc_w - heads - pool_wd) // 3
    o_z, o_xbc, o_dt, o_xp = 0, ssd_w, ssd_w + xbc_w, ssd_w + xbc_w + heads
    o_q = o_xp + pool_wd
    widths = (ssd_w, xbc_w, pool_wd, sb_w, sb_w, sb_w, LANES)

    mods = _mod_call(c, ada_w, ada_b)
    for l in range(depth):
        mod = [mods[l, :, k * d:(k + 1) * d].reshape(bsz, 1, d) for k in range(6)]
        sh1, sc1, g1, sh2, sc2, g2 = mod
        wl = w_in[l]
        w_cat = jnp.concatenate(
            [wl[:, o_z:o_xbc], wl[:, o_xbc:o_dt], wl[:, o_xp:o_q], wl[:, o_q:],
             jnp.pad(wl[:, o_dt:o_xp], ((0, 0), (0, LANES - heads)))], axis=1).astype(BF16)
        z, xbc, xp, q, k, v, dt = _in_call(x, sh1, sc1, mix_norm_g[l].reshape(1, d), w_cat, widths)
        y_ssd = _ssd_call(xbc, dt, z, conv_w[l], conv_b[l], dt_bias[l], a_log[l], d_skip[l],
                          ssd_norm_g[l])
        y_pool = _pool_call(xp, pool_w[l], pool_scale[l])
        y_sb = _sb_call(q, k, v)
        keys = peer_keys[l].reshape(-1, N_KEYS, PEER_HALF).astype(BF16)
        x1, h2, sct = _out_call(y_ssd, y_pool, y_sb, x, g1, sh2, sc2, ffn_norm_g[l].reshape(1, d),
                                w_out[l].astype(BF16), peer_wq[l].astype(BF16), keys)
        x = _peer_call(sct, h2, peer_u[l].astype(BF16), peer_v[l].astype(BF16), x1, g2,
                       final_norm_g.reshape(1, d), final=(l == depth - 1))
    return x
```

```python
import functools

import jax
import jax.numpy as jnp
from jax import lax
from jax.experimental import pallas as pl
from jax.experimental.pallas import tpu as pltpu

F32 = jnp.float32
BF16 = jnp.bfloat16
EPS = 1e-6
HIGHEST = lax.Precision.HIGHEST

SSD_HEAD_DIM = 64
SSD_GROUPS = 2
SSD_STATE = 64
CONV_WIDTH = 4
POOL_WINDOWS = (2, 4, 8, 16)
SB_HEAD_DIM = 64
PEER_HEADS = 8
PEER_TOPK = 16
N_KEYS = 128
PEER_HALF = 128

LANES = 128
SUBLANES = 8
MXU_WIDTH = 256
VMEM_LIMIT = 56 * 1024 * 1024

TM_IN = 512
SSD_CHUNK = 256
HALO = 16
TM_POOL = 512
SB_BLOCK = 128
SB_SKIP = 40.0
GATE_GROUP = 32
GATE_PITCH = 36
TB_PEER = 256
DENSE_SUB = 256
PEER_SPLIT = 1


def _cparams(sem):
    return pltpu.CompilerParams(dimension_semantics=sem, vmem_limit_bytes=VMEM_LIMIT)


def _nt_dot(a, b):
    return lax.dot_general(a, b, (((1,), (1,)), ((), ())), preferred_element_type=F32)


def _softplus(x):
    return jnp.maximum(x, 0.0) + jnp.log(1.0 + jnp.exp(-jnp.abs(x)))


def _silu(x):
    return x * jax.nn.sigmoid(x)


def _mod_kernel(c_ref, w_ref, b_ref, o_ref):
    cond = _silu(c_ref[...])
    o_ref[0] = jnp.dot(cond, w_ref[0], preferred_element_type=F32, precision=HIGHEST) + b_ref[0]


def _mod_call(c, ada_w, ada_b):
    nl, d, n6 = ada_w.shape
    bsz = c.shape[0]
    tn = 1536
    return pl.pallas_call(
        _mod_kernel,
        grid=(nl, n6 // tn),
        in_specs=[pl.BlockSpec((bsz, d), lambda l, j: (0, 0)),
                  pl.BlockSpec((1, d, tn), lambda l, j: (l, 0, j)),
                  pl.BlockSpec((1, 1, tn), lambda l, j: (l, 0, j))],
        out_specs=pl.BlockSpec((1, bsz, tn), lambda l, j: (l, 0, j)),
        out_shape=jax.ShapeDtypeStruct((nl, bsz, n6), F32),
        compiler_params=_cparams(("arbitrary", "arbitrary")),
        name="adaln_mod",
    )(c, ada_w, ada_b.reshape(nl, 1, n6))


def _norm_mod(x, g, sh, sc):
    ms = jnp.mean(x * x, axis=-1, keepdims=True)
    y = x * lax.rsqrt(ms + EPS) * g
    return y * (1.0 + sc) + sh


def _in_kernel(x_ref, sh_ref, sc_ref, g_ref, w_ref, z_ref, xbc_ref, xp_ref, q_ref, k_ref, v_ref,
               dt_ref, *, cuts):
    h = _norm_mod(x_ref[0], g_ref[...], sh_ref[0], sc_ref[0]).astype(BF16)
    outs = (z_ref, xbc_ref, xp_ref, q_ref, k_ref, v_ref, dt_ref)
    for o_ref, (a, b) in zip(outs, cuts):
        o_ref[0] = jnp.dot(h, w_ref[:, a:b], preferred_element_type=F32).astype(o_ref.dtype)


def _in_call(x, sh, sc, g, w_cat, widths):
    bsz, s, d = x.shape
    tm = min(TM_IN, s)
    cuts, a = [], 0
    for w in widths:
        cuts.append((a, a + w))
        a += w
    dtypes = (BF16, BF16, BF16, BF16, BF16, BF16, F32)
    tok = lambda w: pl.BlockSpec((1, tm, w), lambda b, i: (b, i, 0))
    vec = pl.BlockSpec((1, 1, d), lambda b, i: (b, 0, 0))
    return pl.pallas_call(
        functools.partial(_in_kernel, cuts=tuple(cuts)),
        grid=(bsz, s // tm),
        in_specs=[tok(d), vec, vec,
                  pl.BlockSpec((1, d), lambda b, i: (0, 0)),
                  pl.BlockSpec(w_cat.shape, lambda b, i: (0, 0))],
        out_specs=[tok(w) for w in widths],
        out_shape=[jax.ShapeDtypeStruct((bsz, s, w), dt) for w, dt in zip(widths, dtypes)],
        compiler_params=_cparams(("arbitrary", "arbitrary")),
        name="in_proj",
    )(x, sh, sc, g, w_cat)


def _ssd_kernel(xbc_ref, halo_ref, dt_ref, z_ref, cw_ref, cb_ref, dtb_ref, alog_ref, expand_ref,
                dsk_ref, ng_ref, y_ref, state_ref, *, chunk, width, gn):
    i = pl.program_id(1)
    L = chunk
    heads = width // SSD_HEAD_DIM
    hpg = heads // SSD_GROUPS
    half = width // SSD_GROUPS

    @pl.when(i == 0)
    def _():
        state_ref[...] = jnp.zeros_like(state_ref)

    hal = jnp.where(i > 0, halo_ref[0].astype(F32), 0.0)
    full = jnp.concatenate([hal, xbc_ref[0].astype(F32)], axis=0)
    cw = cw_ref[...]
    acc = full * cw[CONV_WIDTH - 1:CONV_WIDTH]
    for j in range(1, CONV_WIDTH):
        acc = acc + pltpu.roll(full, j, 0) * cw[CONV_WIDTH - 1 - j:CONV_WIDTH - j]
    xbc = _silu(acc[HALO:] + cb_ref[...])
    xs = xbc[:, :width]
    bm = xbc[:, width:width + gn]
    cm = xbc[:, width + gn:width + 2 * gn]

    dt = _softplus(dt_ref[0] + dtb_ref[...])
    a = dt * (-jnp.exp(alog_ref[...]))
    row_i = lax.broadcasted_iota(jnp.int32, (L, L), 0)
    col_i = lax.broadcasted_iota(jnp.int32, (L, L), 1)
    tril = row_i >= col_i
    a_cs = jnp.dot(tril.astype(F32), a, preferred_element_type=F32, precision=HIGHEST)
    a_cs_t = a_cs.T
    expand = expand_ref[...]
    dt_x = jnp.dot(dt, expand, preferred_element_type=F32, precision=HIGHEST)
    acs_x = jnp.dot(a_cs, expand, preferred_element_type=F32, precision=HIGHEST)
    alast_x = acs_x[L - 1:L, :]
    xdt = xs * dt_x

    bm_b = bm.astype(BF16)
    cm_b = cm.astype(BF16)
    lane_g = lax.broadcasted_iota(jnp.int32, (L, gn), 1) // SSD_STATE
    lane_h = lax.broadcasted_iota(jnp.int32, (L, half), 1) // SSD_HEAD_DIM

    state = state_ref[...]
    y_off = jnp.dot(cm_b, state.astype(BF16), preferred_element_type=F32) * jnp.exp(acs_x)

    y_halves = []
    for g in range(SSD_GROUPS):
        cb = _nt_dot(jnp.where(lane_g == g, cm_b, jnp.zeros_like(cm_b)), bm_b)
        xdt_g = xdt[:, g * half:(g + 1) * half]
        yh = jnp.zeros((L, half), F32)
        for hh in range(hpg):
            h = g * hpg + hh
            seg = a_cs[:, h:h + 1] - a_cs_t[h:h + 1, :]
            dec = jnp.exp(jnp.where(tril, seg, -1e30))
            m = (cb * dec).astype(BF16)
            rhs = jnp.where(lane_h == hh, xdt_g, 0.0).astype(BF16)
            yh = yh + jnp.dot(m, rhs, preferred_element_type=F32)
        y_halves.append(yh)

    ds_x = jnp.exp(alast_x - acs_x)
    upd = jnp.dot(bm.T.astype(BF16), (xdt * ds_x).astype(BF16), preferred_element_type=F32)
    srow = lax.broadcasted_iota(jnp.int32, (gn, width), 0) // SSD_STATE
    scol = lax.broadcasted_iota(jnp.int32, (gn, width), 1) // half
    state_ref[...] = jnp.exp(alast_x) * state + jnp.where(srow == scol, upd, 0.0)

    zg = _silu(z_ref[0].astype(F32))
    for g in range(SSD_GROUPS):
        sl = slice(g * half, (g + 1) * half)
        yg = (y_halves[g] + y_off[:, sl] + dsk_ref[:, sl] * xs[:, sl]) * zg[:, sl]
        ms = jnp.mean(yg * yg, axis=-1, keepdims=True)
        y_ref[0, :, sl] = (yg * lax.rsqrt(ms + EPS) * ng_ref[:, sl]).astype(y_ref.dtype)


def _ssd_call(xbc, dt, z, conv_w, conv_b, dt_bias, a_log, d_skip, ssd_norm_g):
    bsz, s, c = xbc.shape
    width = z.shape[-1]
    gn = SSD_GROUPS * SSD_STATE
    heads = width // SSD_HEAD_DIM
    L = min(SSD_CHUNK, s)
    pad = lambda v: jnp.zeros((1, LANES), F32).at[0, :heads].set(v)
    expand = (jnp.arange(LANES)[:, None] == (jnp.arange(width)[None, :] // SSD_HEAD_DIM)).astype(F32)
    dsk = jnp.repeat(d_skip, SSD_HEAD_DIM).reshape(1, width)
    hb = L // HALO
    const = lambda shp: pl.BlockSpec(shp, lambda b, i: (0, 0))
    return pl.pallas_call(
        functools.partial(_ssd_kernel, chunk=L, width=width, gn=gn),
        grid=(bsz, s // L),
        in_specs=[pl.BlockSpec((1, L, c), lambda b, i: (b, i, 0)),
                  pl.BlockSpec((1, HALO, c), lambda b, i: (b, jnp.maximum(i * hb - 1, 0), 0)),
                  pl.BlockSpec((1, L, LANES), lambda b, i: (b, i, 0)),
                  pl.BlockSpec((1, L, width), lambda b, i: (b, i, 0)),
                  const((CONV_WIDTH, c)), const((1, c)), const((1, LANES)), const((1, LANES)),
                  const((LANES, width)), const((1, width)), const((1, width))],
        out_specs=pl.BlockSpec((1, L, width), lambda b, i: (b, i, 0)),
        out_shape=jax.ShapeDtypeStruct((bsz, s, width), BF16),
        scratch_shapes=[pltpu.VMEM((gn, width), F32)],
        compiler_params=_cparams(("arbitrary", "arbitrary")),
        name="ssd_mixer",
    )(xbc, xbc, dt, z, conv_w, conv_b.reshape(1, c), pad(dt_bias), pad(a_log), expand, dsk,
      ssd_norm_g.reshape(1, width))


def _pool_kernel(xp_ref, halo_ref, w_ref, sc_ref, y_ref, *, tm, gdim):
    i = pl.program_id(1)
    hal = jnp.where(i > 0, halo_ref[0].astype(F32), 0.0)
    x = xp_ref[0].astype(F32)
    full = jnp.concatenate([hal, x], axis=0)
    lane_g = lax.broadcasted_iota(jnp.int32, x.shape, 1) // gdim
    tpos = i * tm + lax.broadcasted_iota(jnp.int32, x.shape, 0)
    win_sum = jnp.zeros_like(x)
    cnt = jnp.ones_like(x)
    s = full
    span = 1
    for gi, w in enumerate(POOL_WINDOWS):
        while span < w:
            s = s + pltpu.roll(s, span, 0)
            span *= 2
        win_sum = jnp.where(lane_g == gi, s[HALO:], win_sum)
        cnt = jnp.where(lane_g == gi, jnp.minimum(tpos + 1, w).astype(F32), cnt)
    pooled = (win_sum / cnt - x).astype(BF16)
    y = jnp.dot(pooled, w_ref[...], preferred_element_type=F32) * sc_ref[...]
    y_ref[0] = y.astype(y_ref.dtype)


def _pool_call(xp, pool_w, pool_scale):
    bsz, s, w = xp.shape
    ng, gdim, _ = pool_w.shape
    assert POOL_WINDOWS[-1] <= HALO and all(b == 2 * a for a, b in zip(POOL_WINDOWS, POOL_WINDOWS[1:]))
    tm = min(TM_POOL, s)
    wbd = jnp.zeros((w, w), F32)
    for g in range(ng):
        wbd = wbd.at[g * gdim:(g + 1) * gdim, g * gdim:(g + 1) * gdim].set(pool_w[g])
    hb = tm // HALO
    return pl.pallas_call(
        functools.partial(_pool_kernel, tm=tm, gdim=gdim),
        grid=(bsz, s // tm),
        in_specs=[pl.BlockSpec((1, tm, w), lambda b, i: (b, i, 0)),
                  pl.BlockSpec((1, HALO, w), lambda b, i: (b, jnp.maximum(i * hb - 1, 0), 0)),
                  pl.BlockSpec((w, w), lambda b, i: (0, 0)),
                  pl.BlockSpec((1, w), lambda b, i: (0, 0))],
        out_specs=pl.BlockSpec((1, tm, w), lambda b, i: (b, i, 0)),
        out_shape=jax.ShapeDtypeStruct((bsz, s, w), BF16),
        compiler_params=_cparams(("arbitrary", "arbitrary")),
        name="pool_mixer",
    )(xp, xp, wbd.astype(BF16), pool_scale.reshape(1, w))


def _sb_kernel(q_ref, k_ref, v_ref, y_ref, qm_ref, acc_ref, carry_ref, *, blk, heads):
    i = pl.program_id(1)
    T = blk
    w = q_ref.shape[-1]
    lane_h = lax.broadcasted_iota(jnp.int32, (T, w), 1) // SB_HEAD_DIM
    q = q_ref[0] * (SB_HEAD_DIM ** -0.5)
    for h in range(heads):
        qm_ref[h * T:(h + 1) * T, :] = jnp.where(lane_h == h, q, jnp.zeros_like(q))
    row_i = lax.broadcasted_iota(jnp.int32, (T, T), 0)
    col_i = lax.broadcasted_iota(jnp.int32, (T, T), 1)
    suffix = (row_i > col_i).astype(BF16)
    qrow = lax.broadcasted_iota(jnp.int32, (heads * T, T), 0) % T
    causal = lax.broadcasted_iota(jnp.int32, (heads * T, T), 1) < qrow

    def scores(j, mask):
        ks = k_ref[0, pl.ds(pl.multiple_of(j * T, T), T), :]
        z = _nt_dot(qm_ref[...], ks)
        lk = -_softplus(z)
        if mask is not None:
            lk = jnp.where(mask, lk, 0.0)
        hi = lk.astype(BF16)
        lo = (lk - hi.astype(F32)).astype(BF16)
        later = (jnp.dot(hi, suffix, preferred_element_type=F32)
                 + jnp.dot(lo, suffix, preferred_element_type=F32))
        return z, lk, later

    def weighted(j, z, lk, later, c, mask):
        vs = v_ref[0, pl.ds(pl.multiple_of(j * T, T), T), :]
        wgt = jnp.exp(z + lk + later + c)
        if mask is not None:
            wgt = jnp.where(mask, wgt, 0.0)
        pv = jnp.dot(wgt.astype(BF16), vs, preferred_element_type=F32)
        out = pv[0:T]
        for h in range(1, heads):
            out = jnp.where(lane_h == h, pv[h * T:(h + 1) * T], out)
        return out, c + jnp.sum(lk, axis=1, keepdims=True)

    def alive():
        return (jnp.max(carry_ref[...]) > -SB_SKIP).astype(jnp.int32)

    prev = jnp.maximum(i - 1, 0)
    has_prev = jnp.broadcast_to(i > 0, causal.shape)
    zd, lkd, ltd = scores(i, causal)
    zp, lkp, ltp = scores(prev, has_prev)
    out_d, c = weighted(i, zd, lkd, ltd, jnp.zeros((heads * T, 1), F32), causal)
    out_p, c = weighted(prev, zp, lkp, ltp, c, has_prev)
    acc_ref[...] = out_d + out_p
    carry_ref[...] = c

    def cond(st):
        return jnp.logical_and(st[0] >= 0, st[1] > 0)

    def body(st):
        j = st[0]
        z, lk, later = scores(j, None)
        out, c = weighted(j, z, lk, later, carry_ref[...], None)
        acc_ref[...] += out
        carry_ref[...] = c
        return j - 1, alive()

    lax.while_loop(cond, body, (i - 2, alive()))
    y_ref[0] = acc_ref[...].astype(y_ref.dtype)


def _sb_call(q, k, v):
    bsz, s, w = q.shape
    heads = w // SB_HEAD_DIM
    T = min(SB_BLOCK, s)
    return pl.pallas_call(
        functools.partial(_sb_kernel, blk=T, heads=heads),
        grid=(bsz, s // T),
        in_specs=[pl.BlockSpec((1, T, w), lambda b, i: (b, i, 0)),
                  pl.BlockSpec((1, s, w), lambda b, i: (b, 0, 0)),
                  pl.BlockSpec((1, s, w), lambda b, i: (b, 0, 0))],
        out_specs=pl.BlockSpec((1, T, w), lambda b, i: (b, i, 0)),
        out_shape=jax.ShapeDtypeStruct((bsz, s, w), BF16),
        scratch_shapes=[pltpu.VMEM((heads * T, w), BF16),
                        pltpu.VMEM((T, w), F32),
                        pltpu.VMEM((heads * T, 1), F32)],
        compiler_params=_cparams(("arbitrary", "arbitrary")),
        name="stick_breaking",
    )(q, k, v)


def _out_kernel(ys_ref, yp_ref, yb_ref, x_ref, g1_ref, sh_ref, sc_ref, ng_ref, wo_ref, wq_ref,
                keys_ref, x1_ref, h2_ref, sct_ref, *, cuts):
    mix = jnp.zeros(x_ref.shape[1:], F32)
    for y_ref, (a, b) in zip((ys_ref, yp_ref, yb_ref), cuts):
        mix = mix + jnp.dot(y_ref[0], wo_ref[a:b, :], preferred_element_type=F32)
    x1 = x_ref[0] + g1_ref[0] * mix
    x1_ref[0] = x1
    h2 = _norm_mod(x1, ng_ref[...], sh_ref[0], sc_ref[0]).astype(BF16)
    h2_ref[0] = h2
    qb = jnp.dot(h2, wq_ref[...], preferred_element_type=F32).astype(BF16)
    tp = sct_ref.shape[-1]
    for lst in range(keys_ref.shape[0]):
        sc_t = _nt_dot(keys_ref[lst], qb[:, lst * PEER_HALF:(lst + 1) * PEER_HALF])
        for k in range(sct_ref.shape[0]):
            sct_ref[k, lst] = sc_t[:, k * tp:(k + 1) * tp]


def _out_call(ys, yp, yb, x, g1, sh, sc, ng, w_out, wq, keys):
    bsz, s, d = x.shape
    tm = min(TM_IN, s)
    nlist = keys.shape[0]
    cuts, a = [], 0
    for y in (ys, yp, yb):
        cuts.append((a, a + y.shape[-1]))
        a += y.shape[-1]
    tok = lambda w: pl.BlockSpec((1, tm, w), lambda b, i: (b, i, 0))
    vec = pl.BlockSpec((1, 1, d), lambda b, i: (b, 0, 0))
    full2 = lambda arr: pl.BlockSpec(arr.shape, lambda b, i: (0,) * arr.ndim)
    nblk = s // tm
    tp = min(TB_PEER, s)
    per = tm // tp
    return pl.pallas_call(
        functools.partial(_out_kernel, cuts=tuple(cuts)),
        grid=(bsz, nblk),
        in_specs=[tok(ys.shape[-1]), tok(yp.shape[-1]), tok(yb.shape[-1]), tok(d), vec, vec, vec,
                  pl.BlockSpec((1, d), lambda b, i: (0, 0)), full2(w_out), full2(wq), full2(keys)],
        out_specs=[tok(d), tok(d),
                   pl.BlockSpec((per, nlist, N_KEYS, tp), lambda b, i: (b * nblk + i, 0, 0, 0))],
        out_shape=[jax.ShapeDtypeStruct((bsz, s, d), F32),
                   jax.ShapeDtypeStruct((bsz, s, d), BF16),
                   jax.ShapeDtypeStruct((bsz * s // tp, nlist, N_KEYS, tp), F32)],
        compiler_params=_cparams(("arbitrary", "arbitrary")),
        name="out_proj_peer_query",
    )(ys, yp, yb, x, g1, sh, sc, ng, w_out, wq, keys)


def _top_rows(s, k, payload=None):
    nrows = s.shape[0]
    rows = lax.broadcasted_iota(jnp.int32, s.shape, 0).astype(F32)
    vals, picks = [], []
    for _ in range(k):
        m = jnp.max(s, axis=0, keepdims=True)
        idx = jnp.min(jnp.where(s == m, rows, float(nrows)), axis=0, keepdims=True)
        hit = rows == idx
        vals.append(m)
        if payload is None:
            picks.append(idx)
        else:
            picks.append(jnp.sum(jnp.where(hit, payload, 0.0), axis=0, keepdims=True))
        s = jnp.where(hit, -jnp.inf, s)
    return jnp.concatenate(vals, axis=0), jnp.concatenate(picks, axis=0)


def _pair_candidates(v1, x1, v2, x2):
    K = PEER_TOPK
    cands, ids = [], []
    a = 0
    while K // (a + 1) > 1:
        n = K // (a + 1)
        npad = -(-n // SUBLANES) * SUBLANES
        c = v1[a:a + 1, :] + v2[0:npad, :]
        if n < npad:
            c = jnp.where(lax.broadcasted_iota(jnp.int32, c.shape, 0) < n, c, -jnp.inf)
        cands.append(c)
        ids.append(x1[a:a + 1, :] * float(N_KEYS) + x2[0:npad, :])
        a += 1
    cands.append(v1[a:K, :] + v2[0:1, :])
    ids.append(x1[a:K, :] * float(N_KEYS) + x2[0:1, :])
    return jnp.concatenate(cands, axis=0), jnp.concatenate(ids, axis=0)


def _peer_kernel(sct_ref, h_ref, *refs, final):
    u_refs, v_refs = refs[:PEER_SPLIT], refs[PEER_SPLIT:2 * PEER_SPLIT]
    (x_ref, g2_ref, fg_ref, o_ref,
     pk_ref, pt_ref, scr_ref, g_ref, coef_ref, acc_ref) = refs[2 * PEER_SPLIT:]
    _peer_body(sct_ref, h_ref, u_refs, v_refs, x_ref, g2_ref, fg_ref, o_ref,
               pk_ref, pt_ref, scr_ref, g_ref, coef_ref, acc_ref, final=final)


def _peer_body(sct_ref, h_ref, u_refs, v_refs, x_ref, g2_ref, fg_ref, o_ref,
               pk_ref, pt_ref, scr_ref, g_ref, coef_ref, acc_ref, *, final):
    s = pl.program_id(0)
    j = pl.program_id(1)
    K = PEER_TOPK
    slot = s % 2
    ebs = v_refs[0].shape[0]
    eb = ebs * len(v_refs)

    @pl.when(jnp.logical_and(s == 0, j == 0))
    def _():
        pk_ref[...] = jnp.zeros_like(pk_ref)
        scr_ref[...] = jnp.zeros_like(scr_ref)
        g_ref[...] = jnp.zeros_like(g_ref)
        coef_ref[...] = jnp.zeros_like(coef_ref)
        acc_ref[...] = jnp.zeros_like(acc_ref)

    @pl.when(j == 0)
    def _():
        for k in range(pk_ref.shape[0]):
            pt_ref[k] = pk_ref[k].T

    v1, x1 = _top_rows(sct_ref[0], K)
    v2, x2 = _top_rows(sct_ref[1], K)
    cand, ids = _pair_candidates(v1, x1, v2, x2)
    best, expert = _top_rows(cand, K, payload=ids)
    e = jnp.exp(best - best[0:1, :])
    gate = e / jnp.sum(e, axis=0, keepdims=True)
    i1 = jnp.floor(expert * (1.0 / N_KEYS))
    r0 = pl.multiple_of(j * K, K)
    pk_ref[0, pl.ds(r0, K), :] = i1
    pk_ref[1, pl.ds(r0, K), :] = expert - i1 * float(N_KEYS)
    pk_ref[2, pl.ds(r0, K), :] = 0.5 * gate

    npair = pt_ref.shape[-1]
    sub = lax.broadcasted_iota(jnp.int32, (N_KEYS, npair), 0).astype(F32)
    t0 = pl.multiple_of(j * GATE_GROUP, GATE_GROUP)
    for tl in range(GATE_GROUP):
        r = pl.ds(t0 + tl, 1)
        lhs = jnp.where(sub == pt_ref[0, r, :], pt_ref[2, r, :], 0.0).astype(BF16)
        rhs = jnp.where(sub == pt_ref[1, r, :], 1.0, 0.0).astype(BF16)
        scr_ref[pl.ds(tl, N_KEYS, stride=GATE_PITCH), :] = _nt_dot(lhs, rhs)
    for a in range(N_KEYS):
        g_ref[1 - slot, a, pl.ds(t0, GATE_GROUP), :] = (
            scr_ref[a * GATE_PITCH:a * GATE_PITCH + GATE_GROUP, :].astype(g_ref.dtype))

    h = h_ref[...]
    d = h.shape[-1]
    upd = None
    for k, v_ref in enumerate(v_refs):
        part = jnp.dot(coef_ref[:, k * ebs:(k + 1) * ebs], v_ref[:, :d], preferred_element_type=F32)
        upd = part if upd is None else upd + part
    acc_ref[...] += upd

    a0 = j * (eb // N_KEYS)
    pieces = []
    for q in range(eb // DENSE_SUB):
        act = jnp.dot(h, u_refs[0][:, q * DENSE_SUB:(q + 1) * DENSE_SUB],
                      preferred_element_type=F32)
        w = (act * (1.0 + lax.erf(act * (2.0 ** -0.5)))).astype(BF16)
        for a in range(DENSE_SUB // N_KEYS):
            ga = g_ref[slot, a0 + q * (DENSE_SUB // N_KEYS) + a]
            pieces.append(w[:, a * N_KEYS:(a + 1) * N_KEYS] * ga)
    coef_ref[...] = jnp.concatenate(pieces, axis=1)

    @pl.when(j == 0)
    def _():
        x2 = x_ref[...] + g2_ref[0] * acc_ref[...]
        if final:
            ms = jnp.mean(x2 * x2, axis=-1, keepdims=True)
            x2 = x2 * lax.rsqrt(ms + EPS) * fg_ref[...]
        o_ref[...] = x2
        acc_ref[...] = jnp.zeros_like(acc_ref)


def _peer_call(sct, h2, u, v, x1, g2, fg, final):
    bsz, s, d = x1.shape
    t = bsz * s
    ne = u.shape[0]
    _, nlist, nk, _ = sct.shape
    tb = min(TB_PEER, s)
    eb = ne // PEER_HEADS
    npair = PEER_HEADS * PEER_TOPK
    assert tb == PEER_HEADS * GATE_GROUP and nlist == 2 * PEER_HEADS and eb % MXU_WIDTH == 0
    nt = t // tb
    per_b = s // tb
    nchunk = PEER_HEADS
    dense_tile = lambda i: jnp.clip(i - 2, 0, nt - 1)
    row = lambda i, j: (dense_tile(i), 0)
    out_tile = lambda i, j: jnp.clip(i - 2 - (j == 0).astype(jnp.int32), 0, nt - 1)
    out_row = lambda i, j: (out_tile(i, j), 0)
    dp = d + LANES if (d // LANES) % 2 == 0 else d
    u = jnp.pad(u, ((0, 0), (0, dp - d)))
    v = jnp.pad(v, ((0, 0), (0, dp - d)))
    ebs = eb // PEER_SPLIT
    assert ebs % DENSE_SUB == 0
    u = jnp.pad(u[:, :d].T.reshape(d, nchunk, eb), ((0, 0), (0, 0), (0, LANES)))
    u = u.reshape(d, nchunk * (eb + LANES))
    u_spec = lambda k: pl.BlockSpec((d, eb + LANES), lambda i, j: (0, j))
    v_spec = lambda k: pl.BlockSpec(
        (ebs, dp), lambda i, j: (((j + nchunk - 1) % nchunk) * PEER_SPLIT + k, 0))
    out = pl.pallas_call(
        functools.partial(_peer_kernel, final=final),
        grid=(nt + 3, nchunk),
        in_specs=[pl.BlockSpec((None, 2, nk, tb), lambda i, j: (jnp.minimum(i, nt - 1), j, 0, 0)),
                  pl.BlockSpec((tb, d), row),
                  *[u_spec(k) for k in range(PEER_SPLIT)],
                  *[v_spec(k) for k in range(PEER_SPLIT)],
                  pl.BlockSpec((tb, d), out_row),
                  pl.BlockSpec((1, 1, d), lambda i, j: (out_tile(i, j) // per_b, 0, 0)),
                  pl.BlockSpec((1, d), lambda i, j: (0, 0))],
        out_specs=pl.BlockSpec((tb, d), out_row),
        out_shape=jax.ShapeDtypeStruct((t, d), F32),
        scratch_shapes=[pltpu.VMEM((3, npair, tb), F32),
                        pltpu.VMEM((3, tb, npair), F32),
                        pltpu.VMEM((N_KEYS * GATE_PITCH, N_KEYS), F32),
                        pltpu.VMEM((2, N_KEYS, tb, N_KEYS), BF16),
                        pltpu.VMEM((tb, eb), BF16),
                        pltpu.VMEM((tb, d), F32)],
        compiler_params=_cparams(("arbitrary", "arbitrary")),
        name="peer_fused",
    )(sct, h2.reshape(t, d), *([u] * PEER_SPLIT), *([v] * PEER_SPLIT), x1.reshape(t, d), g2, fg)
    return out.reshape(bsz, s, d)


def kernel(x, c, ada_w, ada_b, mix_norm_g, ffn_norm_g, w_in, conv_w, conv_b, dt_bias, a_log, d_skip,
           ssd_norm_g, pool_w, pool_scale, w_out, peer_wq, peer_keys, peer_u, peer_v, final_norm_g):
    bsz, s, d = x.shape
    depth = ada_w.shape[0]
    ssd_w = ssd_norm_g.shape[-1]
    heads = dt_bias.shape[-1]
    xbc_w = conv_w.shape[-1]
    pool_wd = pool_scale.shape[-1]
    sb_w = (w_in.shape[-1] - ssd_w - xbc_w - heads - pool_wd) // 3
    o_z, o_xbc, o_dt, o_xp = 0, ssd_w, ssd_w + xbc_w, ssd_w + xbc_w + heads
    o_q = o_xp + pool_wd
    widths = (ssd_w, xbc_w, pool_wd, sb_w, sb_w, sb_w, LANES)

    mods = _mod_call(c, ada_w, ada_b)
    for l in range(depth):
        mod = [mods[l, :, k * d:(k + 1) * d].reshape(bsz, 1, d) for k in range(6)]
        sh1, sc1, g1, sh2, sc2, g2 = mod
        wl = w_in[l]
        w_cat = jnp.concatenate(
            [wl[:, o_z:o_xbc], wl[:, o_xbc:o_dt], wl[:, o_xp:o_q], wl[:, o_q:],
             jnp.pad(wl[:, o_dt:o_xp], ((0, 0), (0, LANES - heads)))], axis=1).astype(BF16)
        z, xbc, xp, q, k, v, dt = _in_call(x, sh1, sc1, mix_norm_g[l].reshape(1, d), w_cat, widths)
        y_ssd = _ssd_call(xbc, dt, z, conv_w[l], conv_b[l], dt_bias[l], a_log[l], d_skip[l],
                          ssd_norm_g[l])
        y_pool = _pool_call(xp, pool_w[l], pool_scale[l])
        y_sb = _sb_call(q, k, v)
        keys = peer_keys[l].reshape(-1, N_KEYS, PEER_HALF).astype(BF16)
        x1, h2, sct = _out_call(y_ssd, y_pool, y_sb, x, g1, sh2, sc2, ffn_norm_g[l].reshape(1, d),
                                w_out[l].astype(BF16), peer_wq[l].astype(BF16), keys)
        x = _peer_call(sct, h2, peer_u[l].astype(BF16), peer_v[l].astype(BF16), x1, g2,
                       final_norm_g.reshape(1, d), final=(l == depth - 1))
    return x
```

```python
import functools

import jax
import jax.numpy as jnp
from jax import lax
from jax.experimental import pallas as pl
from jax.experimental.pallas import tpu as pltpu

F32 = jnp.float32
BF16 = jnp.bfloat16
EPS = 1e-6
HIGHEST = lax.Precision.HIGHEST

SSD_HEAD_DIM = 64
SSD_GROUPS = 2
SSD_STATE = 64
CONV_WIDTH = 4
POOL_WINDOWS = (2, 4, 8, 16)
SB_HEAD_DIM = 64
PEER_HEADS = 8
PEER_TOPK = 16
N_KEYS = 128
PEER_HALF = 128

LANES = 128
SUBLANES = 8
MXU_WIDTH = 256
VMEM_LIMIT = 56 * 1024 * 1024

TM_IN = 512
SSD_CHUNK = 256
HALO = 16
TM_POOL = 512
SB_BLOCK = 128
SB_SKIP = 40.0
SB_QBLOCKS = 2
GATE_GROUP = 32
GATE_PITCH = 36
TB_PEER = 256
DENSE_SUB = 256


def _cparams(sem):
    return pltpu.CompilerParams(dimension_semantics=sem, vmem_limit_bytes=VMEM_LIMIT)


def _nt_dot(a, b):
    return lax.dot_general(a, b, (((1,), (1,)), ((), ())), preferred_element_type=F32)


def _softplus(x):
    return jnp.maximum(x, 0.0) + jnp.log(1.0 + jnp.exp(-jnp.abs(x)))


def _silu(x):
    return x * jax.nn.sigmoid(x)


def _mod_kernel(c_ref, w_ref, b_ref, o_ref):
    cond = _silu(c_ref[...])
    o_ref[0] = jnp.dot(cond, w_ref[0], preferred_element_type=F32, precision=HIGHEST) + b_ref[0]


def _mod_call(c, ada_w, ada_b):
    nl, d, n6 = ada_w.shape
    bsz = c.shape[0]
    tn = 1536
    return pl.pallas_call(
        _mod_kernel,
        grid=(nl, n6 // tn),
        in_specs=[pl.BlockSpec((bsz, d), lambda l, j: (0, 0)),
                  pl.BlockSpec((1, d, tn), lambda l, j: (l, 0, j)),
                  pl.BlockSpec((1, 1, tn), lambda l, j: (l, 0, j))],
        out_specs=pl.BlockSpec((1, bsz, tn), lambda l, j: (l, 0, j)),
        out_shape=jax.ShapeDtypeStruct((nl, bsz, n6), F32),
        compiler_params=_cparams(("arbitrary", "arbitrary")),
        name="adaln_mod",
    )(c, ada_w, ada_b.reshape(nl, 1, n6))


def _norm_mod(x, g, sh, sc):
    ms = jnp.mean(x * x, axis=-1, keepdims=True)
    y = x * lax.rsqrt(ms + EPS) * g
    return y * (1.0 + sc) + sh


def _in_kernel(x_ref, sh_ref, sc_ref, g_ref, w_ref, z_ref, xbc_ref, xp_ref, q_ref, k_ref, v_ref,
               dt_ref, *, cuts):
    h = _norm_mod(x_ref[0], g_ref[...], sh_ref[0], sc_ref[0]).astype(BF16)
    outs = (z_ref, xbc_ref, xp_ref, q_ref, k_ref, v_ref, dt_ref)
    for o_ref, (a, b) in zip(outs, cuts):
        o_ref[0] = jnp.dot(h, w_ref[:, a:b], preferred_element_type=F32).astype(o_ref.dtype)


def _in_call(x, sh, sc, g, w_cat, widths):
    bsz, s, d = x.shape
    tm = min(TM_IN, s)
    cuts, a = [], 0
    for w in widths:
        cuts.append((a, a + w))
        a += w
    dtypes = (BF16, BF16, BF16, BF16, BF16, BF16, F32)
    tok = lambda w: pl.BlockSpec((1, tm, w), lambda b, i: (b, i, 0))
    vec = pl.BlockSpec((1, 1, d), lambda b, i: (b, 0, 0))
    return pl.pallas_call(
        functools.partial(_in_kernel, cuts=tuple(cuts)),
        grid=(bsz, s // tm),
        in_specs=[tok(d), vec, vec,
                  pl.BlockSpec((1, d), lambda b, i: (0, 0)),
                  pl.BlockSpec(w_cat.shape, lambda b, i: (0, 0))],
        out_specs=[tok(w) for w in widths],
        out_shape=[jax.ShapeDtypeStruct((bsz, s, w), dt) for w, dt in zip(widths, dtypes)],
        compiler_params=_cparams(("arbitrary", "arbitrary")),
        name="in_proj",
    )(x, sh, sc, g, w_cat)


def _ssd_kernel(xbc_ref, halo_ref, dt_ref, z_ref, cw_ref, cb_ref, dtb_ref, alog_ref, expand_ref,
                dsk_ref, ng_ref, y_ref, state_ref, *, chunk, width, gn):
    i = pl.program_id(1)
    L = chunk
    heads = width // SSD_HEAD_DIM
    hpg = heads // SSD_GROUPS
    half = width // SSD_GROUPS

    @pl.when(i == 0)
    def _():
        state_ref[...] = jnp.zeros_like(state_ref)

    hal = jnp.where(i > 0, halo_ref[0].astype(F32), 0.0)
    full = jnp.concatenate([hal, xbc_ref[0].astype(F32)], axis=0)
    cw = cw_ref[...]
    acc = full * cw[CONV_WIDTH - 1:CONV_WIDTH]
    for j in range(1, CONV_WIDTH):
        acc = acc + pltpu.roll(full, j, 0) * cw[CONV_WIDTH - 1 - j:CONV_WIDTH - j]
    xbc = _silu(acc[HALO:] + cb_ref[...])
    xs = xbc[:, :width]
    bm = xbc[:, width:width + gn]
    cm = xbc[:, width + gn:width + 2 * gn]

    dt = _softplus(dt_ref[0] + dtb_ref[...])
    a = dt * (-jnp.exp(alog_ref[...]))
    row_i = lax.broadcasted_iota(jnp.int32, (L, L), 0)
    col_i = lax.broadcasted_iota(jnp.int32, (L, L), 1)
    tril = row_i >= col_i
    a_cs = jnp.dot(tril.astype(F32), a, preferred_element_type=F32, precision=HIGHEST)
    a_cs_t = a_cs.T
    expand = expand_ref[...]
    dt_x = jnp.dot(dt, expand, preferred_element_type=F32, precision=HIGHEST)
    acs_x = jnp.dot(a_cs, expand, preferred_element_type=F32, precision=HIGHEST)
    alast_x = acs_x[L - 1:L, :]
    xdt = xs * dt_x

    bm_b = bm.astype(BF16)
    cm_b = cm.astype(BF16)
    lane_g = lax.broadcasted_iota(jnp.int32, (L, gn), 1) // SSD_STATE
    lane_h = lax.broadcasted_iota(jnp.int32, (L, half), 1) // SSD_HEAD_DIM

    state = state_ref[...]
    y_off = jnp.dot(cm_b, state.astype(BF16), preferred_element_type=F32) * jnp.exp(acs_x)

    y_halves = []
    for g in range(SSD_GROUPS):
        cb = _nt_dot(jnp.where(lane_g == g, cm_b, jnp.zeros_like(cm_b)), bm_b)
        xdt_g = xdt[:, g * half:(g + 1) * half]
        yh = jnp.zeros((L, half), F32)
        for hh in range(hpg):
            h = g * hpg + hh
            seg = a_cs[:, h:h + 1] - a_cs_t[h:h + 1, :]
            dec = jnp.exp(jnp.where(tril, seg, -1e30))
            m = (cb * dec).astype(BF16)
            rhs = jnp.where(lane_h == hh, xdt_g, 0.0).astype(BF16)
            yh = yh + jnp.dot(m, rhs, preferred_element_type=F32)
        y_halves.append(yh)

    ds_x = jnp.exp(alast_x - acs_x)
    upd = jnp.dot(bm.T.astype(BF16), (xdt * ds_x).astype(BF16), preferred_element_type=F32)
    srow = lax.broadcasted_iota(jnp.int32, (gn, width), 0) // SSD_STATE
    scol = lax.broadcasted_iota(jnp.int32, (gn, width), 1) // half
    state_ref[...] = jnp.exp(alast_x) * state + jnp.where(srow == scol, upd, 0.0)

    zg = _silu(z_ref[0].astype(F32))
    for g in range(SSD_GROUPS):
        sl = slice(g * half, (g + 1) * half)
        yg = (y_halves[g] + y_off[:, sl] + dsk_ref[:, sl] * xs[:, sl]) * zg[:, sl]
        ms = jnp.mean(yg * yg, axis=-1, keepdims=True)
        y_ref[0, :, sl] = (yg * lax.rsqrt(ms + EPS) * ng_ref[:, sl]).astype(y_ref.dtype)


def _ssd_call(xbc, dt, z, conv_w, conv_b, dt_bias, a_log, d_skip, ssd_norm_g):
    bsz, s, c = xbc.shape
    width = z.shape[-1]
    gn = SSD_GROUPS * SSD_STATE
    heads = width // SSD_HEAD_DIM
    L = min(SSD_CHUNK, s)
    pad = lambda v: jnp.zeros((1, LANES), F32).at[0, :heads].set(v)
    expand = (jnp.arange(LANES)[:, None] == (jnp.arange(width)[None, :] // SSD_HEAD_DIM)).astype(F32)
    dsk = jnp.repeat(d_skip, SSD_HEAD_DIM).reshape(1, width)
    hb = L // HALO
    const = lambda shp: pl.BlockSpec(shp, lambda b, i: (0, 0))
    return pl.pallas_call(
        functools.partial(_ssd_kernel, chunk=L, width=width, gn=gn),
        grid=(bsz, s // L),
        in_specs=[pl.BlockSpec((1, L, c), lambda b, i: (b, i, 0)),
                  pl.BlockSpec((1, HALO, c), lambda b, i: (b, jnp.maximum(i * hb - 1, 0), 0)),
                  pl.BlockSpec((1, L, LANES), lambda b, i: (b, i, 0)),
                  pl.BlockSpec((1, L, width), lambda b, i: (b, i, 0)),
                  const((CONV_WIDTH, c)), const((1, c)), const((1, LANES)), const((1, LANES)),
                  const((LANES, width)), const((1, width)), const((1, width))],
        out_specs=pl.BlockSpec((1, L, width), lambda b, i: (b, i, 0)),
        out_shape=jax.ShapeDtypeStruct((bsz, s, width), BF16),
        scratch_shapes=[pltpu.VMEM((gn, width), F32)],
        compiler_params=_cparams(("arbitrary", "arbitrary")),
        name="ssd_mixer",
    )(xbc, xbc, dt, z, conv_w, conv_b.reshape(1, c), pad(dt_bias), pad(a_log), expand, dsk,
      ssd_norm_g.reshape(1, width))


def _pool_kernel(xp_ref, halo_ref, w_ref, sc_ref, y_ref, *, tm, gdim):
    i = pl.program_id(1)
    hal = jnp.where(i > 0, halo_ref[0].astype(F32), 0.0)
    x = xp_ref[0].astype(F32)
    full = jnp.concatenate([hal, x], axis=0)
    lane_g = lax.broadcasted_iota(jnp.int32, x.shape, 1) // gdim
    tpos = i * tm + lax.broadcasted_iota(jnp.int32, x.shape, 0)
    win_sum = jnp.zeros_like(x)
    cnt = jnp.ones_like(x)
    s = full
    span = 1
    for gi, w in enumerate(POOL_WINDOWS):
        while span < w:
            s = s + pltpu.roll(s, span, 0)
            span *= 2
        win_sum = jnp.where(lane_g == gi, s[HALO:], win_sum)
        cnt = jnp.where(lane_g == gi, jnp.minimum(tpos + 1, w).astype(F32), cnt)
    pooled = (win_sum / cnt - x).astype(BF16)
    y = jnp.dot(pooled, w_ref[...], preferred_element_type=F32) * sc_ref[...]
    y_ref[0] = y.astype(y_ref.dtype)


def _pool_call(xp, pool_w, pool_scale):
    bsz, s, w = xp.shape
    ng, gdim, _ = pool_w.shape
    assert POOL_WINDOWS[-1] <= HALO and all(b == 2 * a for a, b in zip(POOL_WINDOWS, POOL_WINDOWS[1:]))
    tm = min(TM_POOL, s)
    wbd = jnp.zeros((w, w), F32)
    for g in range(ng):
        wbd = wbd.at[g * gdim:(g + 1) * gdim, g * gdim:(g + 1) * gdim].set(pool_w[g])
    hb = tm // HALO
    return pl.pallas_call(
        functools.partial(_pool_kernel, tm=tm, gdim=gdim),
        grid=(bsz, s // tm),
        in_specs=[pl.BlockSpec((1, tm, w), lambda b, i: (b, i, 0)),
                  pl.BlockSpec((1, HALO, w), lambda b, i: (b, jnp.maximum(i * hb - 1, 0), 0)),
                  pl.BlockSpec((w, w), lambda b, i: (0, 0)),
                  pl.BlockSpec((1, w), lambda b, i: (0, 0))],
        out_specs=pl.BlockSpec((1, tm, w), lambda b, i: (b, i, 0)),
        out_shape=jax.ShapeDtypeStruct((bsz, s, w), BF16),
        compiler_params=_cparams(("arbitrary", "arbitrary")),
        name="pool_mixer",
    )(xp, xp, wbd.astype(BF16), pool_scale.reshape(1, w))


def _sb_kernel(q_ref, k_ref, v_ref, y_ref, qm_ref, acc_ref, carry_ref, *, blk, heads, nq):
    i = pl.program_id(1)
    T = blk
    w = q_ref.shape[-1]
    lane_h = lax.broadcasted_iota(jnp.int32, (T, w), 1) // SB_HEAD_DIM
    for u in range(nq):
        q = q_ref[0, u * T:(u + 1) * T, :] * (SB_HEAD_DIM ** -0.5)
        for h in range(heads):
            qm_ref[u, h * T:(h + 1) * T, :] = jnp.where(lane_h == h, q, jnp.zeros_like(q))
    row_i = lax.broadcasted_iota(jnp.int32, (T, T), 0)
    col_i = lax.broadcasted_iota(jnp.int32, (T, T), 1)
    suffix = (row_i > col_i).astype(BF16)
    qrow = lax.broadcasted_iota(jnp.int32, (heads * T, T), 0) % T
    causal = lax.broadcasted_iota(jnp.int32, (heads * T, T), 1) < qrow

    def scores(u, j, mask):
        ks = k_ref[0, pl.ds(pl.multiple_of(j * T, T), T), :]
        z = _nt_dot(qm_ref[u], ks)
        lk = -_softplus(z)
        if mask is not None:
            lk = jnp.where(mask, lk, 0.0)
        hi = lk.astype(BF16)
        lo = (lk - hi.astype(F32)).astype(BF16)
        later = (jnp.dot(hi, suffix, preferred_element_type=F32)
                 + jnp.dot(lo, suffix, preferred_element_type=F32))
        return z, lk, later

    def weighted(j, z, lk, later, c, mask):
        vs = v_ref[0, pl.ds(pl.multiple_of(j * T, T), T), :]
        wgt = jnp.exp(z + lk + later + c)
        if mask is not None:
            wgt = jnp.where(mask, wgt, 0.0)
        pv = jnp.dot(wgt.astype(BF16), vs, preferred_element_type=F32)
        out = pv[0:T]
        for h in range(1, heads):
            out = jnp.where(lane_h == h, pv[h * T:(h + 1) * T], out)
        return out, c + jnp.sum(lk, axis=1, keepdims=True)

    def alive():
        return (jnp.max(carry_ref[...]) > -SB_SKIP).astype(jnp.int32)

    first = []
    for u in range(nq):
        qb = i * nq + u
        prev = jnp.maximum(qb - 1, 0)
        has_prev = None if u > 0 else jnp.broadcast_to(qb > 0, causal.shape)
        first.append((qb, prev, has_prev, scores(u, qb, causal), scores(u, prev, has_prev)))
    for u, (qb, prev, has_prev, sd, sp) in enumerate(first):
        out_d, c = weighted(qb, *sd, jnp.zeros((heads * T, 1), F32), causal)
        out_p, c = weighted(prev, *sp, c, has_prev)
        acc_ref[u] = out_d + out_p
        carry_ref[u] = c

    def cond(st):
        return jnp.logical_and(i * nq + nq - 3 - st[0] >= 0, st[1] > 0)

    def body(st):
        n = st[0]
        for u in range(nq):
            j = i * nq + u - 2 - n
            valid = jnp.broadcast_to(j >= 0, causal.shape)
            jj = jnp.maximum(j, 0)
            out, c = weighted(jj, *scores(u, jj, valid), carry_ref[u], valid)
            acc_ref[u] += out
            carry_ref[u] = c
        return n + 1, alive()

    lax.while_loop(cond, body, (0, alive()))
    for u in range(nq):
        y_ref[0, u * T:(u + 1) * T, :] = acc_ref[u].astype(y_ref.dtype)


def _sb_call(q, k, v):
    bsz, s, w = q.shape
    heads = w // SB_HEAD_DIM
    T = min(SB_BLOCK, s)
    nq = SB_QBLOCKS if (s // T) % SB_QBLOCKS == 0 else 1
    return pl.pallas_call(
        functools.partial(_sb_kernel, blk=T, heads=heads, nq=nq),
        grid=(bsz, s // (nq * T)),
        in_specs=[pl.BlockSpec((1, nq * T, w), lambda b, i: (b, i, 0)),
                  pl.BlockSpec((1, s, w), lambda b, i: (b, 0, 0)),
                  pl.BlockSpec((1, s, w), lambda b, i: (b, 0, 0))],
        out_specs=pl.BlockSpec((1, nq * T, w), lambda b, i: (b, i, 0)),
        out_shape=jax.ShapeDtypeStruct((bsz, s, w), BF16),
        scratch_shapes=[pltpu.VMEM((nq, heads * T, w), BF16),
                        pltpu.VMEM((nq, T, w), F32),
                        pltpu.VMEM((nq, heads * T, 1), F32)],
        compiler_params=_cparams(("arbitrary", "arbitrary")),
        name="stick_breaking",
    )(q, k, v)


def _out_kernel(ys_ref, yp_ref, yb_ref, x_ref, g1_ref, sh_ref, sc_ref, ng_ref, wo_ref, wq_ref,
                keys_ref, x1_ref, h2_ref, sct_ref, *, cuts):
    mix = jnp.zeros(x_ref.shape[1:], F32)
    for y_ref, (a, b) in zip((ys_ref, yp_ref, yb_ref), cuts):
        mix = mix + jnp.dot(y_ref[0], wo_ref[a:b, :], preferred_element_type=F32)
    x1 = x_ref[0] + g1_ref[0] * mix
    x1_ref[0] = x1
    h2 = _norm_mod(x1, ng_ref[...], sh_ref[0], sc_ref[0]).astype(BF16)
    h2_ref[0] = h2
    qb = jnp.dot(h2, wq_ref[...], preferred_element_type=F32).astype(BF16)
    tp = sct_ref.shape[-1]
    for lst in range(keys_ref.shape[0]):
        sc_t = _nt_dot(keys_ref[lst], qb[:, lst * PEER_HALF:(lst + 1) * PEER_HALF])
        for k in range(sct_ref.shape[0]):
            sct_ref[k, lst] = sc_t[:, k * tp:(k + 1) * tp]


def _out_call(ys, yp, yb, x, g1, sh, sc, ng, w_out, wq, keys):
    bsz, s, d = x.shape
    tm = min(TM_IN, s)
    nlist = keys.shape[0]
    cuts, a = [], 0
    for y in (ys, yp, yb):
        cuts.append((a, a + y.shape[-1]))
        a += y.shape[-1]
    tok = lambda w: pl.BlockSpec((1, tm, w), lambda b, i: (b, i, 0))
    vec = pl.BlockSpec((1, 1, d), lambda b, i: (b, 0, 0))
    full2 = lambda arr: pl.BlockSpec(arr.shape, lambda b, i: (0,) * arr.ndim)
    nblk = s // tm
    tp = min(TB_PEER, s)
    per = tm // tp
    return pl.pallas_call(
        functools.partial(_out_kernel, cuts=tuple(cuts)),
        grid=(bsz, nblk),
        in_specs=[tok(ys.shape[-1]), tok(yp.shape[-1]), tok(yb.shape[-1]), tok(d), vec, vec, vec,
                  pl.BlockSpec((1, d), lambda b, i: (0, 0)), full2(w_out), full2(wq), full2(keys)],
        out_specs=[tok(d), tok(d),
                   pl.BlockSpec((per, nlist, N_KEYS, tp), lambda b, i: (b * nblk + i, 0, 0, 0))],
        out_shape=[jax.ShapeDtypeStruct((bsz, s, d), F32),
                   jax.ShapeDtypeStruct((bsz, s, d), BF16),
                   jax.ShapeDtypeStruct((bsz * s // tp, nlist, N_KEYS, tp), F32)],
        compiler_params=_cparams(("arbitrary", "arbitrary")),
        name="out_proj_peer_query",
    )(ys, yp, yb, x, g1, sh, sc, ng, w_out, wq, keys)


def _top_rows(s, k, payload=None):
    nrows = s.shape[0]
    rows = lax.broadcasted_iota(jnp.int32, s.shape, 0).astype(F32)
    vals, picks = [], []
    for _ in range(k):
        m = jnp.max(s, axis=0, keepdims=True)
        idx = jnp.min(jnp.where(s == m, rows, float(nrows)), axis=0, keepdims=True)
        hit = rows == idx
        vals.append(m)
        if payload is None:
            picks.append(idx)
        else:
            picks.append(jnp.sum(jnp.where(hit, payload, 0.0), axis=0, keepdims=True))
        s = jnp.where(hit, -jnp.inf, s)
    return jnp.concatenate(vals, axis=0), jnp.concatenate(picks, axis=0)


def _pair_candidates(v1, x1, v2, x2):
    K = PEER_TOPK
    cands, ids = [], []
    a = 0
    while K // (a + 1) > 1:
        n = K // (a + 1)
        npad = -(-n // SUBLANES) * SUBLANES
        c = v1[a:a + 1, :] + v2[0:npad, :]
        if n < npad:
            c = jnp.where(lax.broadcasted_iota(jnp.int32, c.shape, 0) < n, c, -jnp.inf)
        cands.append(c)
        ids.append(x1[a:a + 1, :] * float(N_KEYS) + x2[0:npad, :])
        a += 1
    cands.append(v1[a:K, :] + v2[0:1, :])
    ids.append(x1[a:K, :] * float(N_KEYS) + x2[0:1, :])
    return jnp.concatenate(cands, axis=0), jnp.concatenate(ids, axis=0)


def _peer_kernel(sct_ref, h_ref, u_ref, v_ref, x_ref, g2_ref, fg_ref, o_ref,
                 pk_ref, pt_ref, scr_ref, g_ref, coef_ref, acc_ref, *, final):
    s = pl.program_id(0)
    j = pl.program_id(1)
    K = PEER_TOPK
    slot = s % 2
    eb = u_ref.shape[0]

    @pl.when(jnp.logical_and(s == 0, j == 0))
    def _():
        pk_ref[...] = jnp.zeros_like(pk_ref)
        scr_ref[...] = jnp.zeros_like(scr_ref)
        g_ref[...] = jnp.zeros_like(g_ref)
        coef_ref[...] = jnp.zeros_like(coef_ref)
        acc_ref[...] = jnp.zeros_like(acc_ref)

    @pl.when(j == 0)
    def _():
        for k in range(pk_ref.shape[0]):
            pt_ref[k] = pk_ref[k].T

    v1, x1 = _top_rows(sct_ref[0], K)
    v2, x2 = _top_rows(sct_ref[1], K)
    cand, ids = _pair_candidates(v1, x1, v2, x2)
    best, expert = _top_rows(cand, K, payload=ids)
    e = jnp.exp(best - best[0:1, :])
    gate = e / jnp.sum(e, axis=0, keepdims=True)
    i1 = jnp.floor(expert * (1.0 / N_KEYS))
    r0 = pl.multiple_of(j * K, K)
    pk_ref[0, pl.ds(r0, K), :] = i1
    pk_ref[1, pl.ds(r0, K), :] = expert - i1 * float(N_KEYS)
    pk_ref[2, pl.ds(r0, K), :] = 0.5 * gate

    npair = pt_ref.shape[-1]
    sub = lax.broadcasted_iota(jnp.int32, (N_KEYS, npair), 0).astype(F32)
    t0 = pl.multiple_of(j * GATE_GROUP, GATE_GROUP)
    for tl in range(GATE_GROUP):
        r = pl.ds(t0 + tl, 1)
        lhs = jnp.where(sub == pt_ref[0, r, :], pt_ref[2, r, :], 0.0).astype(BF16)
        rhs = jnp.where(sub == pt_ref[1, r, :], 1.0, 0.0).astype(BF16)
        scr_ref[pl.ds(tl, N_KEYS, stride=GATE_PITCH), :] = _nt_dot(lhs, rhs)
    for a in range(N_KEYS):
        g_ref[1 - slot, a, pl.ds(t0, GATE_GROUP), :] = (
            scr_ref[a * GATE_PITCH:a * GATE_PITCH + GATE_GROUP, :].astype(g_ref.dtype))

    h = h_ref[...]
    d = h.shape[-1]
    acc_ref[...] += jnp.dot(coef_ref[...], v_ref[:, :d], preferred_element_type=F32)

    a0 = j * (eb // N_KEYS)
    pieces = []
    for q in range(eb // DENSE_SUB):
        act = _nt_dot(h, u_ref[q * DENSE_SUB:(q + 1) * DENSE_SUB, :d])
        w = (act * (1.0 + lax.erf(act * (2.0 ** -0.5)))).astype(BF16)
        for a in range(DENSE_SUB // N_KEYS):
            ga = g_ref[slot, a0 + q * (DENSE_SUB // N_KEYS) + a]
            pieces.append(w[:, a * N_KEYS:(a + 1) * N_KEYS] * ga)
    coef_ref[...] = jnp.concatenate(pieces, axis=1)

    @pl.when(j == 0)
    def _():
        x2 = x_ref[...] + g2_ref[0] * acc_ref[...]
        if final:
            ms = jnp.mean(x2 * x2, axis=-1, keepdims=True)
            x2 = x2 * lax.rsqrt(ms + EPS) * fg_ref[...]
        o_ref[...] = x2
        acc_ref[...] = jnp.zeros_like(acc_ref)


def _peer_call(sct, h2, u, v, x1, g2, fg, final):
    bsz, s, d = x1.shape
    t = bsz * s
    ne = u.shape[0]
    _, nlist, nk, _ = sct.shape
    tb = min(TB_PEER, s)
    eb = ne // PEER_HEADS
    npair = PEER_HEADS * PEER_TOPK
    assert tb == PEER_HEADS * GATE_GROUP and nlist == 2 * PEER_HEADS and eb % MXU_WIDTH == 0
    nt = t // tb
    per_b = s // tb
    nchunk = PEER_HEADS
    dense_tile = lambda i: jnp.clip(i - 2, 0, nt - 1)
    row = lambda i, j: (dense_tile(i), 0)
    out_tile = lambda i, j: jnp.clip(i - 2 - (j == 0).astype(jnp.int32), 0, nt - 1)
    out_row = lambda i, j: (out_tile(i, j), 0)
    dp = d + LANES if (d // LANES) % 2 == 0 else d
    u = jnp.pad(u, ((0, 0), (0, dp - d)))
    v = jnp.pad(v, ((0, 0), (0, dp - d)))
    out = pl.pallas_call(
        functools.partial(_peer_kernel, final=final),
        grid=(nt + 3, nchunk),
        in_specs=[pl.BlockSpec((None, 2, nk, tb), lambda i, j: (jnp.minimum(i, nt - 1), j, 0, 0)),
                  pl.BlockSpec((tb, d), row),
                  pl.BlockSpec((eb, dp), lambda i, j: (j, 0)),
                  pl.BlockSpec((eb, dp), lambda i, j: ((j + nchunk - 1) % nchunk, 0)),
                  pl.BlockSpec((tb, d), out_row),
                  pl.BlockSpec((1, 1, d), lambda i, j: (out_tile(i, j) // per_b, 0, 0)),
                  pl.BlockSpec((1, d), lambda i, j: (0, 0))],
        out_specs=pl.BlockSpec((tb, d), out_row),
        out_shape=jax.ShapeDtypeStruct((t, d), F32),
        scratch_shapes=[pltpu.VMEM((3, npair, tb), F32),
                        pltpu.VMEM((3, tb, npair), F32),
                        pltpu.VMEM((N_KEYS * GATE_PITCH, N_KEYS), F32),
                        pltpu.VMEM((2, N_KEYS, tb, N_KEYS), BF16),
                        pltpu.VMEM((tb, eb), BF16),
                        pltpu.VMEM((tb, d), F32)],
        compiler_params=_cparams(("arbitrary", "arbitrary")),
        name="peer_fused",
    )(sct, h2.reshape(t, d), u, v, x1.reshape(t, d), g2, fg)
    return out.reshape(bsz, s, d)


def kernel(x, c, ada_w, ada_b, mix_norm_g, ffn_norm_g, w_in, conv_w, conv_b, dt_bias, a_log, d_skip,
           ssd_norm_g, pool_w, pool_scale, w_out, peer_wq, peer_keys, peer_u, peer_v, final_norm_g):
    bsz, s, d = x.shape
    depth = ada_w.shape[0]
    ssd_w = ssd_norm_g.shape[-1]
    heads = dt_bias.shape[-1]
    xbc_w = conv_w.shape[-1]
    pool_wd = pool_scale.shape[-1]
    sb_w = (w_in.shape[-1] - ssd_w - xbc_w - heads - pool_wd) // 3
    o_z, o_xbc, o_dt, o_xp = 0, ssd_w, ssd_w + xbc_w, ssd_w + xbc_w + heads
    o_q = o_xp + pool_wd
    widths = (ssd_w, xbc_w, pool_wd, sb_w, sb_w, sb_w, LANES)

    mods = _mod_call(c, ada_w, ada_b)
    for l in range(depth):
        mod = [mods[l, :, k * d:(k + 1) * d].reshape(bsz, 1, d) for k in range(6)]
        sh1, sc1, g1, sh2, sc2, g2 = mod
        wl = w_in[l]
        w_cat = jnp.concatenate(
            [wl[:, o_z:o_xbc], wl[:, o_xbc:o_dt], wl[:, o_xp:o_q], wl[:, o_q:],
             jnp.pad(wl[:, o_dt:o_xp], ((0, 0), (0, LANES - heads)))], axis=1).astype(BF16)
        z, xbc, xp, q, k, v, dt = _in_call(x, sh1, sc1, mix_norm_g[l].reshape(1, d), w_cat, widths)
        y_ssd = _ssd_call(xbc, dt, z, conv_w[l], conv_b[l], dt_bias[l], a_log[l], d_skip[l],
                          ssd_norm_g[l])
        y_pool = _pool_call(xp, pool_w[l], pool_scale[l])
        y_sb = _sb_call(q, k, v)
        keys = peer_keys[l].reshape(-1, N_KEYS, PEER_HALF).astype(BF16)
        x1, h2, sct = _out_call(y_ssd, y_pool, y_sb, x, g1, sh2, sc2, ffn_norm_g[l].reshape(1, d),
                                w_out[l].astype(BF16), peer_wq[l].astype(BF16), keys)
        x = _peer_call(sct, h2, peer_u[l].astype(BF16), peer_v[l].astype(BF16), x1, g2,
                       final_norm_g.reshape(1, d), final=(l == depth - 1))
    return x
```

```python
import functools

import jax
import jax.numpy as jnp
from jax import lax
from jax.experimental import pallas as pl
from jax.experimental.pallas import tpu as pltpu

F32 = jnp.float32
BF16 = jnp.bfloat16
EPS = 1e-6
HIGHEST = lax.Precision.HIGHEST

SSD_HEAD_DIM = 64
SSD_GROUPS = 2
SSD_STATE = 64
CONV_WIDTH = 4
POOL_WINDOWS = (2, 4, 8, 16)
SB_HEAD_DIM = 64
PEER_HEADS = 8
PEER_TOPK = 16
N_KEYS = 128
PEER_HALF = 128

LANES = 128
SUBLANES = 8
MXU_WIDTH = 256
VMEM_LIMIT = 56 * 1024 * 1024

TM_IN = 512
SSD_CHUNK = 256
HALO = 16
TM_POOL = 512
SB_BLOCK = 128
SB_SKIP = 40.0
SB_QBLOCKS = 4
GATE_GROUP = 32
GATE_PITCH = 36
TB_PEER = 256
DENSE_SUB = 256


def _cparams(sem):
    return pltpu.CompilerParams(dimension_semantics=sem, vmem_limit_bytes=VMEM_LIMIT)


def _nt_dot(a, b):
    return lax.dot_general(a, b, (((1,), (1,)), ((), ())), preferred_element_type=F32)


def _softplus(x):
    return jnp.maximum(x, 0.0) + jnp.log(1.0 + jnp.exp(-jnp.abs(x)))


def _silu(x):
    return x * jax.nn.sigmoid(x)


def _mod_kernel(c_ref, w_ref, b_ref, o_ref):
    cond = _silu(c_ref[...])
    o_ref[0] = jnp.dot(cond, w_ref[0], preferred_element_type=F32, precision=HIGHEST) + b_ref[0]


def _mod_call(c, ada_w, ada_b):
    nl, d, n6 = ada_w.shape
    bsz = c.shape[0]
    tn = 1536
    return pl.pallas_call(
        _mod_kernel,
        grid=(nl, n6 // tn),
        in_specs=[pl.BlockSpec((bsz, d), lambda l, j: (0, 0)),
                  pl.BlockSpec((1, d, tn), lambda l, j: (l, 0, j)),
                  pl.BlockSpec((1, 1, tn), lambda l, j: (l, 0, j))],
        out_specs=pl.BlockSpec((1, bsz, tn), lambda l, j: (l, 0, j)),
        out_shape=jax.ShapeDtypeStruct((nl, bsz, n6), F32),
        compiler_params=_cparams(("arbitrary", "arbitrary")),
        name="adaln_mod",
    )(c, ada_w, ada_b.reshape(nl, 1, n6))


def _norm_mod(x, g, sh, sc):
    ms = jnp.mean(x * x, axis=-1, keepdims=True)
    y = x * lax.rsqrt(ms + EPS) * g
    return y * (1.0 + sc) + sh


def _in_kernel(x_ref, sh_ref, sc_ref, g_ref, w_ref, z_ref, xbc_ref, xp_ref, q_ref, k_ref, v_ref,
               dt_ref, *, cuts):
    h = _norm_mod(x_ref[0], g_ref[...], sh_ref[0], sc_ref[0]).astype(BF16)
    outs = (z_ref, xbc_ref, xp_ref, q_ref, k_ref, v_ref, dt_ref)
    for o_ref, (a, b) in zip(outs, cuts):
        o_ref[0] = jnp.dot(h, w_ref[:, a:b], preferred_element_type=F32).astype(o_ref.dtype)


def _in_call(x, sh, sc, g, w_cat, widths):
    bsz, s, d = x.shape
    tm = min(TM_IN, s)
    cuts, a = [], 0
    for w in widths:
        cuts.append((a, a + w))
        a += w
    dtypes = (BF16, BF16, BF16, BF16, BF16, BF16, F32)
    tok = lambda w: pl.BlockSpec((1, tm, w), lambda b, i: (b, i, 0))
    vec = pl.BlockSpec((1, 1, d), lambda b, i: (b, 0, 0))
    return pl.pallas_call(
        functools.partial(_in_kernel, cuts=tuple(cuts)),
        grid=(bsz, s // tm),
        in_specs=[tok(d), vec, vec,
                  pl.BlockSpec((1, d), lambda b, i: (0, 0)),
                  pl.BlockSpec(w_cat.shape, lambda b, i: (0, 0))],
        out_specs=[tok(w) for w in widths],
        out_shape=[jax.ShapeDtypeStruct((bsz, s, w), dt) for w, dt in zip(widths, dtypes)],
        compiler_params=_cparams(("arbitrary", "arbitrary")),
        name="in_proj",
    )(x, sh, sc, g, w_cat)


def _ssd_kernel(xbc_ref, halo_ref, dt_ref, z_ref, cw_ref, cb_ref, dtb_ref, alog_ref, expand_ref,
                dsk_ref, ng_ref, y_ref, state_ref, *, chunk, width, gn):
    i = pl.program_id(1)
    L = chunk
    heads = width // SSD_HEAD_DIM
    hpg = heads // SSD_GROUPS
    half = width // SSD_GROUPS

    @pl.when(i == 0)
    def _():
        state_ref[...] = jnp.zeros_like(state_ref)

    hal = jnp.where(i > 0, halo_ref[0].astype(F32), 0.0)
    full = jnp.concatenate([hal, xbc_ref[0].astype(F32)], axis=0)
    cw = cw_ref[...]
    acc = full * cw[CONV_WIDTH - 1:CONV_WIDTH]
    for j in range(1, CONV_WIDTH):
        acc = acc + pltpu.roll(full, j, 0) * cw[CONV_WIDTH - 1 - j:CONV_WIDTH - j]
    xbc = _silu(acc[HALO:] + cb_ref[...])
    xs = xbc[:, :width]
    bm = xbc[:, width:width + gn]
    cm = xbc[:, width + gn:width + 2 * gn]

    dt = _softplus(dt_ref[0] + dtb_ref[...])
    a = dt * (-jnp.exp(alog_ref[...]))
    row_i = lax.broadcasted_iota(jnp.int32, (L, L), 0)
    col_i = lax.broadcasted_iota(jnp.int32, (L, L), 1)
    tril = row_i >= col_i
    a_cs = jnp.dot(tril.astype(F32), a, preferred_element_type=F32, precision=HIGHEST)
    a_cs_t = a_cs.T
    expand = expand_ref[...]
    dt_x = jnp.dot(dt, expand, preferred_element_type=F32, precision=HIGHEST)
    acs_x = jnp.dot(a_cs, expand, preferred_element_type=F32, precision=HIGHEST)
    alast_x = acs_x[L - 1:L, :]
    xdt = xs * dt_x

    bm_b = bm.astype(BF16)
    cm_b = cm.astype(BF16)
    lane_g = lax.broadcasted_iota(jnp.int32, (L, gn), 1) // SSD_STATE
    lane_h = lax.broadcasted_iota(jnp.int32, (L, half), 1) // SSD_HEAD_DIM

    state = state_ref[...]
    y_off = jnp.dot(cm_b, state.astype(BF16), preferred_element_type=F32) * jnp.exp(acs_x)

    y_halves = []
    for g in range(SSD_GROUPS):
        cb = _nt_dot(jnp.where(lane_g == g, cm_b, jnp.zeros_like(cm_b)), bm_b)
        xdt_g = xdt[:, g * half:(g + 1) * half]
        yh = jnp.zeros((L, half), F32)
        for hh in range(hpg):
            h = g * hpg + hh
            seg = a_cs[:, h:h + 1] - a_cs_t[h:h + 1, :]
            dec = jnp.exp(jnp.where(tril, seg, -1e30))
            m = (cb * dec).astype(BF16)
            rhs = jnp.where(lane_h == hh, xdt_g, 0.0).astype(BF16)
            yh = yh + jnp.dot(m, rhs, preferred_element_type=F32)
        y_halves.append(yh)

    ds_x = jnp.exp(alast_x - acs_x)
    upd = jnp.dot(bm.T.astype(BF16), (xdt * ds_x).astype(BF16), preferred_element_type=F32)
    srow = lax.broadcasted_iota(jnp.int32, (gn, width), 0) // SSD_STATE
    scol = lax.broadcasted_iota(jnp.int32, (gn, width), 1) // half
    state_ref[...] = jnp.exp(alast_x) * state + jnp.where(srow == scol, upd, 0.0)

    zg = _silu(z_ref[0].astype(F32))
    for g in range(SSD_GROUPS):
        sl = slice(g * half, (g + 1) * half)
        yg = (y_halves[g] + y_off[:, sl] + dsk_ref[:, sl] * xs[:, sl]) * zg[:, sl]
        ms = jnp.mean(yg * yg, axis=-1, keepdims=True)
        y_ref[0, :, sl] = (yg * lax.rsqrt(ms + EPS) * ng_ref[:, sl]).astype(y_ref.dtype)


def _ssd_call(xbc, dt, z, conv_w, conv_b, dt_bias, a_log, d_skip, ssd_norm_g):
    bsz, s, c = xbc.shape
    width = z.shape[-1]
    gn = SSD_GROUPS * SSD_STATE
    heads = width // SSD_HEAD_DIM
    L = min(SSD_CHUNK, s)
    pad = lambda v: jnp.zeros((1, LANES), F32).at[0, :heads].set(v)
    expand = (jnp.arange(LANES)[:, None] == (jnp.arange(width)[None, :] // SSD_HEAD_DIM)).astype(F32)
    dsk = jnp.repeat(d_skip, SSD_HEAD_DIM).reshape(1, width)
    hb = L // HALO
    const = lambda shp: pl.BlockSpec(shp, lambda b, i: (0, 0))
    return pl.pallas_call(
        functools.partial(_ssd_kernel, chunk=L, width=width, gn=gn),
        grid=(bsz, s // L),
        in_specs=[pl.BlockSpec((1, L, c), lambda b, i: (b, i, 0)),
                  pl.BlockSpec((1, HALO, c), lambda b, i: (b, jnp.maximum(i * hb - 1, 0), 0)),
                  pl.BlockSpec((1, L, LANES), lambda b, i: (b, i, 0)),
                  pl.BlockSpec((1, L, width), lambda b, i: (b, i, 0)),
                  const((CONV_WIDTH, c)), const((1, c)), const((1, LANES)), const((1, LANES)),
                  const((LANES, width)), const((1, width)), const((1, width))],
        out_specs=pl.BlockSpec((1, L, width), lambda b, i: (b, i, 0)),
        out_shape=jax.ShapeDtypeStruct((bsz, s, width), BF16),
        scratch_shapes=[pltpu.VMEM((gn, width), F32)],
        compiler_params=_cparams(("arbitrary", "arbitrary")),
        name="ssd_mixer",
    )(xbc, xbc, dt, z, conv_w, conv_b.reshape(1, c), pad(dt_bias), pad(a_log), expand, dsk,
      ssd_norm_g.reshape(1, width))


def _pool_kernel(xp_ref, halo_ref, w_ref, sc_ref, y_ref, *, tm, gdim):
    i = pl.program_id(1)
    hal = jnp.where(i > 0, halo_ref[0].astype(F32), 0.0)
    x = xp_ref[0].astype(F32)
    full = jnp.concatenate([hal, x], axis=0)
    lane_g = lax.broadcasted_iota(jnp.int32, x.shape, 1) // gdim
    tpos = i * tm + lax.broadcasted_iota(jnp.int32, x.shape, 0)
    win_sum = jnp.zeros_like(x)
    cnt = jnp.ones_like(x)
    s = full
    span = 1
    for gi, w in enumerate(POOL_WINDOWS):
        while span < w:
            s = s + pltpu.roll(s, span, 0)
            span *= 2
        win_sum = jnp.where(lane_g == gi, s[HALO:], win_sum)
        cnt = jnp.where(lane_g == gi, jnp.minimum(tpos + 1, w).astype(F32), cnt)
    pooled = (win_sum / cnt - x).astype(BF16)
    y = jnp.dot(pooled, w_ref[...], preferred_element_type=F32) * sc_ref[...]
    y_ref[0] = y.astype(y_ref.dtype)


def _pool_call(xp, pool_w, pool_scale):
    bsz, s, w = xp.shape
    ng, gdim, _ = pool_w.shape
    assert POOL_WINDOWS[-1] <= HALO and all(b == 2 * a for a, b in zip(POOL_WINDOWS, POOL_WINDOWS[1:]))
    tm = min(TM_POOL, s)
    wbd = jnp.zeros((w, w), F32)
    for g in range(ng):
        wbd = wbd.at[g * gdim:(g + 1) * gdim, g * gdim:(g + 1) * gdim].set(pool_w[g])
    hb = tm // HALO
    return pl.pallas_call(
        functools.partial(_pool_kernel, tm=tm, gdim=gdim),
        grid=(bsz, s // tm),
        in_specs=[pl.BlockSpec((1, tm, w), lambda b, i: (b, i, 0)),
                  pl.BlockSpec((1, HALO, w), lambda b, i: (b, jnp.maximum(i * hb - 1, 0), 0)),
                  pl.BlockSpec((w, w), lambda b, i: (0, 0)),
                  pl.BlockSpec((1, w), lambda b, i: (0, 0))],
        out_specs=pl.BlockSpec((1, tm, w), lambda b, i: (b, i, 0)),
        out_shape=jax.ShapeDtypeStruct((bsz, s, w), BF16),
        compiler_params=_cparams(("arbitrary", "arbitrary")),
        name="pool_mixer",
    )(xp, xp, wbd.astype(BF16), pool_scale.reshape(1, w))


def _sb_kernel(q_ref, k_ref, v_ref, y_ref, qm_ref, acc_ref, carry_ref, *, blk, heads, nq):
    i = pl.program_id(1)
    T = blk
    w = q_ref.shape[-1]
    lane_h = lax.broadcasted_iota(jnp.int32, (T, w), 1) // SB_HEAD_DIM
    for u in range(nq):
        q = q_ref[0, u * T:(u + 1) * T, :] * (SB_HEAD_DIM ** -0.5)
        for h in range(heads):
            qm_ref[u, h * T:(h + 1) * T, :] = jnp.where(lane_h == h, q, jnp.zeros_like(q))
    row_i = lax.broadcasted_iota(jnp.int32, (T, T), 0)
    col_i = lax.broadcasted_iota(jnp.int32, (T, T), 1)
    suffix = (row_i > col_i).astype(BF16)
    qrow = lax.broadcasted_iota(jnp.int32, (heads * T, T), 0) % T
    causal = lax.broadcasted_iota(jnp.int32, (heads * T, T), 1) < qrow

    def scores(u, j, mask):
        ks = k_ref[0, pl.ds(pl.multiple_of(j * T, T), T), :]
        z = _nt_dot(qm_ref[u], ks)
        lk = -_softplus(z)
        if mask is not None:
            lk = jnp.where(mask, lk, 0.0)
        hi = lk.astype(BF16)
        lo = (lk - hi.astype(F32)).astype(BF16)
        later = (jnp.dot(hi, suffix, preferred_element_type=F32)
                 + jnp.dot(lo, suffix, preferred_element_type=F32))
        return z, lk, later

    def weighted(j, z, lk, later, c, mask):
        vs = v_ref[0, pl.ds(pl.multiple_of(j * T, T), T), :]
        wgt = jnp.exp(z + lk + later + c)
        if mask is not None:
            wgt = jnp.where(mask, wgt, 0.0)
        pv = jnp.dot(wgt.astype(BF16), vs, preferred_element_type=F32)
        out = pv[0:T]
        for h in range(1, heads):
            out = jnp.where(lane_h == h, pv[h * T:(h + 1) * T], out)
        return out, c + jnp.sum(lk, axis=1, keepdims=True)

    def alive():
        return (jnp.max(carry_ref[...]) > -SB_SKIP).astype(jnp.int32)

    first = []
    for u in range(nq):
        qb = i * nq + u
        prev = jnp.maximum(qb - 1, 0)
        has_prev = None if u > 0 else jnp.broadcast_to(qb > 0, causal.shape)
        first.append((qb, prev, has_prev, scores(u, qb, causal), scores(u, prev, has_prev)))
    for u, (qb, prev, has_prev, sd, sp) in enumerate(first):
        out_d, c = weighted(qb, *sd, jnp.zeros((heads * T, 1), F32), causal)
        out_p, c = weighted(prev, *sp, c, has_prev)
        acc_ref[u] = out_d + out_p
        carry_ref[u] = c

    def cond(st):
        return jnp.logical_and(i * nq + nq - 3 - st[0] >= 0, st[1] > 0)

    def body(st):
        n = st[0]
        for u in range(nq):
            j = i * nq + u - 2 - n
            valid = jnp.broadcast_to(j >= 0, causal.shape)
            jj = jnp.maximum(j, 0)
            out, c = weighted(jj, *scores(u, jj, valid), carry_ref[u], valid)
            acc_ref[u] += out
            carry_ref[u] = c
        return n + 1, alive()

    lax.while_loop(cond, body, (0, alive()))
    for u in range(nq):
        y_ref[0, u * T:(u + 1) * T, :] = acc_ref[u].astype(y_ref.dtype)


def _sb_call(q, k, v):
    bsz, s, w = q.shape
    heads = w // SB_HEAD_DIM
    T = min(SB_BLOCK, s)
    nq = SB_QBLOCKS if (s // T) % SB_QBLOCKS == 0 else 1
    return pl.pallas_call(
        functools.partial(_sb_kernel, blk=T, heads=heads, nq=nq),
        grid=(bsz, s // (nq * T)),
        in_specs=[pl.BlockSpec((1, nq * T, w), lambda b, i: (b, i, 0)),
                  pl.BlockSpec((1, s, w), lambda b, i: (b, 0, 0)),
                  pl.BlockSpec((1, s, w), lambda b, i: (b, 0, 0))],
        out_specs=pl.BlockSpec((1, nq * T, w), lambda b, i: (b, i, 0)),
        out_shape=jax.ShapeDtypeStruct((bsz, s, w), BF16),
        scratch_shapes=[pltpu.VMEM((nq, heads * T, w), BF16),
                        pltpu.VMEM((nq, T, w), F32),
                        pltpu.VMEM((nq, heads * T, 1), F32)],
        compiler_params=_cparams(("arbitrary", "arbitrary")),
        name="stick_breaking",
    )(q, k, v)


def _out_kernel(ys_ref, yp_ref, yb_ref, x_ref, g1_ref, sh_ref, sc_ref, ng_ref, wo_ref, wq_ref,
                keys_ref, x1_ref, h2_ref, sct_ref, *, cuts):
    mix = jnp.zeros(x_ref.shape[1:], F32)
    for y_ref, (a, b) in zip((ys_ref, yp_ref, yb_ref), cuts):
        mix = mix + jnp.dot(y_ref[0], wo_ref[a:b, :], preferred_element_type=F32)
    x1 = x_ref[0] + g1_ref[0] * mix
    x1_ref[0] = x1
    h2 = _norm_mod(x1, ng_ref[...], sh_ref[0], sc_ref[0]).astype(BF16)
    h2_ref[0] = h2
    qb = jnp.dot(h2, wq_ref[...], preferred_element_type=F32).astype(BF16)
    tp = sct_ref.shape[-1]
    for lst in range(keys_ref.shape[0]):
        sc_t = _nt_dot(keys_ref[lst], qb[:, lst * PEER_HALF:(lst + 1) * PEER_HALF])
        for k in range(sct_ref.shape[0]):
            sct_ref[k, lst] = sc_t[:, k * tp:(k + 1) * tp]


def _out_call(ys, yp, yb, x, g1, sh, sc, ng, w_out, wq, keys):
    bsz, s, d = x.shape
    tm = min(TM_IN, s)
    nlist = keys.shape[0]
    cuts, a = [], 0
    for y in (ys, yp, yb):
        cuts.append((a, a + y.shape[-1]))
        a += y.shape[-1]
    tok = lambda w: pl.BlockSpec((1, tm, w), lambda b, i: (b, i, 0))
    vec = pl.BlockSpec((1, 1, d), lambda b, i: (b, 0, 0))
    full2 = lambda arr: pl.BlockSpec(arr.shape, lambda b, i: (0,) * arr.ndim)
    nblk = s // tm
    tp = min(TB_PEER, s)
    per = tm // tp
    return pl.pallas_call(
        functools.partial(_out_kernel, cuts=tuple(cuts)),
        grid=(bsz, nblk),
        in_specs=[tok(ys.shape[-1]), tok(yp.shape[-1]), tok(yb.shape[-1]), tok(d), vec, vec, vec,
                  pl.BlockSpec((1, d), lambda b, i: (0, 0)), full2(w_out), full2(wq), full2(keys)],
        out_specs=[tok(d), tok(d),
                   pl.BlockSpec((per, nlist, N_KEYS, tp), lambda b, i: (b * nblk + i, 0, 0, 0))],
        out_shape=[jax.ShapeDtypeStruct((bsz, s, d), F32),
                   jax.ShapeDtypeStruct((bsz, s, d), BF16),
                   jax.ShapeDtypeStruct((bsz * s // tp, nlist, N_KEYS, tp), F32)],
        compiler_params=_cparams(("arbitrary", "arbitrary")),
        name="out_proj_peer_query",
    )(ys, yp, yb, x, g1, sh, sc, ng, w_out, wq, keys)


def _top_rows(s, k, payload=None):
    nrows = s.shape[0]
    rows = lax.broadcasted_iota(jnp.int32, s.shape, 0).astype(F32)
    vals, picks = [], []
    for _ in range(k):
        m = jnp.max(s, axis=0, keepdims=True)
        idx = jnp.min(jnp.where(s == m, rows, float(nrows)), axis=0, keepdims=True)
        hit = rows == idx
        vals.append(m)
        if payload is None:
            picks.append(idx)
        else:
            picks.append(jnp.sum(jnp.where(hit, payload, 0.0), axis=0, keepdims=True))
        s = jnp.where(hit, -jnp.inf, s)
    return jnp.concatenate(vals, axis=0), jnp.concatenate(picks, axis=0)


def _pair_candidates(v1, x1, v2, x2):
    K = PEER_TOPK
    cands, ids = [], []
    a = 0
    while K // (a + 1) > 1:
        n = K // (a + 1)
        npad = -(-n // SUBLANES) * SUBLANES
        c = v1[a:a + 1, :] + v2[0:npad, :]
        if n < npad:
            c = jnp.where(lax.broadcasted_iota(jnp.int32, c.shape, 0) < n, c, -jnp.inf)
        cands.append(c)
        ids.append(x1[a:a + 1, :] * float(N_KEYS) + x2[0:npad, :])
        a += 1
    cands.append(v1[a:K, :] + v2[0:1, :])
    ids.append(x1[a:K, :] * float(N_KEYS) + x2[0:1, :])
    return jnp.concatenate(cands, axis=0), jnp.concatenate(ids, axis=0)


def _peer_kernel(sct_ref, h_ref, u_ref, v_ref, x_ref, g2_ref, fg_ref, o_ref,
                 pk_ref, pt_ref, scr_ref, g_ref, coef_ref, acc_ref, *, final):
    s = pl.program_id(0)
    j = pl.program_id(1)
    K = PEER_TOPK
    slot = s % 2
    eb = u_ref.shape[0]

    @pl.when(jnp.logical_and(s == 0, j == 0))
    def _():
        pk_ref[...] = jnp.zeros_like(pk_ref)
        scr_ref[...] = jnp.zeros_like(scr_ref)
        g_ref[...] = jnp.zeros_like(g_ref)
        coef_ref[...] = jnp.zeros_like(coef_ref)
        acc_ref[...] = jnp.zeros_like(acc_ref)

    @pl.when(j == 0)
    def _():
        for k in range(pk_ref.shape[0]):
            pt_ref[k] = pk_ref[k].T

    v1, x1 = _top_rows(sct_ref[0], K)
    v2, x2 = _top_rows(sct_ref[1], K)
    cand, ids = _pair_candidates(v1, x1, v2, x2)
    best, expert = _top_rows(cand, K, payload=ids)
    e = jnp.exp(best - best[0:1, :])
    gate = e / jnp.sum(e, axis=0, keepdims=True)
    i1 = jnp.floor(expert * (1.0 / N_KEYS))
    r0 = pl.multiple_of(j * K, K)
    pk_ref[0, pl.ds(r0, K), :] = i1
    pk_ref[1, pl.ds(r0, K), :] = expert - i1 * float(N_KEYS)
    pk_ref[2, pl.ds(r0, K), :] = 0.5 * gate

    npair = pt_ref.shape[-1]
    sub = lax.broadcasted_iota(jnp.int32, (N_KEYS, npair), 0).astype(F32)
    t0 = pl.multiple_of(j * GATE_GROUP, GATE_GROUP)
    for tl in range(GATE_GROUP):
        r = pl.ds(t0 + tl, 1)
        lhs = jnp.where(sub == pt_ref[0, r, :], pt_ref[2, r, :], 0.0).astype(BF16)
        rhs = jnp.where(sub == pt_ref[1, r, :], 1.0, 0.0).astype(BF16)
        scr_ref[pl.ds(tl, N_KEYS, stride=GATE_PITCH), :] = _nt_dot(lhs, rhs)
    for a in range(N_KEYS):
        g_ref[1 - slot, a, pl.ds(t0, GATE_GROUP), :] = (
            scr_ref[a * GATE_PITCH:a * GATE_PITCH + GATE_GROUP, :].astype(g_ref.dtype))

    h = h_ref[...]
    d = h.shape[-1]
    acc_ref[...] += jnp.dot(coef_ref[...], v_ref[:, :d], preferred_element_type=F32)

    a0 = j * (eb // N_KEYS)
    pieces = []
    for q in range(eb // DENSE_SUB):
        act = _nt_dot(h, u_ref[q * DENSE_SUB:(q + 1) * DENSE_SUB, :d])
        w = (act * (1.0 + lax.erf(act * (2.0 ** -0.5)))).astype(BF16)
        for a in range(DENSE_SUB // N_KEYS):
            ga = g_ref[slot, a0 + q * (DENSE_SUB // N_KEYS) + a]
            pieces.append(w[:, a * N_KEYS:(a + 1) * N_KEYS] * ga)
    coef_ref[...] = jnp.concatenate(pieces, axis=1)

    @pl.when(j == 0)
    def _():
        x2 = x_ref[...] + g2_ref[0] * acc_ref[...]
        if final:
            ms = jnp.mean(x2 * x2, axis=-1, keepdims=True)
            x2 = x2 * lax.rsqrt(ms + EPS) * fg_ref[...]
        o_ref[...] = x2
        acc_ref[...] = jnp.zeros_like(acc_ref)


def _peer_call(sct, h2, u, v, x1, g2, fg, final):
    bsz, s, d = x1.shape
    t = bsz * s
    ne = u.shape[0]
    _, nlist, nk, _ = sct.shape
    tb = min(TB_PEER, s)
    eb = ne // PEER_HEADS
    npair = PEER_HEADS * PEER_TOPK
    assert tb == PEER_HEADS * GATE_GROUP and nlist == 2 * PEER_HEADS and eb % MXU_WIDTH == 0
    nt = t // tb
    per_b = s // tb
    nchunk = PEER_HEADS
    dense_tile = lambda i: jnp.clip(i - 2, 0, nt - 1)
    row = lambda i, j: (dense_tile(i), 0)
    out_tile = lambda i, j: jnp.clip(i - 2 - (j == 0).astype(jnp.int32), 0, nt - 1)
    out_row = lambda i, j: (out_tile(i, j), 0)
    dp = d + LANES if (d // LANES) % 2 == 0 else d
    u = jnp.pad(u, ((0, 0), (0, dp - d)))
    v = jnp.pad(v, ((0, 0), (0, dp - d)))
    out = pl.pallas_call(
        functools.partial(_peer_kernel, final=final),
        grid=(nt + 3, nchunk),
        in_specs=[pl.BlockSpec((None, 2, nk, tb), lambda i, j: (jnp.minimum(i, nt - 1), j, 0, 0)),
                  pl.BlockSpec((tb, d), row),
                  pl.BlockSpec((eb, dp), lambda i, j: (j, 0)),
                  pl.BlockSpec((eb, dp), lambda i, j: ((j + nchunk - 1) % nchunk, 0)),
                  pl.BlockSpec((tb, d), out_row),
                  pl.BlockSpec((1, 1, d), lambda i, j: (out_tile(i, j) // per_b, 0, 0)),
                  pl.BlockSpec((1, d), lambda i, j: (0, 0))],
        out_specs=pl.BlockSpec((tb, d), out_row),
        out_shape=jax.ShapeDtypeStruct((t, d), F32),
        scratch_shapes=[pltpu.VMEM((3, npair, tb), F32),
                        pltpu.VMEM((3, tb, npair), F32),
                        pltpu.VMEM((N_KEYS * GATE_PITCH, N_KEYS), F32),
                        pltpu.VMEM((2, N_KEYS, tb, N_KEYS), BF16),
                        pltpu.VMEM((tb, eb), BF16),
                        pltpu.VMEM((tb, d), F32)],
        compiler_params=_cparams(("arbitrary", "arbitrary")),
        name="peer_fused",
    )(sct, h2.reshape(t, d), u, v, x1.reshape(t, d), g2, fg)
    return out.reshape(bsz, s, d)


def kernel(x, c, ada_w, ada_b, mix_norm_g, ffn_norm_g, w_in, conv_w, conv_b, dt_bias, a_log, d_skip,
           ssd_norm_g, pool_w, pool_scale, w_out, peer_wq, peer_keys, peer_u, peer_v, final_norm_g):
    bsz, s, d = x.shape
    depth = ada_w.shape[0]
    ssd_w = ssd_norm_g.shape[-1]
    heads = dt_bias.shape[-1]
    xbc_w = conv_w.shape[-1]
    pool_wd = pool_scale.shape[-1]
    sb_w = (w_in.shape[-1] - ssd_w - xbc_w - heads - pool_wd) // 3
    o_z, o_xbc, o_dt, o_xp = 0, ssd_w, ssd_w + xbc_w, ssd_w + xbc_w + heads
    o_q = o_xp + pool_wd
    widths = (ssd_w, xbc_w, pool_wd, sb_w, sb_w, sb_w, LANES)

    mods = _mod_call(c, ada_w, ada_b)
    for l in range(depth):
        mod = [mods[l, :, k * d:(k + 1) * d].reshape(bsz, 1, d) for k in range(6)]
        sh1, sc1, g1, sh2, sc2, g2 = mod
        wl = w_in[l]
        w_cat = jnp.concatenate(
            [wl[:, o_z:o_xbc], wl[:, o_xbc:o_dt], wl[:, o_xp:o_q], wl[:, o_q:],
             jnp.pad(wl[:, o_dt:o_xp], ((0, 0), (0, LANES - heads)))], axis=1).astype(BF16)
        z, xbc, xp, q, k, v, dt = _in_call(x, sh1, sc1, mix_norm_g[l].reshape(1, d), w_cat, widths)
        y_ssd = _ssd_call(xbc, dt, z, conv_w[l], conv_b[l], dt_bias[l], a_log[l], d_skip[l],
                          ssd_norm_g[l])
        y_pool = _pool_call(xp, pool_w[l], pool_scale[l])
        y_sb = _sb_call(q, k, v)
        keys = peer_keys[l].reshape(-1, N_KEYS, PEER_HALF).astype(BF16)
        x1, h2, sct = _out_call(y_ssd, y_pool, y_sb, x, g1, sh2, sc2, ffn_norm_g[l].reshape(1, d),
                                w_out[l].astype(BF16), peer_wq[l].astype(BF16), keys)
        x = _peer_call(sct, h2, peer_u[l].astype(BF16), peer_v[l].astype(BF16), x1, g2,
                       final_norm_g.reshape(1, d), final=(l == depth - 1))
    return x
```

```python
import functools

import jax
import jax.numpy as jnp
from jax import lax
from jax.experimental import pallas as pl
from jax.experimental.pallas import tpu as pltpu

F32 = jnp.float32
BF16 = jnp.bfloat16
EPS = 1e-6
HIGHEST = lax.Precision.HIGHEST

SSD_HEAD_DIM = 64
SSD_GROUPS = 2
SSD_STATE = 64
CONV_WIDTH = 4
POOL_WINDOWS = (2, 4, 8, 16)
SB_HEAD_DIM = 64
PEER_HEADS = 8
PEER_TOPK = 16
N_KEYS = 128
PEER_HALF = 128

LANES = 128
SUBLANES = 8
MXU_WIDTH = 256
VMEM_LIMIT = 56 * 1024 * 1024

TM_IN = 512
SSD_CHUNK = 256
HALO = 16
TM_POOL = 512
SB_BLOCK = 128
SB_SKIP = 40.0
SB_QBLOCKS = 8
GATE_GROUP = 32
GATE_PITCH = 36
TB_PEER = 256
DENSE_SUB = 256


def _cparams(sem):
    return pltpu.CompilerParams(dimension_semantics=sem, vmem_limit_bytes=VMEM_LIMIT)


def _nt_dot(a, b):
    return lax.dot_general(a, b, (((1,), (1,)), ((), ())), preferred_element_type=F32)


def _softplus(x):
    return jnp.maximum(x, 0.0) + jnp.log(1.0 + jnp.exp(-jnp.abs(x)))


def _silu(x):
    return x * jax.nn.sigmoid(x)


def _mod_kernel(c_ref, w_ref, b_ref, o_ref):
    cond = _silu(c_ref[...])
    o_ref[0] = jnp.dot(cond, w_ref[0], preferred_element_type=F32, precision=HIGHEST) + b_ref[0]


def _mod_call(c, ada_w, ada_b):
    nl, d, n6 = ada_w.shape
    bsz = c.shape[0]
    tn = 1536
    return pl.pallas_call(
        _mod_kernel,
        grid=(nl, n6 // tn),
        in_specs=[pl.BlockSpec((bsz, d), lambda l, j: (0, 0)),
                  pl.BlockSpec((1, d, tn), lambda l, j: (l, 0, j)),
                  pl.BlockSpec((1, 1, tn), lambda l, j: (l, 0, j))],
        out_specs=pl.BlockSpec((1, bsz, tn), lambda l, j: (l, 0, j)),
        out_shape=jax.ShapeDtypeStruct((nl, bsz, n6), F32),
        compiler_params=_cparams(("arbitrary", "arbitrary")),
        name="adaln_mod",
    )(c, ada_w, ada_b.reshape(nl, 1, n6))


def _norm_mod(x, g, sh, sc):
    ms = jnp.mean(x * x, axis=-1, keepdims=True)
    y = x * lax.rsqrt(ms + EPS) * g
    return y * (1.0 + sc) + sh


def _in_kernel(x_ref, sh_ref, sc_ref, g_ref, w_ref, z_ref, xbc_ref, xp_ref, q_ref, k_ref, v_ref,
               dt_ref, *, cuts):
    h = _norm_mod(x_ref[0], g_ref[...], sh_ref[0], sc_ref[0]).astype(BF16)
    outs = (z_ref, xbc_ref, xp_ref, q_ref, k_ref, v_ref, dt_ref)
    for o_ref, (a, b) in zip(outs, cuts):
        o_ref[0] = jnp.dot(h, w_ref[:, a:b], preferred_element_type=F32).astype(o_ref.dtype)


def _in_call(x, sh, sc, g, w_cat, widths):
    bsz, s, d = x.shape
    tm = min(TM_IN, s)
    cuts, a = [], 0
    for w in widths:
        cuts.append((a, a + w))
        a += w
    dtypes = (BF16, BF16, BF16, BF16, BF16, BF16, F32)
    tok = lambda w: pl.BlockSpec((1, tm, w), lambda b, i: (b, i, 0))
    vec = pl.BlockSpec((1, 1, d), lambda b, i: (b, 0, 0))
    return pl.pallas_call(
        functools.partial(_in_kernel, cuts=tuple(cuts)),
        grid=(bsz, s // tm),
        in_specs=[tok(d), vec, vec,
                  pl.BlockSpec((1, d), lambda b, i: (0, 0)),
                  pl.BlockSpec(w_cat.shape, lambda b, i: (0, 0))],
        out_specs=[tok(w) for w in widths],
        out_shape=[jax.ShapeDtypeStruct((bsz, s, w), dt) for w, dt in zip(widths, dtypes)],
        compiler_params=_cparams(("arbitrary", "arbitrary")),
        name="in_proj",
    )(x, sh, sc, g, w_cat)


def _ssd_kernel(xbc_ref, halo_ref, dt_ref, z_ref, cw_ref, cb_ref, dtb_ref, alog_ref, expand_ref,
                dsk_ref, ng_ref, y_ref, state_ref, *, chunk, width, gn):
    i = pl.program_id(1)
    L = chunk
    heads = width // SSD_HEAD_DIM
    hpg = heads // SSD_GROUPS
    half = width // SSD_GROUPS

    @pl.when(i == 0)
    def _():
        state_ref[...] = jnp.zeros_like(state_ref)

    hal = jnp.where(i > 0, halo_ref[0].astype(F32), 0.0)
    full = jnp.concatenate([hal, xbc_ref[0].astype(F32)], axis=0)
    cw = cw_ref[...]
    acc = full * cw[CONV_WIDTH - 1:CONV_WIDTH]
    for j in range(1, CONV_WIDTH):
        acc = acc + pltpu.roll(full, j, 0) * cw[CONV_WIDTH - 1 - j:CONV_WIDTH - j]
    xbc = _silu(acc[HALO:] + cb_ref[...])
    xs = xbc[:, :width]
    bm = xbc[:, width:width + gn]
    cm = xbc[:, width + gn:width + 2 * gn]

    dt = _softplus(dt_ref[0] + dtb_ref[...])
    a = dt * (-jnp.exp(alog_ref[...]))
    row_i = lax.broadcasted_iota(jnp.int32, (L, L), 0)
    col_i = lax.broadcasted_iota(jnp.int32, (L, L), 1)
    tril = row_i >= col_i
    a_cs = jnp.dot(tril.astype(F32), a, preferred_element_type=F32, precision=HIGHEST)
    a_cs_t = a_cs.T
    expand = expand_ref[...]
    dt_x = jnp.dot(dt, expand, preferred_element_type=F32, precision=HIGHEST)
    acs_x = jnp.dot(a_cs, expand, preferred_element_type=F32, precision=HIGHEST)
    alast_x = acs_x[L - 1:L, :]
    xdt = xs * dt_x

    bm_b = bm.astype(BF16)
    cm_b = cm.astype(BF16)
    lane_g = lax.broadcasted_iota(jnp.int32, (L, gn), 1) // SSD_STATE
    lane_h = lax.broadcasted_iota(jnp.int32, (L, half), 1) // SSD_HEAD_DIM

    state = state_ref[...]
    y_off = jnp.dot(cm_b, state.astype(BF16), preferred_element_type=F32) * jnp.exp(acs_x)

    y_halves = []
    for g in range(SSD_GROUPS):
        cb = _nt_dot(jnp.where(lane_g == g, cm_b, jnp.zeros_like(cm_b)), bm_b)
        xdt_g = xdt[:, g * half:(g + 1) * half]
        yh = jnp.zeros((L, half), F32)
        for hh in range(hpg):
            h = g * hpg + hh
            seg = a_cs[:, h:h + 1] - a_cs_t[h:h + 1, :]
            dec = jnp.exp(jnp.where(tril, seg, -1e30))
            m = (cb * dec).astype(BF16)
            rhs = jnp.where(lane_h == hh, xdt_g, 0.0).astype(BF16)
            yh = yh + jnp.dot(m, rhs, preferred_element_type=F32)
        y_halves.append(yh)

    ds_x = jnp.exp(alast_x - acs_x)
    upd = jnp.dot(bm.T.astype(BF16), (xdt * ds_x).astype(BF16), preferred_element_type=F32)
    srow = lax.broadcasted_iota(jnp.int32, (gn, width), 0) // SSD_STATE
    scol = lax.broadcasted_iota(jnp.int32, (gn, width), 1) // half
    state_ref[...] = jnp.exp(alast_x) * state + jnp.where(srow == scol, upd, 0.0)

    zg = _silu(z_ref[0].astype(F32))
    for g in range(SSD_GROUPS):
        sl = slice(g * half, (g + 1) * half)
        yg = (y_halves[g] + y_off[:, sl] + dsk_ref[:, sl] * xs[:, sl]) * zg[:, sl]
        ms = jnp.mean(yg * yg, axis=-1, keepdims=True)
        y_ref[0, :, sl] = (yg * lax.rsqrt(ms + EPS) * ng_ref[:, sl]).astype(y_ref.dtype)


def _ssd_call(xbc, dt, z, conv_w, conv_b, dt_bias, a_log, d_skip, ssd_norm_g):
    bsz, s, c = xbc.shape
    width = z.shape[-1]
    gn = SSD_GROUPS * SSD_STATE
    heads = width // SSD_HEAD_DIM
    L = min(SSD_CHUNK, s)
    pad = lambda v: jnp.zeros((1, LANES), F32).at[0, :heads].set(v)
    expand = (jnp.arange(LANES)[:, None] == (jnp.arange(width)[None, :] // SSD_HEAD_DIM)).astype(F32)
    dsk = jnp.repeat(d_skip, SSD_HEAD_DIM).reshape(1, width)
    hb = L // HALO
    const = lambda shp: pl.BlockSpec(shp, lambda b, i: (0, 0))
    return pl.pallas_call(
        functools.partial(_ssd_kernel, chunk=L, width=width, gn=gn),
        grid=(bsz, s // L),
        in_specs=[pl.BlockSpec((1, L, c), lambda b, i: (b, i, 0)),
                  pl.BlockSpec((1, HALO, c), lambda b, i: (b, jnp.maximum(i * hb - 1, 0), 0)),
                  pl.BlockSpec((1, L, LANES), lambda b, i: (b, i, 0)),
                  pl.BlockSpec((1, L, width), lambda b, i: (b, i, 0)),
                  const((CONV_WIDTH, c)), const((1, c)), const((1, LANES)), const((1, LANES)),
                  const((LANES, width)), const((1, width)), const((1, width))],
        out_specs=pl.BlockSpec((1, L, width), lambda b, i: (b, i, 0)),
        out_shape=jax.ShapeDtypeStruct((bsz, s, width), BF16),
        scratch_shapes=[pltpu.VMEM((gn, width), F32)],
        compiler_params=_cparams(("arbitrary", "arbitrary")),
        name="ssd_mixer",
    )(xbc, xbc, dt, z, conv_w, conv_b.reshape(1, c), pad(dt_bias), pad(a_log), expand, dsk,
      ssd_norm_g.reshape(1, width))


def _pool_kernel(xp_ref, halo_ref, w_ref, sc_ref, y_ref, *, tm, gdim):
    i = pl.program_id(1)
    hal = jnp.where(i > 0, halo_ref[0].astype(F32), 0.0)
    x = xp_ref[0].astype(F32)
    full = jnp.concatenate([hal, x], axis=0)
    lane_g = lax.broadcasted_iota(jnp.int32, x.shape, 1) // gdim
    tpos = i * tm + lax.broadcasted_iota(jnp.int32, x.shape, 0)
    win_sum = jnp.zeros_like(x)
    cnt = jnp.ones_like(x)
    s = full
    span = 1
    for gi, w in enumerate(POOL_WINDOWS):
        while span < w:
            s = s + pltpu.roll(s, span, 0)
            span *= 2
        win_sum = jnp.where(lane_g == gi, s[HALO:], win_sum)
        cnt = jnp.where(lane_g == gi, jnp.minimum(tpos + 1, w).astype(F32), cnt)
    pooled = (win_sum / cnt - x).astype(BF16)
    y = jnp.dot(pooled, w_ref[...], preferred_element_type=F32) * sc_ref[...]
    y_ref[0] = y.astype(y_ref.dtype)


def _pool_call(xp, pool_w, pool_scale):
    bsz, s, w = xp.shape
    ng, gdim, _ = pool_w.shape
    assert POOL_WINDOWS[-1] <= HALO and all(b == 2 * a for a, b in zip(POOL_WINDOWS, POOL_WINDOWS[1:]))
    tm = min(TM_POOL, s)
    wbd = jnp.zeros((w, w), F32)
    for g in range(ng):
        wbd = wbd.at[g * gdim:(g + 1) * gdim, g * gdim:(g + 1) * gdim].set(pool_w[g])
    hb = tm // HALO
    return pl.pallas_call(
        functools.partial(_pool_kernel, tm=tm, gdim=gdim),
        grid=(bsz, s // tm),
        in_specs=[pl.BlockSpec((1, tm, w), lambda b, i: (b, i, 0)),
                  pl.BlockSpec((1, HALO, w), lambda b, i: (b, jnp.maximum(i * hb - 1, 0), 0)),
                  pl.BlockSpec((w, w), lambda b, i: (0, 0)),
                  pl.BlockSpec((1, w), lambda b, i: (0, 0))],
        out_specs=pl.BlockSpec((1, tm, w), lambda b, i: (b, i, 0)),
        out_shape=jax.ShapeDtypeStruct((bsz, s, w), BF16),
        compiler_params=_cparams(("arbitrary", "arbitrary")),
        name="pool_mixer",
    )(xp, xp, wbd.astype(BF16), pool_scale.reshape(1, w))


def _sb_kernel(q_ref, k_ref, v_ref, y_ref, qm_ref, acc_ref, carry_ref, *, blk, heads, nq):
    i = pl.program_id(1)
    T = blk
    w = q_ref.shape[-1]
    lane_h = lax.broadcasted_iota(jnp.int32, (T, w), 1) // SB_HEAD_DIM
    for u in range(nq):
        q = q_ref[0, u * T:(u + 1) * T, :] * (SB_HEAD_DIM ** -0.5)
        for h in range(heads):
            qm_ref[u, h * T:(h + 1) * T, :] = jnp.where(lane_h == h, q, jnp.zeros_like(q))
    row_i = lax.broadcasted_iota(jnp.int32, (T, T), 0)
    col_i = lax.broadcasted_iota(jnp.int32, (T, T), 1)
    suffix = (row_i > col_i).astype(BF16)
    qrow = lax.broadcasted_iota(jnp.int32, (heads * T, T), 0) % T
    causal = lax.broadcasted_iota(jnp.int32, (heads * T, T), 1) < qrow

    def scores(u, j, mask):
        ks = k_ref[0, pl.ds(pl.multiple_of(j * T, T), T), :]
        z = _nt_dot(qm_ref[u], ks)
        lk = -_softplus(z)
        if mask is not None:
            lk = jnp.where(mask, lk, 0.0)
        hi = lk.astype(BF16)
        lo = (lk - hi.astype(F32)).astype(BF16)
        later = (jnp.dot(hi, suffix, preferred_element_type=F32)
                 + jnp.dot(lo, suffix, preferred_element_type=F32))
        return z, lk, later

    def weighted(j, z, lk, later, c, mask):
        vs = v_ref[0, pl.ds(pl.multiple_of(j * T, T), T), :]
        wgt = jnp.exp(z + lk + later + c)
        if mask is not None:
            wgt = jnp.where(mask, wgt, 0.0)
        pv = jnp.dot(wgt.astype(BF16), vs, preferred_element_type=F32)
        out = pv[0:T]
        for h in range(1, heads):
            out = jnp.where(lane_h == h, pv[h * T:(h + 1) * T], out)
        return out, c + jnp.sum(lk, axis=1, keepdims=True)

    def alive():
        return (jnp.max(carry_ref[...]) > -SB_SKIP).astype(jnp.int32)

    first = []
    for u in range(nq):
        qb = i * nq + u
        prev = jnp.maximum(qb - 1, 0)
        has_prev = None if u > 0 else jnp.broadcast_to(qb > 0, causal.shape)
        first.append((qb, prev, has_prev, scores(u, qb, causal), scores(u, prev, has_prev)))
    for u, (qb, prev, has_prev, sd, sp) in enumerate(first):
        out_d, c = weighted(qb, *sd, jnp.zeros((heads * T, 1), F32), causal)
        out_p, c = weighted(prev, *sp, c, has_prev)
        acc_ref[u] = out_d + out_p
        carry_ref[u] = c

    def cond(st):
        return jnp.logical_and(i * nq + nq - 3 - st[0] >= 0, st[1] > 0)

    def body(st):
        n = st[0]
        for u in range(nq):
            j = i * nq + u - 2 - n
            valid = jnp.broadcast_to(j >= 0, causal.shape)
            jj = jnp.maximum(j, 0)
            out, c = weighted(jj, *scores(u, jj, valid), carry_ref[u], valid)
            acc_ref[u] += out
            carry_ref[u] = c
        return n + 1, alive()

    lax.while_loop(cond, body, (0, alive()))
    for u in range(nq):
        y_ref[0, u * T:(u + 1) * T, :] = acc_ref[u].astype(y_ref.dtype)


def _sb_call(q, k, v):
    bsz, s, w = q.shape
    heads = w // SB_HEAD_DIM
    T = min(SB_BLOCK, s)
    nq = SB_QBLOCKS if (s // T) % SB_QBLOCKS == 0 else 1
    return pl.pallas_call(
        functools.partial(_sb_kernel, blk=T, heads=heads, nq=nq),
        grid=(bsz, s // (nq * T)),
        in_specs=[pl.BlockSpec((1, nq * T, w), lambda b, i: (b, i, 0)),
                  pl.BlockSpec((1, s, w), lambda b, i: (b, 0, 0)),
                  pl.BlockSpec((1, s, w), lambda b, i: (b, 0, 0))],
        out_specs=pl.BlockSpec((1, nq * T, w), lambda b, i: (b, i, 0)),
        out_shape=jax.ShapeDtypeStruct((bsz, s, w), BF16),
        scratch_shapes=[pltpu.VMEM((nq, heads * T, w), BF16),
                        pltpu.VMEM((nq, T, w), F32),
                        pltpu.VMEM((nq, heads * T, 1), F32)],
        compiler_params=_cparams(("arbitrary", "arbitrary")),
        name="stick_breaking",
    )(q, k, v)


def _out_kernel(ys_ref, yp_ref, yb_ref, x_ref, g1_ref, sh_ref, sc_ref, ng_ref, wo_ref, wq_ref,
                keys_ref, x1_ref, h2_ref, sct_ref, *, cuts):
    mix = jnp.zeros(x_ref.shape[1:], F32)
    for y_ref, (a, b) in zip((ys_ref, yp_ref, yb_ref), cuts):
        mix = mix + jnp.dot(y_ref[0], wo_ref[a:b, :], preferred_element_type=F32)
    x1 = x_ref[0] + g1_ref[0] * mix
    x1_ref[0] = x1
    h2 = _norm_mod(x1, ng_ref[...], sh_ref[0], sc_ref[0]).astype(BF16)
    h2_ref[0] = h2
    qb = jnp.dot(h2, wq_ref[...], preferred_element_type=F32).astype(BF16)
    tp = sct_ref.shape[-1]
    for lst in range(keys_ref.shape[0]):
        sc_t = _nt_dot(keys_ref[lst], qb[:, lst * PEER_HALF:(lst + 1) * PEER_HALF])
        for k in range(sct_ref.shape[0]):
            sct_ref[k, lst] = sc_t[:, k * tp:(k + 1) * tp]


def _out_call(ys, yp, yb, x, g1, sh, sc, ng, w_out, wq, keys):
    bsz, s, d = x.shape
    tm = min(TM_IN, s)
    nlist = keys.shape[0]
    cuts, a = [], 0
    for y in (ys, yp, yb):
        cuts.append((a, a + y.shape[-1]))
        a += y.shape[-1]
    tok = lambda w: pl.BlockSpec((1, tm, w), lambda b, i: (b, i, 0))
    vec = pl.BlockSpec((1, 1, d), lambda b, i: (b, 0, 0))
    full2 = lambda arr: pl.BlockSpec(arr.shape, lambda b, i: (0,) * arr.ndim)
    nblk = s // tm
    tp = min(TB_PEER, s)
    per = tm // tp
    return pl.pallas_call(
        functools.partial(_out_kernel, cuts=tuple(cuts)),
        grid=(bsz, nblk),
        in_specs=[tok(ys.shape[-1]), tok(yp.shape[-1]), tok(yb.shape[-1]), tok(d), vec, vec, vec,
                  pl.BlockSpec((1, d), lambda b, i: (0, 0)), full2(w_out), full2(wq), full2(keys)],
        out_specs=[tok(d), tok(d),
                   pl.BlockSpec((per, nlist, N_KEYS, tp), lambda b, i: (b * nblk + i, 0, 0, 0))],
        out_shape=[jax.ShapeDtypeStruct((bsz, s, d), F32),
                   jax.ShapeDtypeStruct((bsz, s, d), BF16),
                   jax.ShapeDtypeStruct((bsz * s // tp, nlist, N_KEYS, tp), F32)],
        compiler_params=_cparams(("arbitrary", "arbitrary")),
        name="out_proj_peer_query",
    )(ys, yp, yb, x, g1, sh, sc, ng, w_out, wq, keys)


def _top_rows(s, k, payload=None):
    nrows = s.shape[0]
    rows = lax.broadcasted_iota(jnp.int32, s.shape, 0).astype(F32)
    vals, picks = [], []
    for _ in range(k):
        m = jnp.max(s, axis=0, keepdims=True)
        idx = jnp.min(jnp.where(s == m, rows, float(nrows)), axis=0, keepdims=True)
        hit = rows == idx
        vals.append(m)
        if payload is None:
            picks.append(idx)
        else:
            picks.append(jnp.sum(jnp.where(hit, payload, 0.0), axis=0, keepdims=True))
        s = jnp.where(hit, -jnp.inf, s)
    return jnp.concatenate(vals, axis=0), jnp.concatenate(picks, axis=0)


def _pair_candidates(v1, x1, v2, x2):
    K = PEER_TOPK
    cands, ids = [], []
    a = 0
    while K // (a + 1) > 1:
        n = K // (a + 1)
        npad = -(-n // SUBLANES) * SUBLANES
        c = v1[a:a + 1, :] + v2[0:npad, :]
        if n < npad:
            c = jnp.where(lax.broadcasted_iota(jnp.int32, c.shape, 0) < n, c, -jnp.inf)
        cands.append(c)
        ids.append(x1[a:a + 1, :] * float(N_KEYS) + x2[0:npad, :])
        a += 1
    cands.append(v1[a:K, :] + v2[0:1, :])
    ids.append(x1[a:K, :] * float(N_KEYS) + x2[0:1, :])
    return jnp.concatenate(cands, axis=0), jnp.concatenate(ids, axis=0)


def _peer_kernel(sct_ref, h_ref, u_ref, v_ref, x_ref, g2_ref, fg_ref, o_ref,
                 pk_ref, pt_ref, scr_ref, g_ref, coef_ref, acc_ref, *, final):
    s = pl.program_id(0)
    j = pl.program_id(1)
    K = PEER_TOPK
    slot = s % 2
    eb = u_ref.shape[0]

    @pl.when(jnp.logical_and(s == 0, j == 0))
    def _():
        pk_ref[...] = jnp.zeros_like(pk_ref)
        scr_ref[...] = jnp.zeros_like(scr_ref)
        g_ref[...] = jnp.zeros_like(g_ref)
        coef_ref[...] = jnp.zeros_like(coef_ref)
        acc_ref[...] = jnp.zeros_like(acc_ref)

    @pl.when(j == 0)
    def _():
        for k in range(pk_ref.shape[0]):
            pt_ref[k] = pk_ref[k].T

    v1, x1 = _top_rows(sct_ref[0], K)
    v2, x2 = _top_rows(sct_ref[1], K)
    cand, ids = _pair_candidates(v1, x1, v2, x2)
    best, expert = _top_rows(cand, K, payload=ids)
    e = jnp.exp(best - best[0:1, :])
    gate = e / jnp.sum(e, axis=0, keepdims=True)
    i1 = jnp.floor(expert * (1.0 / N_KEYS))
    r0 = pl.multiple_of(j * K, K)
    pk_ref[0, pl.ds(r0, K), :] = i1
    pk_ref[1, pl.ds(r0, K), :] = expert - i1 * float(N_KEYS)
    pk_ref[2, pl.ds(r0, K), :] = 0.5 * gate

    npair = pt_ref.shape[-1]
    sub = lax.broadcasted_iota(jnp.int32, (N_KEYS, npair), 0).astype(F32)
    t0 = pl.multiple_of(j * GATE_GROUP, GATE_GROUP)
    for tl in range(GATE_GROUP):
        r = pl.ds(t0 + tl, 1)
        lhs = jnp.where(sub == pt_ref[0, r, :], pt_ref[2, r, :], 0.0).astype(BF16)
        rhs = jnp.where(sub == pt_ref[1, r, :], 1.0, 0.0).astype(BF16)
        scr_ref[pl.ds(tl, N_KEYS, stride=GATE_PITCH), :] = _nt_dot(lhs, rhs)
    for a in range(N_KEYS):
        g_ref[1 - slot, a, pl.ds(t0, GATE_GROUP), :] = (
            scr_ref[a * GATE_PITCH:a * GATE_PITCH + GATE_GROUP, :].astype(g_ref.dtype))

    h = h_ref[...]
    d = h.shape[-1]
    acc_ref[...] += jnp.dot(coef_ref[...], v_ref[:, :d], preferred_element_type=F32)

    a0 = j * (eb // N_KEYS)
    pieces = []
    for q in range(eb // DENSE_SUB):
        act = _nt_dot(h, u_ref[q * DENSE_SUB:(q + 1) * DENSE_SUB, :d])
        w = (act * (1.0 + lax.erf(act * (2.0 ** -0.5)))).astype(BF16)
        for a in range(DENSE_SUB // N_KEYS):
            ga = g_ref[slot, a0 + q * (DENSE_SUB // N_KEYS) + a]
            pieces.append(w[:, a * N_KEYS:(a + 1) * N_KEYS] * ga)
    coef_ref[...] = jnp.concatenate(pieces, axis=1)

    @pl.when(j == 0)
    def _():
        x2 = x_ref[...] + g2_ref[0] * acc_ref[...]
        if final:
            ms = jnp.mean(x2 * x2, axis=-1, keepdims=True)
            x2 = x2 * lax.rsqrt(ms + EPS) * fg_ref[...]
        o_ref[...] = x2
        acc_ref[...] = jnp.zeros_like(acc_ref)


def _peer_call(sct, h2, u, v, x1, g2, fg, final):
    bsz, s, d = x1.shape
    t = bsz * s
    ne = u.shape[0]
    _, nlist, nk, _ = sct.shape
    tb = min(TB_PEER, s)
    eb = ne // PEER_HEADS
    npair = PEER_HEADS * PEER_TOPK
    assert tb == PEER_HEADS * GATE_GROUP and nlist == 2 * PEER_HEADS and eb % MXU_WIDTH == 0
    nt = t // tb
    per_b = s // tb
    nchunk = PEER_HEADS
    dense_tile = lambda i: jnp.clip(i - 2, 0, nt - 1)
    row = lambda i, j: (dense_tile(i), 0)
    out_tile = lambda i, j: jnp.clip(i - 2 - (j == 0).astype(jnp.int32), 0, nt - 1)
    out_row = lambda i, j: (out_tile(i, j), 0)
    dp = d + LANES if (d // LANES) % 2 == 0 else d
    u = jnp.pad(u, ((0, 0), (0, dp - d)))
    v = jnp.pad(v, ((0, 0), (0, dp - d)))
    out = pl.pallas_call(
        functools.partial(_peer_kernel, final=final),
        grid=(nt + 3, nchunk),
        in_specs=[pl.BlockSpec((None, 2, nk, tb), lambda i, j: (jnp.minimum(i, nt - 1), j, 0, 0)),
                  pl.BlockSpec((tb, d), row),
                  pl.BlockSpec((eb, dp), lambda i, j: (j, 0)),
                  pl.BlockSpec((eb, dp), lambda i, j: ((j + nchunk - 1) % nchunk, 0)),
                  pl.BlockSpec((tb, d), out_row),
                  pl.BlockSpec((1, 1, d), lambda i, j: (out_tile(i, j) // per_b, 0, 0)),
                  pl.BlockSpec((1, d), lambda i, j: (0, 0))],
        out_specs=pl.BlockSpec((tb, d), out_row),
        out_shape=jax.ShapeDtypeStruct((t, d), F32),
        scratch_shapes=[pltpu.VMEM((3, npair, tb), F32),
                        pltpu.VMEM((3, tb, npair), F32),
                        pltpu.VMEM((N_KEYS * GATE_PITCH, N_KEYS), F32),
                        pltpu.VMEM((2, N_KEYS, tb, N_KEYS), BF16),
                        pltpu.VMEM((tb, eb), BF16),
                        pltpu.VMEM((tb, d), F32)],
        compiler_params=_cparams(("arbitrary", "arbitrary")),
        name="peer_fused",
    )(sct, h2.reshape(t, d), u, v, x1.reshape(t, d), g2, fg)
    return out.reshape(bsz, s, d)


def kernel(x, c, ada_w, ada_b, mix_norm_g, ffn_norm_g, w_in, conv_w, conv_b, dt_bias, a_log, d_skip,
           ssd_norm_g, pool_w, pool_scale, w_out, peer_wq, peer_keys, peer_u, peer_v, final_norm_g):
    bsz, s, d = x.shape
    depth = ada_w.shape[0]
    ssd_w = ssd_norm_g.shape[-1]
    heads = dt_bias.shape[-1]
    xbc_w = conv_w.shape[-1]
    pool_wd = pool_scale.shape[-1]
    sb_w = (w_in.shape[-1] - ssd_w - xbc_w - heads - pool_wd) // 3
    o_z, o_xbc, o_dt, o_xp = 0, ssd_w, ssd_w + xbc_w, ssd_w + xbc_w + heads
    o_q = o_xp + pool_wd
    widths = (ssd_w, xbc_w, pool_wd, sb_w, sb_w, sb_w, LANES)

    mods = _mod_call(c, ada_w, ada_b)
    for l in range(depth):
        mod = [mods[l, :, k * d:(k + 1) * d].reshape(bsz, 1, d) for k in range(6)]
        sh1, sc1, g1, sh2, sc2, g2 = mod
        wl = w_in[l]
        w_cat = jnp.concatenate(
            [wl[:, o_z:o_xbc], wl[:, o_xbc:o_dt], wl[:, o_xp:o_q], wl[:, o_q:],
             jnp.pad(wl[:, o_dt:o_xp], ((0, 0), (0, LANES - heads)))], axis=1).astype(BF16)
        z, xbc, xp, q, k, v, dt = _in_call(x, sh1, sc1, mix_norm_g[l].reshape(1, d), w_cat, widths)
        y_ssd = _ssd_call(xbc, dt, z, conv_w[l], conv_b[l], dt_bias[l], a_log[l], d_skip[l],
                          ssd_norm_g[l])
        y_pool = _pool_call(xp, pool_w[l], pool_scale[l])
        y_sb = _sb_call(q, k, v)
        keys = peer_keys[l].reshape(-1, N_KEYS, PEER_HALF).astype(BF16)
        x1, h2, sct = _out_call(y_ssd, y_pool, y_sb, x, g1, sh2, sc2, ffn_norm_g[l].reshape(1, d),
                                w_out[l].astype(BF16), peer_wq[l].astype(BF16), keys)
        x = _peer_call(sct, h2, peer_u[l].astype(BF16), peer_v[l].astype(BF16), x1, g2,
                       final_norm_g.reshape(1, d), final=(l == depth - 1))
    return x
```

```python
import functools

import jax
import jax.numpy as jnp
from jax import lax
from jax.experimental import pallas as pl
from jax.experimental.pallas import tpu as pltpu

F32 = jnp.float32
BF16 = jnp.bfloat16
EPS = 1e-6
HIGHEST = lax.Precision.HIGHEST

SSD_HEAD_DIM = 64
SSD_GROUPS = 2
SSD_STATE = 64
CONV_WIDTH = 4
POOL_WINDOWS = (2, 4, 8, 16)
SB_HEAD_DIM = 64
PEER_HEADS = 8
PEER_TOPK = 16
N_KEYS = 128
PEER_HALF = 128

LANES = 128
SUBLANES = 8
MXU_WIDTH = 256
VMEM_LIMIT = 56 * 1024 * 1024

TM_IN = 512
TM_INPROJ = 1024
SSD_CHUNK = 256
HALO = 16
TM_POOL = 2048
SB_BLOCK = 128
SB_SKIP = 40.0
SB_QBLOCKS = 4
GATE_GROUP = 32
GATE_PITCH = 36
TB_PEER = 256
DENSE_SUB = 256


def _cparams(sem):
    return pltpu.CompilerParams(dimension_semantics=sem, vmem_limit_bytes=VMEM_LIMIT)


def _nt_dot(a, b):
    return lax.dot_general(a, b, (((1,), (1,)), ((), ())), preferred_element_type=F32)


def _softplus(x):
    return jnp.maximum(x, 0.0) + jnp.log(1.0 + jnp.exp(-jnp.abs(x)))


def _silu(x):
    return x * jax.nn.sigmoid(x)


def _mod_kernel(c_ref, w_ref, b_ref, o_ref):
    cond = _silu(c_ref[...])
    o_ref[0] = jnp.dot(cond, w_ref[0], preferred_element_type=F32, precision=HIGHEST) + b_ref[0]


def _mod_call(c, ada_w, ada_b):
    nl, d, n6 = ada_w.shape
    bsz = c.shape[0]
    tn = 1536
    return pl.pallas_call(
        _mod_kernel,
        grid=(nl, n6 // tn),
        in_specs=[pl.BlockSpec((bsz, d), lambda l, j: (0, 0)),
                  pl.BlockSpec((1, d, tn), lambda l, j: (l, 0, j)),
                  pl.BlockSpec((1, 1, tn), lambda l, j: (l, 0, j))],
        out_specs=pl.BlockSpec((1, bsz, tn), lambda l, j: (l, 0, j)),
        out_shape=jax.ShapeDtypeStruct((nl, bsz, n6), F32),
        compiler_params=_cparams(("arbitrary", "arbitrary")),
        name="adaln_mod",
    )(c, ada_w, ada_b.reshape(nl, 1, n6))


def _norm_mod(x, g, sh, sc):
    ms = jnp.mean(x * x, axis=-1, keepdims=True)
    y = x * lax.rsqrt(ms + EPS) * g
    return y * (1.0 + sc) + sh


def _in_kernel(x_ref, sh_ref, sc_ref, g_ref, w_ref, z_ref, xbc_ref, xp_ref, q_ref, k_ref, v_ref,
               dt_ref, *, cuts):
    h = _norm_mod(x_ref[0], g_ref[...], sh_ref[0], sc_ref[0]).astype(BF16)
    outs = (z_ref, xbc_ref, xp_ref, q_ref, k_ref, v_ref, dt_ref)
    for o_ref, (a, b) in zip(outs, cuts):
        o_ref[0] = jnp.dot(h, w_ref[:, a:b], preferred_element_type=F32).astype(o_ref.dtype)


def _in_call(x, sh, sc, g, w_cat, widths):
    bsz, s, d = x.shape
    tm = min(TM_INPROJ, s)
    cuts, a = [], 0
    for w in widths:
        cuts.append((a, a + w))
        a += w
    dtypes = (BF16, BF16, BF16, BF16, BF16, BF16, F32)
    tok = lambda w: pl.BlockSpec((1, tm, w), lambda b, i: (b, i, 0))
    vec = pl.BlockSpec((1, 1, d), lambda b, i: (b, 0, 0))
    return pl.pallas_call(
        functools.partial(_in_kernel, cuts=tuple(cuts)),
        grid=(bsz, s // tm),
        in_specs=[tok(d), vec, vec,
                  pl.BlockSpec((1, d), lambda b, i: (0, 0)),
                  pl.BlockSpec(w_cat.shape, lambda b, i: (0, 0))],
        out_specs=[tok(w) for w in widths],
        out_shape=[jax.ShapeDtypeStruct((bsz, s, w), dt) for w, dt in zip(widths, dtypes)],
        compiler_params=_cparams(("arbitrary", "arbitrary")),
        name="in_proj",
    )(x, sh, sc, g, w_cat)


def _ssd_kernel(xbc_ref, halo_ref, dt_ref, z_ref, cw_ref, cb_ref, dtb_ref, alog_ref, expand_ref,
                dsk_ref, ng_ref, y_ref, state_ref, *, chunk, width, gn):
    i = pl.program_id(1)
    L = chunk
    heads = width // SSD_HEAD_DIM
    hpg = heads // SSD_GROUPS
    half = width // SSD_GROUPS

    @pl.when(i == 0)
    def _():
        state_ref[...] = jnp.zeros_like(state_ref)

    hal = jnp.where(i > 0, halo_ref[0].astype(F32), 0.0)
    full = jnp.concatenate([hal, xbc_ref[0].astype(F32)], axis=0)
    cw = cw_ref[...]
    acc = full * cw[CONV_WIDTH - 1:CONV_WIDTH]
    for j in range(1, CONV_WIDTH):
        acc = acc + pltpu.roll(full, j, 0) * cw[CONV_WIDTH - 1 - j:CONV_WIDTH - j]
    xbc = _silu(acc[HALO:] + cb_ref[...])
    xs = xbc[:, :width]
    bm = xbc[:, width:width + gn]
    cm = xbc[:, width + gn:width + 2 * gn]

    dt = _softplus(dt_ref[0] + dtb_ref[...])
    a = dt * (-jnp.exp(alog_ref[...]))
    row_i = lax.broadcasted_iota(jnp.int32, (L, L), 0)
    col_i = lax.broadcasted_iota(jnp.int32, (L, L), 1)
    tril = row_i >= col_i
    a_cs = jnp.dot(tril.astype(F32), a, preferred_element_type=F32, precision=HIGHEST)
    a_cs_t = a_cs.T
    expand = expand_ref[...]
    dt_x = jnp.dot(dt, expand, preferred_element_type=F32, precision=HIGHEST)
    acs_x = jnp.dot(a_cs, expand, preferred_element_type=F32, precision=HIGHEST)
    alast_x = acs_x[L - 1:L, :]
    xdt = xs * dt_x

    bm_b = bm.astype(BF16)
    cm_b = cm.astype(BF16)
    lane_g = lax.broadcasted_iota(jnp.int32, (L, gn), 1) // SSD_STATE
    lane_h = lax.broadcasted_iota(jnp.int32, (L, half), 1) // SSD_HEAD_DIM

    state = state_ref[...]
    y_off = jnp.dot(cm_b, state.astype(BF16), preferred_element_type=F32) * jnp.exp(acs_x)

    y_halves = []
    for g in range(SSD_GROUPS):
        cb = _nt_dot(jnp.where(lane_g == g, cm_b, jnp.zeros_like(cm_b)), bm_b)
        xdt_g = xdt[:, g * half:(g + 1) * half]
        yh = jnp.zeros((L, half), F32)
        for hh in range(hpg):
            h = g * hpg + hh
            seg = a_cs[:, h:h + 1] - a_cs_t[h:h + 1, :]
            dec = jnp.exp(jnp.where(tril, seg, -1e30))
            m = (cb * dec).astype(BF16)
            rhs = jnp.where(lane_h == hh, xdt_g, 0.0).astype(BF16)
            yh = yh + jnp.dot(m, rhs, preferred_element_type=F32)
        y_halves.append(yh)

    ds_x = jnp.exp(alast_x - acs_x)
    upd = jnp.dot(bm.T.astype(BF16), (xdt * ds_x).astype(BF16), preferred_element_type=F32)
    srow = lax.broadcasted_iota(jnp.int32, (gn, width), 0) // SSD_STATE
    scol = lax.broadcasted_iota(jnp.int32, (gn, width), 1) // half
    state_ref[...] = jnp.exp(alast_x) * state + jnp.where(srow == scol, upd, 0.0)

    zg = _silu(z_ref[0].astype(F32))
    for g in range(SSD_GROUPS):
        sl = slice(g * half, (g + 1) * half)
        yg = (y_halves[g] + y_off[:, sl] + dsk_ref[:, sl] * xs[:, sl]) * zg[:, sl]
        ms = jnp.mean(yg * yg, axis=-1, keepdims=True)
        y_ref[0, :, sl] = (yg * lax.rsqrt(ms + EPS) * ng_ref[:, sl]).astype(y_ref.dtype)


def _ssd_call(xbc, dt, z, conv_w, conv_b, dt_bias, a_log, d_skip, ssd_norm_g):
    bsz, s, c = xbc.shape
    width = z.shape[-1]
    gn = SSD_GROUPS * SSD_STATE
    heads = width // SSD_HEAD_DIM
    L = min(SSD_CHUNK, s)
    pad = lambda v: jnp.zeros((1, LANES), F32).at[0, :heads].set(v)
    expand = (jnp.arange(LANES)[:, None] == (jnp.arange(width)[None, :] // SSD_HEAD_DIM)).astype(F32)
    dsk = jnp.repeat(d_skip, SSD_HEAD_DIM).reshape(1, width)
    hb = L // HALO
    const = lambda shp: pl.BlockSpec(shp, lambda b, i: (0, 0))
    return pl.pallas_call(
        functools.partial(_ssd_kernel, chunk=L, width=width, gn=gn),
        grid=(bsz, s // L),
        in_specs=[pl.BlockSpec((1, L, c), lambda b, i: (b, i, 0)),
                  pl.BlockSpec((1, HALO, c), lambda b, i: (b, jnp.maximum(i * hb - 1, 0), 0)),
                  pl.BlockSpec((1, L, LANES), lambda b, i: (b, i, 0)),
                  pl.BlockSpec((1, L, width), lambda b, i: (b, i, 0)),
                  const((CONV_WIDTH, c)), const((1, c)), const((1, LANES)), const((1, LANES)),
                  const((LANES, width)), const((1, width)), const((1, width))],
        out_specs=pl.BlockSpec((1, L, width), lambda b, i: (b, i, 0)),
        out_shape=jax.ShapeDtypeStruct((bsz, s, width), BF16),
        scratch_shapes=[pltpu.VMEM((gn, width), F32)],
        compiler_params=_cparams(("arbitrary", "arbitrary")),
        name="ssd_mixer",
    )(xbc, xbc, dt, z, conv_w, conv_b.reshape(1, c), pad(dt_bias), pad(a_log), expand, dsk,
      ssd_norm_g.reshape(1, width))


def _pool_kernel(xp_ref, halo_ref, w_ref, sc_ref, y_ref, *, tm, gdim):
    i = pl.program_id(1)
    hal = jnp.where(i > 0, halo_ref[0].astype(F32), 0.0)
    x = xp_ref[0].astype(F32)
    full = jnp.concatenate([hal, x], axis=0)
    lane_g = lax.broadcasted_iota(jnp.int32, x.shape, 1) // gdim
    tpos = i * tm + lax.broadcasted_iota(jnp.int32, x.shape, 0)
    win_sum = jnp.zeros_like(x)
    cnt = jnp.ones_like(x)
    s = full
    span = 1
    for gi, w in enumerate(POOL_WINDOWS):
        while span < w:
            s = s + pltpu.roll(s, span, 0)
            span *= 2
        win_sum = jnp.where(lane_g == gi, s[HALO:], win_sum)
        cnt = jnp.where(lane_g == gi, jnp.minimum(tpos + 1, w).astype(F32), cnt)
    pooled = (win_sum / cnt - x).astype(BF16)
    y = jnp.dot(pooled, w_ref[...], preferred_element_type=F32) * sc_ref[...]
    y_ref[0] = y.astype(y_ref.dtype)


def _pool_call(xp, pool_w, pool_scale):
    bsz, s, w = xp.shape
    ng, gdim, _ = pool_w.shape
    assert POOL_WINDOWS[-1] <= HALO and all(b == 2 * a for a, b in zip(POOL_WINDOWS, POOL_WINDOWS[1:]))
    tm = min(TM_POOL, s)
    wbd = jnp.zeros((w, w), F32)
    for g in range(ng):
        wbd = wbd.at[g * gdim:(g + 1) * gdim, g * gdim:(g + 1) * gdim].set(pool_w[g])
    hb = tm // HALO
    return pl.pallas_call(
        functools.partial(_pool_kernel, tm=tm, gdim=gdim),
        grid=(bsz, s // tm),
        in_specs=[pl.BlockSpec((1, tm, w), lambda b, i: (b, i, 0)),
                  pl.BlockSpec((1, HALO, w), lambda b, i: (b, jnp.maximum(i * hb - 1, 0), 0)),
                  pl.BlockSpec((w, w), lambda b, i: (0, 0)),
                  pl.BlockSpec((1, w), lambda b, i: (0, 0))],
        out_specs=pl.BlockSpec((1, tm, w), lambda b, i: (b, i, 0)),
        out_shape=jax.ShapeDtypeStruct((bsz, s, w), BF16),
        compiler_params=_cparams(("arbitrary", "arbitrary")),
        name="pool_mixer",
    )(xp, xp, wbd.astype(BF16), pool_scale.reshape(1, w))


def _sb_kernel(q_ref, k_ref, v_ref, y_ref, qm_ref, acc_ref, carry_ref, *, blk, heads, nq):
    i = pl.program_id(1)
    T = blk
    w = q_ref.shape[-1]
    lane_h = lax.broadcasted_iota(jnp.int32, (T, w), 1) // SB_HEAD_DIM
    for u in range(nq):
        q = q_ref[0, u * T:(u + 1) * T, :] * (SB_HEAD_DIM ** -0.5)
        for h in range(heads):
            qm_ref[u, h * T:(h + 1) * T, :] = jnp.where(lane_h == h, q, jnp.zeros_like(q))
    row_i = lax.broadcasted_iota(jnp.int32, (T, T), 0)
    col_i = lax.broadcasted_iota(jnp.int32, (T, T), 1)
    suffix = (row_i > col_i).astype(BF16)
    qrow = lax.broadcasted_iota(jnp.int32, (heads * T, T), 0) % T
    causal = lax.broadcasted_iota(jnp.int32, (heads * T, T), 1) < qrow

    def scores(u, j, mask):
        ks = k_ref[0, pl.ds(pl.multiple_of(j * T, T), T), :]
        z = _nt_dot(qm_ref[u], ks)
        lk = -_softplus(z)
        if mask is not None:
            lk = jnp.where(mask, lk, 0.0)
        hi = lk.astype(BF16)
        lo = (lk - hi.astype(F32)).astype(BF16)
        later = (jnp.dot(hi, suffix, preferred_element_type=F32)
                 + jnp.dot(lo, suffix, preferred_element_type=F32))
        return z, lk, later

    def weighted(j, z, lk, later, c, mask):
        vs = v_ref[0, pl.ds(pl.multiple_of(j * T, T), T), :]
        wgt = jnp.exp(z + lk + later + c)
        if mask is not None:
            wgt = jnp.where(mask, wgt, 0.0)
        pv = jnp.dot(wgt.astype(BF16), vs, preferred_element_type=F32)
        out = pv[0:T]
        for h in range(1, heads):
            out = jnp.where(lane_h == h, pv[h * T:(h + 1) * T], out)
        return out, c + jnp.sum(lk, axis=1, keepdims=True)

    def alive():
        return (jnp.max(carry_ref[...]) > -SB_SKIP).astype(jnp.int32)

    first = []
    for u in range(nq):
        qb = i * nq + u
        prev = jnp.maximum(qb - 1, 0)
        has_prev = None if u > 0 else jnp.broadcast_to(qb > 0, causal.shape)
        first.append((qb, prev, has_prev, scores(u, qb, causal), scores(u, prev, has_prev)))
    for u, (qb, prev, has_prev, sd, sp) in enumerate(first):
        out_d, c = weighted(qb, *sd, jnp.zeros((heads * T, 1), F32), causal)
        out_p, c = weighted(prev, *sp, c, has_prev)
        acc_ref[u] = out_d + out_p
        carry_ref[u] = c

    def cond(st):
        return jnp.logical_and(i * nq + nq - 3 - st[0] >= 0, st[1] > 0)

    def body(st):
        n = st[0]
        for u in range(nq):
            j = i * nq + u - 2 - n
            valid = jnp.broadcast_to(j >= 0, causal.shape)
            jj = jnp.maximum(j, 0)
            out, c = weighted(jj, *scores(u, jj, valid), carry_ref[u], valid)
            acc_ref[u] += out
            carry_ref[u] = c
        return n + 1, alive()

    lax.while_loop(cond, body, (0, alive()))
    for u in range(nq):
        y_ref[0, u * T:(u + 1) * T, :] = acc_ref[u].astype(y_ref.dtype)


def _sb_call(q, k, v):
    bsz, s, w = q.shape
    heads = w // SB_HEAD_DIM
    T = min(SB_BLOCK, s)
    nq = SB_QBLOCKS if (s // T) % SB_QBLOCKS == 0 else 1
    return pl.pallas_call(
        functools.partial(_sb_kernel, blk=T, heads=heads, nq=nq),
        grid=(bsz, s // (nq * T)),
        in_specs=[pl.BlockSpec((1, nq * T, w), lambda b, i: (b, i, 0)),
                  pl.BlockSpec((1, s, w), lambda b, i: (b, 0, 0)),
                  pl.BlockSpec((1, s, w), lambda b, i: (b, 0, 0))],
        out_specs=pl.BlockSpec((1, nq * T, w), lambda b, i: (b, i, 0)),
        out_shape=jax.ShapeDtypeStruct((bsz, s, w), BF16),
        scratch_shapes=[pltpu.VMEM((nq, heads * T, w), BF16),
                        pltpu.VMEM((nq, T, w), F32),
                        pltpu.VMEM((nq, heads * T, 1), F32)],
        compiler_params=_cparams(("arbitrary", "arbitrary")),
        name="stick_breaking",
    )(q, k, v)


def _out_kernel(ys_ref, yp_ref, yb_ref, x_ref, g1_ref, sh_ref, sc_ref, ng_ref, wo_ref, wq_ref,
                keys_ref, x1_ref, h2_ref, sct_ref, *, cuts):
    mix = jnp.zeros(x_ref.shape[1:], F32)
    for y_ref, (a, b) in zip((ys_ref, yp_ref, yb_ref), cuts):
        mix = mix + jnp.dot(y_ref[0], wo_ref[a:b, :], preferred_element_type=F32)
    x1 = x_ref[0] + g1_ref[0] * mix
    x1_ref[0] = x1
    h2 = _norm_mod(x1, ng_ref[...], sh_ref[0], sc_ref[0]).astype(BF16)
    h2_ref[0] = h2
    qb = jnp.dot(h2, wq_ref[...], preferred_element_type=F32).astype(BF16)
    tp = sct_ref.shape[-1]
    for lst in range(keys_ref.shape[0]):
        sc_t = _nt_dot(keys_ref[lst], qb[:, lst * PEER_HALF:(lst + 1) * PEER_HALF])
        for k in range(sct_ref.shape[0]):
            sct_ref[k, lst] = sc_t[:, k * tp:(k + 1) * tp]


def _out_call(ys, yp, yb, x, g1, sh, sc, ng, w_out, wq, keys):
    bsz, s, d = x.shape
    tm = min(TM_IN, s)
    nlist = keys.shape[0]
    cuts, a = [], 0
    for y in (ys, yp, yb):
        cuts.append((a, a + y.shape[-1]))
        a += y.shape[-1]
    tok = lambda w: pl.BlockSpec((1, tm, w), lambda b, i: (b, i, 0))
    vec = pl.BlockSpec((1, 1, d), lambda b, i: (b, 0, 0))
    full2 = lambda arr: pl.BlockSpec(arr.shape, lambda b, i: (0,) * arr.ndim)
    nblk = s // tm
    tp = min(TB_PEER, s)
    per = tm // tp
    return pl.pallas_call(
        functools.partial(_out_kernel, cuts=tuple(cuts)),
        grid=(bsz, nblk),
        in_specs=[tok(ys.shape[-1]), tok(yp.shape[-1]), tok(yb.shape[-1]), tok(d), vec, vec, vec,
                  pl.BlockSpec((1, d), lambda b, i: (0, 0)), full2(w_out), full2(wq), full2(keys)],
        out_specs=[tok(d), tok(d),
                   pl.BlockSpec((per, nlist, N_KEYS, tp), lambda b, i: (b * nblk + i, 0, 0, 0))],
        out_shape=[jax.ShapeDtypeStruct((bsz, s, d), F32),
                   jax.ShapeDtypeStruct((bsz, s, d), BF16),
                   jax.ShapeDtypeStruct((bsz * s // tp, nlist, N_KEYS, tp), F32)],
        compiler_params=_cparams(("arbitrary", "arbitrary")),
        name="out_proj_peer_query",
    )(ys, yp, yb, x, g1, sh, sc, ng, w_out, wq, keys)


def _top_rows(s, k, payload=None):
    nrows = s.shape[0]
    rows = lax.broadcasted_iota(jnp.int32, s.shape, 0).astype(F32)
    vals, picks = [], []
    for _ in range(k):
        m = jnp.max(s, axis=0, keepdims=True)
        idx = jnp.min(jnp.where(s == m, rows, float(nrows)), axis=0, keepdims=True)
        hit = rows == idx
        vals.append(m)
        if payload is None:
            picks.append(idx)
        else:
            picks.append(jnp.sum(jnp.where(hit, payload, 0.0), axis=0, keepdims=True))
        s = jnp.where(hit, -jnp.inf, s)
    return jnp.concatenate(vals, axis=0), jnp.concatenate(picks, axis=0)


def _pair_candidates(v1, x1, v2, x2):
    K = PEER_TOPK
    cands, ids = [], []
    a = 0
    while K // (a + 1) > 1:
        n = K // (a + 1)
        npad = -(-n // SUBLANES) * SUBLANES
        c = v1[a:a + 1, :] + v2[0:npad, :]
        if n < npad:
            c = jnp.where(lax.broadcasted_iota(jnp.int32, c.shape, 0) < n, c, -jnp.inf)
        cands.append(c)
        ids.append(x1[a:a + 1, :] * float(N_KEYS) + x2[0:npad, :])
        a += 1
    cands.append(v1[a:K, :] + v2[0:1, :])
    ids.append(x1[a:K, :] * float(N_KEYS) + x2[0:1, :])
    return jnp.concatenate(cands, axis=0), jnp.concatenate(ids, axis=0)


def _peer_kernel(sct_ref, h_ref, u_ref, v_ref, x_ref, g2_ref, fg_ref, o_ref,
                 pk_ref, pt_ref, scr_ref, g_ref, coef_ref, acc_ref, *, final):
    s = pl.program_id(0)
    j = pl.program_id(1)
    K = PEER_TOPK
    slot = s % 2
    eb = u_ref.shape[0]

    @pl.when(jnp.logical_and(s == 0, j == 0))
    def _():
        pk_ref[...] = jnp.zeros_like(pk_ref)
        scr_ref[...] = jnp.zeros_like(scr_ref)
        g_ref[...] = jnp.zeros_like(g_ref)
        coef_ref[...] = jnp.zeros_like(coef_ref)
        acc_ref[...] = jnp.zeros_like(acc_ref)

    @pl.when(j == 0)
    def _():
        for k in range(pk_ref.shape[0]):
            pt_ref[k] = pk_ref[k].T

    v1, x1 = _top_rows(sct_ref[0], K)
    v2, x2 = _top_rows(sct_ref[1], K)
    cand, ids = _pair_candidates(v1, x1, v2, x2)
    best, expert = _top_rows(cand, K, payload=ids)
    e = jnp.exp(best - best[0:1, :])
    gate = e / jnp.sum(e, axis=0, keepdims=True)
    i1 = jnp.floor(expert * (1.0 / N_KEYS))
    r0 = pl.multiple_of(j * K, K)
    pk_ref[0, pl.ds(r0, K), :] = i1
    pk_ref[1, pl.ds(r0, K), :] = expert - i1 * float(N_KEYS)
    pk_ref[2, pl.ds(r0, K), :] = 0.5 * gate

    npair = pt_ref.shape[-1]
    sub = lax.broadcasted_iota(jnp.int32, (N_KEYS, npair), 0).astype(F32)
    t0 = pl.multiple_of(j * GATE_GROUP, GATE_GROUP)
    for tl in range(GATE_GROUP):
        r = pl.ds(t0 + tl, 1)
        lhs = jnp.where(sub == pt_ref[0, r, :], pt_ref[2, r, :], 0.0).astype(BF16)
        rhs = jnp.where(sub == pt_ref[1, r, :], 1.0, 0.0).astype(BF16)
        scr_ref[pl.ds(tl, N_KEYS, stride=GATE_PITCH), :] = _nt_dot(lhs, rhs)
    for a in range(N_KEYS):
        g_ref[1 - slot, a, pl.ds(t0, GATE_GROUP), :] = (
            scr_ref[a * GATE_PITCH:a * GATE_PITCH + GATE_GROUP, :].astype(g_ref.dtype))

    h = h_ref[...]
    d = h.shape[-1]
    acc_ref[...] += jnp.dot(coef_ref[...], v_ref[:, :d], preferred_element_type=F32)

    a0 = j * (eb // N_KEYS)
    pieces = []
    for q in range(eb // DENSE_SUB):
        act = _nt_dot(h, u_ref[q * DENSE_SUB:(q + 1) * DENSE_SUB, :d])
        w = (act * (1.0 + lax.erf(act * (2.0 ** -0.5)))).astype(BF16)
        for a in range(DENSE_SUB // N_KEYS):
            ga = g_ref[slot, a0 + q * (DENSE_SUB // N_KEYS) + a]
            pieces.append(w[:, a * N_KEYS:(a + 1) * N_KEYS] * ga)
    coef_ref[...] = jnp.concatenate(pieces, axis=1)

    @pl.when(j == 0)
    def _():
        x2 = x_ref[...] + g2_ref[0] * acc_ref[...]
        if final:
            ms = jnp.mean(x2 * x2, axis=-1, keepdims=True)
            x2 = x2 * lax.rsqrt(ms + EPS) * fg_ref[...]
        o_ref[...] = x2
        acc_ref[...] = jnp.zeros_like(acc_ref)


def _peer_call(sct, h2, u, v, x1, g2, fg, final):
    bsz, s, d = x1.shape
    t = bsz * s
    ne = u.shape[0]
    _, nlist, nk, _ = sct.shape
    tb = min(TB_PEER, s)
    eb = ne // PEER_HEADS
    npair = PEER_HEADS * PEER_TOPK
    assert tb == PEER_HEADS * GATE_GROUP and nlist == 2 * PEER_HEADS and eb % MXU_WIDTH == 0
    nt = t // tb
    per_b = s // tb
    nchunk = PEER_HEADS
    dense_tile = lambda i: jnp.clip(i - 2, 0, nt - 1)
    row = lambda i, j: (dense_tile(i), 0)
    out_tile = lambda i, j: jnp.clip(i - 2 - (j == 0).astype(jnp.int32), 0, nt - 1)
    out_row = lambda i, j: (out_tile(i, j), 0)
    dp = d + LANES if (d // LANES) % 2 == 0 else d
    u = jnp.pad(u, ((0, 0), (0, dp - d)))
    v = jnp.pad(v, ((0, 0), (0, dp - d)))
    out = pl.pallas_call(
        functools.partial(_peer_kernel, final=final),
        grid=(nt + 3, nchunk),
        in_specs=[pl.BlockSpec((None, 2, nk, tb), lambda i, j: (jnp.minimum(i, nt - 1), j, 0, 0)),
                  pl.BlockSpec((tb, d), row),
                  pl.BlockSpec((eb, dp), lambda i, j: (j, 0)),
                  pl.BlockSpec((eb, dp), lambda i, j: ((j + nchunk - 1) % nchunk, 0)),
                  pl.BlockSpec((tb, d), out_row),
                  pl.BlockSpec((1, 1, d), lambda i, j: (out_tile(i, j) // per_b, 0, 0)),
                  pl.BlockSpec((1, d), lambda i, j: (0, 0))],
        out_specs=pl.BlockSpec((tb, d), out_row),
        out_shape=jax.ShapeDtypeStruct((t, d), F32),
        scratch_shapes=[pltpu.VMEM((3, npair, tb), F32),
                        pltpu.VMEM((3, tb, npair), F32),
                        pltpu.VMEM((N_KEYS * GATE_PITCH, N_KEYS), F32),
                        pltpu.VMEM((2, N_KEYS, tb, N_KEYS), BF16),
                        pltpu.VMEM((tb, eb), BF16),
                        pltpu.VMEM((tb, d), F32)],
        compiler_params=_cparams(("arbitrary", "arbitrary")),
        name="peer_fused",
    )(sct, h2.reshape(t, d), u, v, x1.reshape(t, d), g2, fg)
    return out.reshape(bsz, s, d)


def kernel(x, c, ada_w, ada_b, mix_norm_g, ffn_norm_g, w_in, conv_w, conv_b, dt_bias, a_log, d_skip,
           ssd_norm_g, pool_w, pool_scale, w_out, peer_wq, peer_keys, peer_u, peer_v, final_norm_g):
    bsz, s, d = x.shape
    depth = ada_w.shape[0]
    ssd_w = ssd_norm_g.shape[-1]
    heads = dt_bias.shape[-1]
    xbc_w = conv_w.shape[-1]
    pool_wd = pool_scale.shape[-1]
    sb_w = (w_in.shape[-1] - ssd_w - xbc_w - heads - pool_wd) // 3
    o_z, o_xbc, o_dt, o_xp = 0, ssd_w, ssd_w + xbc_w, ssd_w + xbc_w + heads
    o_q = o_xp + pool_wd
    widths = (ssd_w, xbc_w, pool_wd, sb_w, sb_w, sb_w, LANES)

    mods = _mod_call(c, ada_w, ada_b)
    for l in range(depth):
        mod = [mods[l, :, k * d:(k + 1) * d].reshape(bsz, 1, d) for k in range(6)]
        sh1, sc1, g1, sh2, sc2, g2 = mod
        wl = w_in[l]
        w_cat = jnp.concatenate(
            [wl[:, o_z:o_xbc], wl[:, o_xbc:o_dt], wl[:, o_xp:o_q], wl[:, o_q:],
             jnp.pad(wl[:, o_dt:o_xp], ((0, 0), (0, LANES - heads)))], axis=1).astype(BF16)
        z, xbc, xp, q, k, v, dt = _in_call(x, sh1, sc1, mix_norm_g[l].reshape(1, d), w_cat, widths)
        y_ssd = _ssd_call(xbc, dt, z, conv_w[l], conv_b[l], dt_bias[l], a_log[l], d_skip[l],
                          ssd_norm_g[l])
        y_pool = _pool_call(xp, pool_w[l], pool_scale[l])
        y_sb = _sb_call(q, k, v)
        keys = peer_keys[l].reshape(-1, N_KEYS, PEER_HALF).astype(BF16)
        x1, h2, sct = _out_call(y_ssd, y_pool, y_sb, x, g1, sh2, sc2, ffn_norm_g[l].reshape(1, d),
                                w_out[l].astype(BF16), peer_wq[l].astype(BF16), keys)
        x = _peer_call(sct, h2, peer_u[l].astype(BF16), peer_v[l].astype(BF16), x1, g2,
                       final_norm_g.reshape(1, d), final=(l == depth - 1))
    return x
```

```python
import functools

import jax
import jax.numpy as jnp
from jax import lax
from jax.experimental import pallas as pl
from jax.experimental.pallas import tpu as pltpu

F32 = jnp.float32
BF16 = jnp.bfloat16
EPS = 1e-6
HIGHEST = lax.Precision.HIGHEST

SSD_HEAD_DIM = 64
SSD_GROUPS = 2
SSD_STATE = 64
CONV_WIDTH = 4
POOL_WINDOWS = (2, 4, 8, 16)
SB_HEAD_DIM = 64
PEER_HEADS = 8
PEER_TOPK = 16
N_KEYS = 128
PEER_HALF = 128

LANES = 128
SUBLANES = 8
MXU_WIDTH = 256
VMEM_LIMIT = 56 * 1024 * 1024

TM_IN = 512
TM_INPROJ = 1024
SSD_CHUNK = 256
SSD_SEQS = 2
HALO = 16
TM_POOL = 2048
SB_BLOCK = 128
SB_SKIP = 40.0
SB_QBLOCKS = 4
GATE_GROUP = 32
GATE_PITCH = 36
TB_PEER = 256
DENSE_SUB = 256


def _cparams(sem):
    return pltpu.CompilerParams(dimension_semantics=sem, vmem_limit_bytes=VMEM_LIMIT)


def _nt_dot(a, b):
    return lax.dot_general(a, b, (((1,), (1,)), ((), ())), preferred_element_type=F32)


def _softplus(x):
    return jnp.maximum(x, 0.0) + jnp.log(1.0 + jnp.exp(-jnp.abs(x)))


def _silu(x):
    return x * jax.nn.sigmoid(x)


def _mod_kernel(c_ref, w_ref, b_ref, o_ref):
    cond = _silu(c_ref[...])
    o_ref[0] = jnp.dot(cond, w_ref[0], preferred_element_type=F32, precision=HIGHEST) + b_ref[0]


def _mod_call(c, ada_w, ada_b):
    nl, d, n6 = ada_w.shape
    bsz = c.shape[0]
    tn = 1536
    return pl.pallas_call(
        _mod_kernel,
        grid=(nl, n6 // tn),
        in_specs=[pl.BlockSpec((bsz, d), lambda l, j: (0, 0)),
                  pl.BlockSpec((1, d, tn), lambda l, j: (l, 0, j)),
                  pl.BlockSpec((1, 1, tn), lambda l, j: (l, 0, j))],
        out_specs=pl.BlockSpec((1, bsz, tn), lambda l, j: (l, 0, j)),
        out_shape=jax.ShapeDtypeStruct((nl, bsz, n6), F32),
        compiler_params=_cparams(("arbitrary", "arbitrary")),
        name="adaln_mod",
    )(c, ada_w, ada_b.reshape(nl, 1, n6))


def _norm_mod(x, g, sh, sc):
    ms = jnp.mean(x * x, axis=-1, keepdims=True)
    y = x * lax.rsqrt(ms + EPS) * g
    return y * (1.0 + sc) + sh


def _in_kernel(x_ref, sh_ref, sc_ref, g_ref, w_ref, z_ref, xbc_ref, xp_ref, q_ref, k_ref, v_ref,
               dt_ref, *, cuts):
    h = _norm_mod(x_ref[0], g_ref[...], sh_ref[0], sc_ref[0]).astype(BF16)
    outs = (z_ref, xbc_ref, xp_ref, q_ref, k_ref, v_ref, dt_ref)
    for o_ref, (a, b) in zip(outs, cuts):
        o_ref[0] = jnp.dot(h, w_ref[:, a:b], preferred_element_type=F32).astype(o_ref.dtype)


def _in_call(x, sh, sc, g, w_cat, widths):
    bsz, s, d = x.shape
    tm = min(TM_INPROJ, s)
    cuts, a = [], 0
    for w in widths:
        cuts.append((a, a + w))
        a += w
    dtypes = (BF16, BF16, BF16, BF16, BF16, BF16, F32)
    tok = lambda w: pl.BlockSpec((1, tm, w), lambda b, i: (b, i, 0))
    vec = pl.BlockSpec((1, 1, d), lambda b, i: (b, 0, 0))
    return pl.pallas_call(
        functools.partial(_in_kernel, cuts=tuple(cuts)),
        grid=(bsz, s // tm),
        in_specs=[tok(d), vec, vec,
                  pl.BlockSpec((1, d), lambda b, i: (0, 0)),
                  pl.BlockSpec(w_cat.shape, lambda b, i: (0, 0))],
        out_specs=[tok(w) for w in widths],
        out_shape=[jax.ShapeDtypeStruct((bsz, s, w), dt) for w, dt in zip(widths, dtypes)],
        compiler_params=_cparams(("arbitrary", "arbitrary")),
        name="in_proj",
    )(x, sh, sc, g, w_cat)


def _ssd_kernel(xbc_ref, halo_ref, dt_ref, z_ref, cw_ref, cb_ref, dtb_ref, alog_ref, expand_ref,
                dsk_ref, ng_ref, y_ref, state_ref, *, chunk, width, gn):
    for b in range(xbc_ref.shape[0]):
        _ssd_chunk(xbc_ref.at[b:b + 1], halo_ref.at[b:b + 1], dt_ref.at[b:b + 1], z_ref.at[b:b + 1],
                   cw_ref, cb_ref, dtb_ref, alog_ref, expand_ref, dsk_ref, ng_ref,
                   y_ref.at[b:b + 1], state_ref.at[b], chunk=chunk, width=width, gn=gn)


def _ssd_chunk(xbc_ref, halo_ref, dt_ref, z_ref, cw_ref, cb_ref, dtb_ref, alog_ref, expand_ref,
               dsk_ref, ng_ref, y_ref, state_ref, *, chunk, width, gn):
    i = pl.program_id(1)
    L = chunk
    heads = width // SSD_HEAD_DIM
    hpg = heads // SSD_GROUPS
    half = width // SSD_GROUPS

    @pl.when(i == 0)
    def _():
        state_ref[...] = jnp.zeros_like(state_ref)

    hal = jnp.where(i > 0, halo_ref[0].astype(F32), 0.0)
    full = jnp.concatenate([hal, xbc_ref[0].astype(F32)], axis=0)
    cw = cw_ref[...]
    acc = full * cw[CONV_WIDTH - 1:CONV_WIDTH]
    for j in range(1, CONV_WIDTH):
        acc = acc + pltpu.roll(full, j, 0) * cw[CONV_WIDTH - 1 - j:CONV_WIDTH - j]
    xbc = _silu(acc[HALO:] + cb_ref[...])
    xs = xbc[:, :width]
    bm = xbc[:, width:width + gn]
    cm = xbc[:, width + gn:width + 2 * gn]

    dt = _softplus(dt_ref[0] + dtb_ref[...])
    a = dt * (-jnp.exp(alog_ref[...]))
    row_i = lax.broadcasted_iota(jnp.int32, (L, L), 0)
    col_i = lax.broadcasted_iota(jnp.int32, (L, L), 1)
    tril = row_i >= col_i
    a_cs = jnp.dot(tril.astype(F32), a, preferred_element_type=F32, precision=HIGHEST)
    a_cs_t = a_cs.T
    expand = expand_ref[...]
    dt_x = jnp.dot(dt, expand, preferred_element_type=F32, precision=HIGHEST)
    acs_x = jnp.dot(a_cs, expand, preferred_element_type=F32, precision=HIGHEST)
    alast_x = acs_x[L - 1:L, :]
    xdt = xs * dt_x

    bm_b = bm.astype(BF16)
    cm_b = cm.astype(BF16)
    lane_g = lax.broadcasted_iota(jnp.int32, (L, gn), 1) // SSD_STATE
    lane_h = lax.broadcasted_iota(jnp.int32, (L, half), 1) // SSD_HEAD_DIM

    state = state_ref[...]
    y_off = jnp.dot(cm_b, state.astype(BF16), preferred_element_type=F32) * jnp.exp(acs_x)

    y_halves = []
    for g in range(SSD_GROUPS):
        cb = _nt_dot(jnp.where(lane_g == g, cm_b, jnp.zeros_like(cm_b)), bm_b)
        xdt_g = xdt[:, g * half:(g + 1) * half]
        yh = jnp.zeros((L, half), F32)
        for hh in range(hpg):
            h = g * hpg + hh
            seg = a_cs[:, h:h + 1] - a_cs_t[h:h + 1, :]
            dec = jnp.exp(jnp.where(tril, seg, -1e30))
            m = (cb * dec).astype(BF16)
            rhs = jnp.where(lane_h == hh, xdt_g, 0.0).astype(BF16)
            yh = yh + jnp.dot(m, rhs, preferred_element_type=F32)
        y_halves.append(yh)

    ds_x = jnp.exp(alast_x - acs_x)
    upd = jnp.dot(bm.T.astype(BF16), (xdt * ds_x).astype(BF16), preferred_element_type=F32)
    srow = lax.broadcasted_iota(jnp.int32, (gn, width), 0) // SSD_STATE
    scol = lax.broadcasted_iota(jnp.int32, (gn, width), 1) // half
    state_ref[...] = jnp.exp(alast_x) * state + jnp.where(srow == scol, upd, 0.0)

    zg = _silu(z_ref[0].astype(F32))
    for g in range(SSD_GROUPS):
        sl = slice(g * half, (g + 1) * half)
        yg = (y_halves[g] + y_off[:, sl] + dsk_ref[:, sl] * xs[:, sl]) * zg[:, sl]
        ms = jnp.mean(yg * yg, axis=-1, keepdims=True)
        y_ref[0, :, sl] = (yg * lax.rsqrt(ms + EPS) * ng_ref[:, sl]).astype(y_ref.dtype)


def _ssd_call(xbc, dt, z, conv_w, conv_b, dt_bias, a_log, d_skip, ssd_norm_g):
    bsz, s, c = xbc.shape
    width = z.shape[-1]
    gn = SSD_GROUPS * SSD_STATE
    heads = width // SSD_HEAD_DIM
    L = min(SSD_CHUNK, s)
    pad = lambda v: jnp.zeros((1, LANES), F32).at[0, :heads].set(v)
    expand = (jnp.arange(LANES)[:, None] == (jnp.arange(width)[None, :] // SSD_HEAD_DIM)).astype(F32)
    dsk = jnp.repeat(d_skip, SSD_HEAD_DIM).reshape(1, width)
    hb = L // HALO
    const = lambda shp: pl.BlockSpec(shp, lambda b, i: (0, 0))
    nb = SSD_SEQS if bsz % SSD_SEQS == 0 else 1
    return pl.pallas_call(
        functools.partial(_ssd_kernel, chunk=L, width=width, gn=gn),
        grid=(bsz // nb, s // L),
        in_specs=[pl.BlockSpec((nb, L, c), lambda b, i: (b, i, 0)),
                  pl.BlockSpec((nb, HALO, c), lambda b, i: (b, jnp.maximum(i * hb - 1, 0), 0)),
                  pl.BlockSpec((nb, L, LANES), lambda b, i: (b, i, 0)),
                  pl.BlockSpec((nb, L, width), lambda b, i: (b, i, 0)),
                  const((CONV_WIDTH, c)), const((1, c)), const((1, LANES)), const((1, LANES)),
                  const((LANES, width)), const((1, width)), const((1, width))],
        out_specs=pl.BlockSpec((nb, L, width), lambda b, i: (b, i, 0)),
        out_shape=jax.ShapeDtypeStruct((bsz, s, width), BF16),
        scratch_shapes=[pltpu.VMEM((nb, gn, width), F32)],
        compiler_params=_cparams(("arbitrary", "arbitrary")),
        name="ssd_mixer",
    )(xbc, xbc, dt, z, conv_w, conv_b.reshape(1, c), pad(dt_bias), pad(a_log), expand, dsk,
      ssd_norm_g.reshape(1, width))


def _pool_kernel(xp_ref, halo_ref, w_ref, sc_ref, y_ref, *, tm, gdim):
    i = pl.program_id(1)
    hal = jnp.where(i > 0, halo_ref[0].astype(F32), 0.0)
    x = xp_ref[0].astype(F32)
    full = jnp.concatenate([hal, x], axis=0)
    lane_g = lax.broadcasted_iota(jnp.int32, x.shape, 1) // gdim
    tpos = i * tm + lax.broadcasted_iota(jnp.int32, x.shape, 0)
    win_sum = jnp.zeros_like(x)
    cnt = jnp.ones_like(x)
    s = full
    span = 1
    for gi, w in enumerate(POOL_WINDOWS):
        while span < w:
            s = s + pltpu.roll(s, span, 0)
            span *= 2
        win_sum = jnp.where(lane_g == gi, s[HALO:], win_sum)
        cnt = jnp.where(lane_g == gi, jnp.minimum(tpos + 1, w).astype(F32), cnt)
    pooled = (win_sum / cnt - x).astype(BF16)
    y = jnp.dot(pooled, w_ref[...], preferred_element_type=F32) * sc_ref[...]
    y_ref[0] = y.astype(y_ref.dtype)


def _pool_call(xp, pool_w, pool_scale):
    bsz, s, w = xp.shape
    ng, gdim, _ = pool_w.shape
    assert POOL_WINDOWS[-1] <= HALO and all(b == 2 * a for a, b in zip(POOL_WINDOWS, POOL_WINDOWS[1:]))
    tm = min(TM_POOL, s)
    wbd = jnp.zeros((w, w), F32)
    for g in range(ng):
        wbd = wbd.at[g * gdim:(g + 1) * gdim, g * gdim:(g + 1) * gdim].set(pool_w[g])
    hb = tm // HALO
    return pl.pallas_call(
        functools.partial(_pool_kernel, tm=tm, gdim=gdim),
        grid=(bsz, s // tm),
        in_specs=[pl.BlockSpec((1, tm, w), lambda b, i: (b, i, 0)),
                  pl.BlockSpec((1, HALO, w), lambda b, i: (b, jnp.maximum(i * hb - 1, 0), 0)),
                  pl.BlockSpec((w, w), lambda b, i: (0, 0)),
                  pl.BlockSpec((1, w), lambda b, i: (0, 0))],
        out_specs=pl.BlockSpec((1, tm, w), lambda b, i: (b, i, 0)),
        out_shape=jax.ShapeDtypeStruct((bsz, s, w), BF16),
        compiler_params=_cparams(("arbitrary", "arbitrary")),
        name="pool_mixer",
    )(xp, xp, wbd.astype(BF16), pool_scale.reshape(1, w))


def _sb_kernel(q_ref, k_ref, v_ref, y_ref, qm_ref, acc_ref, carry_ref, *, blk, heads, nq):
    i = pl.program_id(1)
    T = blk
    w = q_ref.shape[-1]
    lane_h = lax.broadcasted_iota(jnp.int32, (T, w), 1) // SB_HEAD_DIM
    for u in range(nq):
        q = q_ref[0, u * T:(u + 1) * T, :] * (SB_HEAD_DIM ** -0.5)
        for h in range(heads):
            qm_ref[u, h * T:(h + 1) * T, :] = jnp.where(lane_h == h, q, jnp.zeros_like(q))
    row_i = lax.broadcasted_iota(jnp.int32, (T, T), 0)
    col_i = lax.broadcasted_iota(jnp.int32, (T, T), 1)
    suffix = (row_i > col_i).astype(BF16)
    qrow = lax.broadcasted_iota(jnp.int32, (heads * T, T), 0) % T
    causal = lax.broadcasted_iota(jnp.int32, (heads * T, T), 1) < qrow

    def scores(u, j, mask):
        ks = k_ref[0, pl.ds(pl.multiple_of(j * T, T), T), :]
        z = _nt_dot(qm_ref[u], ks)
        lk = -_softplus(z)
        if mask is not None:
            lk = jnp.where(mask, lk, 0.0)
        hi = lk.astype(BF16)
        lo = (lk - hi.astype(F32)).astype(BF16)
        later = (jnp.dot(hi, suffix, preferred_element_type=F32)
                 + jnp.dot(lo, suffix, preferred_element_type=F32))
        return z, lk, later

    def weighted(j, z, lk, later, c, mask):
        vs = v_ref[0, pl.ds(pl.multiple_of(j * T, T), T), :]
        wgt = jnp.exp(z + lk + later + c)
        if mask is not None:
            wgt = jnp.where(mask, wgt, 0.0)
        pv = jnp.dot(wgt.astype(BF16), vs, preferred_element_type=F32)
        out = pv[0:T]
        for h in range(1, heads):
            out = jnp.where(lane_h == h, pv[h * T:(h + 1) * T], out)
        return out, c + jnp.sum(lk, axis=1, keepdims=True)

    def alive():
        return (jnp.max(carry_ref[...]) > -SB_SKIP).astype(jnp.int32)

    first = []
    for u in range(nq):
        qb = i * nq + u
        prev = jnp.maximum(qb - 1, 0)
        has_prev = None if u > 0 else jnp.broadcast_to(qb > 0, causal.shape)
        first.append((qb, prev, has_prev, scores(u, qb, causal), scores(u, prev, has_prev)))
    for u, (qb, prev, has_prev, sd, sp) in enumerate(first):
        out_d, c = weighted(qb, *sd, jnp.zeros((heads * T, 1), F32), causal)
        out_p, c = weighted(prev, *sp, c, has_prev)
        acc_ref[u] = out_d + out_p
        carry_ref[u] = c

    def cond(st):
        return jnp.logical_and(i * nq + nq - 3 - st[0] >= 0, st[1] > 0)

    def body(st):
        n = st[0]
        for u in range(nq):
            j = i * nq + u - 2 - n
            valid = jnp.broadcast_to(j >= 0, causal.shape)
            jj = jnp.maximum(j, 0)
            out, c = weighted(jj, *scores(u, jj, valid), carry_ref[u], valid)
            acc_ref[u] += out
            carry_ref[u] = c
        return n + 1, alive()

    lax.while_loop(cond, body, (0, alive()))
    for u in range(nq):
        y_ref[0, u * T:(u + 1) * T, :] = acc_ref[u].astype(y_ref.dtype)


def _sb_call(q, k, v):
    bsz, s, w = q.shape
    heads = w // SB_HEAD_DIM
    T = min(SB_BLOCK, s)
    nq = SB_QBLOCKS if (s // T) % SB_QBLOCKS == 0 else 1
    return pl.pallas_call(
        functools.partial(_sb_kernel, blk=T, heads=heads, nq=nq),
        grid=(bsz, s // (nq * T)),
        in_specs=[pl.BlockSpec((1, nq * T, w), lambda b, i: (b, i, 0)),
                  pl.BlockSpec((1, s, w), lambda b, i: (b, 0, 0)),
                  pl.BlockSpec((1, s, w), lambda b, i: (b, 0, 0))],
        out_specs=pl.BlockSpec((1, nq * T, w), lambda b, i: (b, i, 0)),
        out_shape=jax.ShapeDtypeStruct((bsz, s, w), BF16),
        scratch_shapes=[pltpu.VMEM((nq, heads * T, w), BF16),
                        pltpu.VMEM((nq, T, w), F32),
                        pltpu.VMEM((nq, heads * T, 1), F32)],
        compiler_params=_cparams(("arbitrary", "arbitrary")),
        name="stick_breaking",
    )(q, k, v)


def _out_kernel(ys_ref, yp_ref, yb_ref, x_ref, g1_ref, sh_ref, sc_ref, ng_ref, wo_ref, wq_ref,
                keys_ref, x1_ref, h2_ref, sct_ref, *, cuts):
    mix = jnp.zeros(x_ref.shape[1:], F32)
    for y_ref, (a, b) in zip((ys_ref, yp_ref, yb_ref), cuts):
        mix = mix + jnp.dot(y_ref[0], wo_ref[a:b, :], preferred_element_type=F32)
    x1 = x_ref[0] + g1_ref[0] * mix
    x1_ref[0] = x1
    h2 = _norm_mod(x1, ng_ref[...], sh_ref[0], sc_ref[0]).astype(BF16)
    h2_ref[0] = h2
    qb = jnp.dot(h2, wq_ref[...], preferred_element_type=F32).astype(BF16)
    tp = sct_ref.shape[-1]
    for lst in range(keys_ref.shape[0]):
        sc_t = _nt_dot(keys_ref[lst], qb[:, lst * PEER_HALF:(lst + 1) * PEER_HALF])
        for k in range(sct_ref.shape[0]):
            sct_ref[k, lst] = sc_t[:, k * tp:(k + 1) * tp]


def _out_call(ys, yp, yb, x, g1, sh, sc, ng, w_out, wq, keys):
    bsz, s, d = x.shape
    tm = min(TM_IN, s)
    nlist = keys.shape[0]
    cuts, a = [], 0
    for y in (ys, yp, yb):
        cuts.append((a, a + y.shape[-1]))
        a += y.shape[-1]
    tok = lambda w: pl.BlockSpec((1, tm, w), lambda b, i: (b, i, 0))
    vec = pl.BlockSpec((1, 1, d), lambda b, i: (b, 0, 0))
    full2 = lambda arr: pl.BlockSpec(arr.shape, lambda b, i: (0,) * arr.ndim)
    nblk = s // tm
    tp = min(TB_PEER, s)
    per = tm // tp
    return pl.pallas_call(
        functools.partial(_out_kernel, cuts=tuple(cuts)),
        grid=(bsz, nblk),
        in_specs=[tok(ys.shape[-1]), tok(yp.shape[-1]), tok(yb.shape[-1]), tok(d), vec, vec, vec,
                  pl.BlockSpec((1, d), lambda b, i: (0, 0)), full2(w_out), full2(wq), full2(keys)],
        out_specs=[tok(d), tok(d),
                   pl.BlockSpec((per, nlist, N_KEYS, tp), lambda b, i: (b * nblk + i, 0, 0, 0))],
        out_shape=[jax.ShapeDtypeStruct((bsz, s, d), F32),
                   jax.ShapeDtypeStruct((bsz, s, d), BF16),
                   jax.ShapeDtypeStruct((bsz * s // tp, nlist, N_KEYS, tp), F32)],
        compiler_params=_cparams(("arbitrary", "arbitrary")),
        name="out_proj_peer_query",
    )(ys, yp, yb, x, g1, sh, sc, ng, w_out, wq, keys)


def _top_rows(s, k, payload=None):
    nrows = s.shape[0]
    rows = lax.broadcasted_iota(jnp.int32, s.shape, 0).astype(F32)
    vals, picks = [], []
    for _ in range(k):
        m = jnp.max(s, axis=0, keepdims=True)
        idx = jnp.min(jnp.where(s == m, rows, float(nrows)), axis=0, keepdims=True)
        hit = rows == idx
        vals.append(m)
        if payload is None:
            picks.append(idx)
        else:
            picks.append(jnp.sum(jnp.where(hit, payload, 0.0), axis=0, keepdims=True))
        s = jnp.where(hit, -jnp.inf, s)
    return jnp.concatenate(vals, axis=0), jnp.concatenate(picks, axis=0)


def _pair_candidates(v1, x1, v2, x2):
    K = PEER_TOPK
    cands, ids = [], []
    a = 0
    while K // (a + 1) > 1:
        n = K // (a + 1)
        npad = -(-n // SUBLANES) * SUBLANES
        c = v1[a:a + 1, :] + v2[0:npad, :]
        if n < npad:
            c = jnp.where(lax.broadcasted_iota(jnp.int32, c.shape, 0) < n, c, -jnp.inf)
        cands.append(c)
        ids.append(x1[a:a + 1, :] * float(N_KEYS) + x2[0:npad, :])
        a += 1
    cands.append(v1[a:K, :] + v2[0:1, :])
    ids.append(x1[a:K, :] * float(N_KEYS) + x2[0:1, :])
    return jnp.concatenate(cands, axis=0), jnp.concatenate(ids, axis=0)


def _peer_kernel(sct_ref, h_ref, u_ref, v_ref, x_ref, g2_ref, fg_ref, o_ref,
                 pk_ref, pt_ref, scr_ref, g_ref, coef_ref, acc_ref, *, final):
    s = pl.program_id(0)
    j = pl.program_id(1)
    K = PEER_TOPK
    slot = s % 2
    eb = u_ref.shape[0]

    @pl.when(jnp.logical_and(s == 0, j == 0))
    def _():
        pk_ref[...] = jnp.zeros_like(pk_ref)
        scr_ref[...] = jnp.zeros_like(scr_ref)
        g_ref[...] = jnp.zeros_like(g_ref)
        coef_ref[...] = jnp.zeros_like(coef_ref)
        acc_ref[...] = jnp.zeros_like(acc_ref)

    @pl.when(j == 0)
    def _():
        for k in range(pk_ref.shape[0]):
            pt_ref[k] = pk_ref[k].T

    v1, x1 = _top_rows(sct_ref[0], K)
    v2, x2 = _top_rows(sct_ref[1], K)
    cand, ids = _pair_candidates(v1, x1, v2, x2)
    best, expert = _top_rows(cand, K, payload=ids)
    e = jnp.exp(best - best[0:1, :])
    gate = e / jnp.sum(e, axis=0, keepdims=True)
    i1 = jnp.floor(expert * (1.0 / N_KEYS))
    r0 = pl.multiple_of(j * K, K)
    pk_ref[0, pl.ds(r0, K), :] = i1
    pk_ref[1, pl.ds(r0, K), :] = expert - i1 * float(N_KEYS)
    pk_ref[2, pl.ds(r0, K), :] = 0.5 * gate

    npair = pt_ref.shape[-1]
    sub = lax.broadcasted_iota(jnp.int32, (N_KEYS, npair), 0).astype(F32)
    t0 = pl.multiple_of(j * GATE_GROUP, GATE_GROUP)
    for tl in range(GATE_GROUP):
        r = pl.ds(t0 + tl, 1)
        lhs = jnp.where(sub == pt_ref[0, r, :], pt_ref[2, r, :], 0.0).astype(BF16)
        rhs = jnp.where(sub == pt_ref[1, r, :], 1.0, 0.0).astype(BF16)
        scr_ref[pl.ds(tl, N_KEYS, stride=GATE_PITCH), :] = _nt_dot(lhs, rhs)
    for a in range(N_KEYS):
        g_ref[1 - slot, a, pl.ds(t0, GATE_GROUP), :] = (
            scr_ref[a * GATE_PITCH:a * GATE_PITCH + GATE_GROUP, :].astype(g_ref.dtype))

    h = h_ref[...]
    d = h.shape[-1]
    acc_ref[...] += jnp.dot(coef_ref[...], v_ref[:, :d], preferred_element_type=F32)

    a0 = j * (eb // N_KEYS)
    pieces = []
    for q in range(eb // DENSE_SUB):
        act = _nt_dot(h, u_ref[q * DENSE_SUB:(q + 1) * DENSE_SUB, :d])
        w = (act * (1.0 + lax.erf(act * (2.0 ** -0.5)))).astype(BF16)
        for a in range(DENSE_SUB // N_KEYS):
            ga = g_ref[slot, a0 + q * (DENSE_SUB // N_KEYS) + a]
            pieces.append(w[:, a * N_KEYS:(a + 1) * N_KEYS] * ga)
    coef_ref[...] = jnp.concatenate(pieces, axis=1)

    @pl.when(j == 0)
    def _():
        x2 = x_ref[...] + g2_ref[0] * acc_ref[...]
        if final:
            ms = jnp.mean(x2 * x2, axis=-1, keepdims=True)
            x2 = x2 * lax.rsqrt(ms + EPS) * fg_ref[...]
        o_ref[...] = x2
        acc_ref[...] = jnp.zeros_like(acc_ref)


def _peer_call(sct, h2, u, v, x1, g2, fg, final):
    bsz, s, d = x1.shape
    t = bsz * s
    ne = u.shape[0]
    _, nlist, nk, _ = sct.shape
    tb = min(TB_PEER, s)
    eb = ne // PEER_HEADS
    npair = PEER_HEADS * PEER_TOPK
    assert tb == PEER_HEADS * GATE_GROUP and nlist == 2 * PEER_HEADS and eb % MXU_WIDTH == 0
    nt = t // tb
    per_b = s // tb
    nchunk = PEER_HEADS
    dense_tile = lambda i: jnp.clip(i - 2, 0, nt - 1)
    row = lambda i, j: (dense_tile(i), 0)
    out_tile = lambda i, j: jnp.clip(i - 2 - (j == 0).astype(jnp.int32), 0, nt - 1)
    out_row = lambda i, j: (out_tile(i, j), 0)
    dp = d + LANES if (d // LANES) % 2 == 0 else d
    u = jnp.pad(u, ((0, 0), (0, dp - d)))
    v = jnp.pad(v, ((0, 0), (0, dp - d)))
    out = pl.pallas_call(
        functools.partial(_peer_kernel, final=final),
        grid=(nt + 3, nchunk),
        in_specs=[pl.BlockSpec((None, 2, nk, tb), lambda i, j: (jnp.minimum(i, nt - 1), j, 0, 0)),
                  pl.BlockSpec((tb, d), row),
                  pl.BlockSpec((eb, dp), lambda i, j: (j, 0)),
                  pl.BlockSpec((eb, dp), lambda i, j: ((j + nchunk - 1) % nchunk, 0)),
                  pl.BlockSpec((tb, d), out_row),
                  pl.BlockSpec((1, 1, d), lambda i, j: (out_tile(i, j) // per_b, 0, 0)),
                  pl.BlockSpec((1, d), lambda i, j: (0, 0))],
        out_specs=pl.BlockSpec((tb, d), out_row),
        out_shape=jax.ShapeDtypeStruct((t, d), F32),
        scratch_shapes=[pltpu.VMEM((3, npair, tb), F32),
                        pltpu.VMEM((3, tb, npair), F32),
                        pltpu.VMEM((N_KEYS * GATE_PITCH, N_KEYS), F32),
                        pltpu.VMEM((2, N_KEYS, tb, N_KEYS), BF16),
                        pltpu.VMEM((tb, eb), BF16),
                        pltpu.VMEM((tb, d), F32)],
        compiler_params=_cparams(("arbitrary", "arbitrary")),
        name="peer_fused",
    )(sct, h2.reshape(t, d), u, v, x1.reshape(t, d), g2, fg)
    return out.reshape(bsz, s, d)


def kernel(x, c, ada_w, ada_b, mix_norm_g, ffn_norm_g, w_in, conv_w, conv_b, dt_bias, a_log, d_skip,
           ssd_norm_g, pool_w, pool_scale, w_out, peer_wq, peer_keys, peer_u, peer_v, final_norm_g):
    bsz, s, d = x.shape
    depth = ada_w.shape[0]
    ssd_w = ssd_norm_g.shape[-1]
    heads = dt_bias.shape[-1]
    xbc_w = conv_w.shape[-1]
    pool_wd = pool_scale.shape[-1]
    sb_w = (w_in.shape[-1] - ssd_w - xbc_w - heads - pool_wd) // 3
    o_z, o_xbc, o_dt, o_xp = 0, ssd_w, ssd_w + xbc_w, ssd_w + xbc_w + heads
    o_q = o_xp + pool_wd
    widths = (ssd_w, xbc_w, pool_wd, sb_w, sb_w, sb_w, LANES)

    mods = _mod_call(c, ada_w, ada_b)
    for l in range(depth):
        mod = [mods[l, :, k * d:(k + 1) * d].reshape(bsz, 1, d) for k in range(6)]
        sh1, sc1, g1, sh2, sc2, g2 = mod
        wl = w_in[l]
        w_cat = jnp.concatenate(
            [wl[:, o_z:o_xbc], wl[:, o_xbc:o_dt], wl[:, o_xp:o_q], wl[:, o_q:],
             jnp.pad(wl[:, o_dt:o_xp], ((0, 0), (0, LANES - heads)))], axis=1).astype(BF16)
        z, xbc, xp, q, k, v, dt = _in_call(x, sh1, sc1, mix_norm_g[l].reshape(1, d), w_cat, widths)
        y_ssd = _ssd_call(xbc, dt, z, conv_w[l], conv_b[l], dt_bias[l], a_log[l], d_skip[l],
                          ssd_norm_g[l])
        y_pool = _pool_call(xp, pool_w[l], pool_scale[l])
        y_sb = _sb_call(q, k, v)
        keys = peer_keys[l].reshape(-1, N_KEYS, PEER_HALF).astype(BF16)
        x1, h2, sct = _out_call(y_ssd, y_pool, y_sb, x, g1, sh2, sc2, ffn_norm_g[l].reshape(1, d),
                                w_out[l].astype(BF16), peer_wq[l].astype(BF16), keys)
        x = _peer_call(sct, h2, peer_u[l].astype(BF16), peer_v[l].astype(BF16), x1, g2,
                       final_norm_g.reshape(1, d), final=(l == depth - 1))
    return x
```
